```python
import jax, jax.numpy as jnp
from jax import lax
import numpy as np

D_MODEL = 4096
BATCH = 4
SEQ = 4096
DEPTH = 1

EPS = 1e-6
PLE_DIM = 256
MLA_HEADS = 16
QK_NOPE = 128
QK_ROPE = 64
QK_HEAD = QK_NOPE + QK_ROPE
V_HEAD = 128
Q_LORA = 768
KV_LORA = 512
ROPE_THETA = 10000.0
Q_BLOCK = 128
HG_HEADS = 16
HG_EXPAND = 128
HG_HEAD_V = 128
HG_FDIM = HG_HEADS * HG_EXPAND
HG_VDIM = HG_HEADS * HG_HEAD_V
CHUNK = 64
MIX_WIDTH = MLA_HEADS * V_HEAD + HG_VDIM
D_FF = 4 * D_MODEL
IN_WIDTH = Q_LORA + KV_LORA + QK_ROPE + HG_FDIM + HG_FDIM + HG_VDIM + HG_VDIM
IN_SPLITS = (
    Q_LORA,
    Q_LORA + KV_LORA,
    Q_LORA + KV_LORA + QK_ROPE,
    Q_LORA + KV_LORA + QK_ROPE + HG_FDIM,
    Q_LORA + KV_LORA + QK_ROPE + 2 * HG_FDIM,
    Q_LORA + KV_LORA + QK_ROPE + 2 * HG_FDIM + HG_VDIM,
)

kernel_name = "hymba_mla_hgrn2_relu2_ple"


def rms_norm(x, g):
    xf = x.astype(jnp.float32)
    y = xf * lax.rsqrt(jnp.mean(xf * xf, axis=-1, keepdims=True) + EPS)
    return (y * g.astype(jnp.float32)).astype(x.dtype)


def rope_tables(positions):
    inv_freq = ROPE_THETA ** (-jnp.arange(0, QK_ROPE, 2, dtype=jnp.float32) / QK_ROPE)
    ang = positions.astype(jnp.float32)[..., None] * inv_freq
    return jnp.cos(ang), jnp.sin(ang)


def apply_rope(t, cos, sin):
    tf = t.astype(jnp.float32)
    t1, t2 = jnp.split(tf, 2, axis=-1)
    return jnp.concatenate([t1 * cos - t2 * sin, t2 * cos + t1 * sin], axis=-1).astype(t.dtype)


def causal_block_attention(q, k, v):
    B, S, H, _ = q.shape
    nb = S // Q_BLOCK
    scale = QK_HEAD ** -0.5
    qb = q.reshape(B, nb, Q_BLOCK, H, QK_HEAD).transpose(1, 0, 2, 3, 4)
    key_pos = jnp.arange(S)

    def one_block(args):
        qi, blk = args
        s = jnp.einsum('bqhd,bkhd->bhqk', qi, k, preferred_element_type=jnp.float32) * scale
        q_pos = blk * Q_BLOCK + jnp.arange(Q_BLOCK)
        s = jnp.where(key_pos[None, :] <= q_pos[:, None], s, -jnp.inf)
        pr = jax.nn.softmax(s, axis=-1).astype(v.dtype)
        return jnp.einsum('bhqk,bkhd->bqhd', pr, v)

    out = lax.map(one_block, (qb, jnp.arange(nb)))
    return out.transpose(1, 0, 2, 3, 4).reshape(B, S, H, V_HEAD)


def hgrn2_chunked(q, k, v, logf):
    B, S, H, dk = q.shape
    dv = v.shape[-1]
    nc = S // CHUNK

    def to_chunks(t):
        return t.reshape(B, nc, CHUNK, H, t.shape[-1]).transpose(1, 0, 3, 2, 4)

    causal = jnp.tril(jnp.ones((CHUNK, CHUNK), dtype=bool))

    def step(state, inp):
        qc, kc, vc, gc = inp
        b = jnp.cumsum(gc, axis=2)
        o_inter = jnp.einsum('bhtk,bhkv->bhtv', qc * jnp.exp(b), state)
        diff = b[:, :, :, None, :] - b[:, :, None, :, :]
        decay = jnp.exp(jnp.where(causal[:, :, None], diff, -jnp.inf))
        a = jnp.einsum('bhtk,bhtsk,bhsk->bhts', qc, decay, kc)
        o = o_inter + jnp.einsum('bhts,bhsv->bhtv', a, vc)
        b_last = b[:, :, -1:, :]
        new_state = jnp.exp(b_last[:, :, 0, :])[..., None] * state + jnp.einsum(
            'bhsk,bhsv->bhkv', kc * jnp.exp(b_last - b), vc)
        return new_state, o

    s0 = jnp.zeros((B, H, dk, dv), jnp.float32)
    _, o = lax.scan(step, s0, (to_chunks(q), to_chunks(k), to_chunks(v), to_chunks(logf)))
    return o.transpose(1, 0, 3, 2, 4).reshape(B, S, H, dv)


def setup_inputs(seed: int = 0) -> dict:
    key = jax.random.key(seed)
    ks = jax.random.split(key, 24)
    f32 = jnp.float32

    def w(k, shape, fan_in):
        return jax.random.normal(k, shape, f32) * (fan_in ** -0.5)

    def gain(k, shape):
        return 1.0 + 0.02 * jax.random.normal(k, shape, f32)

    x = jax.random.normal(ks[0], (BATCH, SEQ, D_MODEL), f32)
    p = jax.random.normal(ks[1], (DEPTH, BATCH, SEQ, PLE_DIM), f32)
    positions = (jax.random.randint(ks[2], (BATCH, 1), 0, 1024, jnp.int32)
                 + jnp.arange(SEQ, dtype=jnp.int32)[None, :])
    return {
        "x": x,
        "p": p,
        "positions": positions,
        "norm_mix": gain(ks[3], (DEPTH, D_MODEL)),
        "w_in": w(ks[4], (DEPTH, D_MODEL, IN_WIDTH), D_MODEL),
        "q_a_norm": gain(ks[5], (DEPTH, Q_LORA)),
        "kv_a_norm": gain(ks[6], (DEPTH, KV_LORA)),
        "w_uq": w(ks[7], (DEPTH, Q_LORA, MLA_HEADS * QK_HEAD), Q_LORA),
        "w_ukv": w(ks[8], (DEPTH, KV_LORA, MLA_HEADS * (QK_NOPE + V_HEAD)), KV_LORA),
        "hg_lower_bound": 0.5 * jax.random.normal(ks[9], (DEPTH + 1, HG_FDIM), f32),
        "hg_out_norm": gain(ks[10], (DEPTH, HG_VDIM)),
        "w_o": w(ks[11], (DEPTH, MIX_WIDTH, D_MODEL), MIX_WIDTH),
        "norm_mlp": gain(ks[12], (DEPTH, D_MODEL)),
        "w_up": w(ks[13], (DEPTH, D_MODEL, D_FF), D_MODEL),
        "w_down": w(ks[14], (DEPTH, D_FF, D_MODEL), D_FF),
        "norm_ple": gain(ks[15], (DEPTH, D_MODEL)),
        "w_ple_gate": w(ks[16], (DEPTH, D_MODEL, D_MODEL), D_MODEL),
        "w_ple": w(ks[17], (DEPTH, PLE_DIM, D_MODEL), PLE_DIM),
        "ple_post_norm": gain(ks[18], (DEPTH, D_MODEL)),
        "final_norm": gain(ks[19], (D_MODEL,)),
    }


def reference(x, p, positions, norm_mix, w_in, q_a_norm, kv_a_norm, w_uq, w_ukv,
              hg_lower_bound, hg_out_norm, w_o, norm_mlp, w_up, w_down,
              norm_ple, w_ple_gate, w_ple, ple_post_norm, final_norm):
    B, S, _ = x.shape
    cos, sin = rope_tables(positions)
    lb_all = jnp.cumsum(jax.nn.softmax(hg_lower_bound.astype(jnp.float32), axis=0), axis=0)
    h = x
    for i in range(DEPTH):
        u = rms_norm(h, norm_mix[i])
        proj = u @ w_in[i]
        c_q, c_kv, k_r, hq, hf, hi, hg = jnp.split(proj, IN_SPLITS, axis=-1)

        q = (rms_norm(c_q, q_a_norm[i]) @ w_uq[i]).reshape(B, S, MLA_HEADS, QK_HEAD)
        q_nope, q_rope = jnp.split(q, [QK_NOPE], axis=-1)
        q = jnp.concatenate([q_nope, apply_rope(q_rope, cos[:, :, None], sin[:, :, None])], axis=-1)
        kv = (rms_norm(c_kv, kv_a_norm[i]) @ w_ukv[i]).reshape(B, S, MLA_HEADS, QK_NOPE + V_HEAD)
        k_nope, v = jnp.split(kv, [QK_NOPE], axis=-1)
        k_rope = apply_rope(k_r, cos, sin)
        k = jnp.concatenate(
            [k_nope, jnp.broadcast_to(k_rope[:, :, None, :], (B, S, MLA_HEADS, QK_ROPE))], axis=-1)
        o_mla = causal_block_attention(q, k, v).reshape(B, S, MLA_HEADS * V_HEAD)

        lb = lb_all[i]
        zf = hf.astype(jnp.float32)
        f = lb + (1.0 - lb) * jax.nn.sigmoid(zf)
        k_h = ((1.0 - lb) * jax.nn.sigmoid(-zf)).reshape(B, S, HG_HEADS, HG_EXPAND)
        logf = jnp.log(f).reshape(B, S, HG_HEADS, HG_EXPAND)
        q_h = jax.nn.silu(hq.astype(jnp.float32)).reshape(B, S, HG_HEADS, HG_EXPAND)
        v_h = hi.astype(jnp.float32).reshape(B, S, HG_HEADS, HG_HEAD_V)
        o_h = hgrn2_chunked(q_h, k_h, logf=logf, v=v_h)
        o_h = rms_norm(o_h, hg_out_norm[i].reshape(HG_HEADS, HG_HEAD_V))
        o_h = o_h * jax.nn.silu(hg.astype(jnp.float32)).reshape(B, S, HG_HEADS, HG_HEAD_V)
        o_h = o_h.reshape(B, S, HG_VDIM).astype(x.dtype)

        h = h + jnp.concatenate([o_mla, o_h], axis=-1) @ w_o[i]

        hidden = jnp.square(jax.nn.relu(rms_norm(h, norm_mlp[i]) @ w_up[i]))
        h = h + hidden @ w_down[i]

        gate = jax.nn.sigmoid(rms_norm(h, norm_ple[i]) @ w_ple_gate[i])
        e = rms_norm(p[i] @ w_ple[i], ple_post_norm[i])
        h = h + gate * e
    return rms_norm(h, final_norm)
```

```python
import functools

import numpy as np
import jax
import jax.numpy as jnp
from jax import lax
from jax.experimental import pallas as pl
from jax.experimental.pallas import tpu as pltpu

EPS = 1e-6
MLA_HEADS = 16
QK_NOPE = 128
QK_ROPE = 64
QK_HEAD = QK_NOPE + QK_ROPE
V_HEAD = 128
Q_LORA = 768
KV_LORA = 512
ROPE_THETA = 10000.0
HG_HEADS = 16
HG_DK = 128
HG_DV = 128
HG_FDIM = HG_HEADS * HG_DK
HG_VDIM = HG_HEADS * HG_DV

LANES = 128
QK_PAD = 2 * LANES
A_WIDTH = Q_LORA + KV_LORA + 2 * LANES
HG_CHUNK = 128
VMEM_LIMIT_BYTES = 56 * 1024 * 1024

F32 = jnp.float32
BF16 = jnp.bfloat16


def _params(*semantics):
    return pltpu.CompilerParams(dimension_semantics=semantics, vmem_limit_bytes=VMEM_LIMIT_BYTES)


def _rms(x, g):
    return x * lax.rsqrt(jnp.mean(x * x, axis=-1, keepdims=True) + EPS) * g


def _rmsnorm_body(x_ref, g_ref, o_ref):
    o_ref[...] = _rms(x_ref[...], g_ref[...]).astype(o_ref.dtype)


def rmsnorm(x, g, out_dtype, tm=256):
    n, d = x.shape
    return pl.pallas_call(
        _rmsnorm_body,
        grid=(n // tm,),
        in_specs=[pl.BlockSpec((tm, d), lambda i: (i, 0)), pl.BlockSpec((1, d), lambda i: (0, 0))],
        out_specs=pl.BlockSpec((tm, d), lambda i: (i, 0)),
        out_shape=jax.ShapeDtypeStruct((n, d), out_dtype),
        compiler_params=_params("parallel"),
        name="rmsnorm",
    )(x, g.reshape(1, d))


def _mm_body(a_ref, w_ref, o_ref, *, relu2):
    acc = jnp.dot(a_ref[...], w_ref[...], preferred_element_type=F32)
    if relu2:
        acc = jnp.square(jnp.maximum(acc, 0.0))
    o_ref[...] = acc.astype(o_ref.dtype)


def matmul(a, w, out_dtype, relu2=False, tm=1024, tn=512):
    m, k = a.shape
    n = w.shape[1]
    return pl.pallas_call(
        functools.partial(_mm_body, relu2=relu2),
        grid=(m // tm, n // tn),
        in_specs=[pl.BlockSpec((tm, k), lambda i, j: (i, 0)), pl.BlockSpec((k, tn), lambda i, j: (0, j))],
        out_specs=pl.BlockSpec((tm, tn), lambda i, j: (i, j)),
        out_shape=jax.ShapeDtypeStruct((m, n), out_dtype),
        compiler_params=_params("parallel", "parallel"),
        name="matmul_relu2" if relu2 else "matmul",
    )(a, w)


def _mm2_res_body(a1_ref, a2_ref, w1_ref, w2_ref, r_ref, o_ref):
    acc = jnp.dot(a1_ref[...], w1_ref[...], preferred_element_type=F32)
    acc = acc + jnp.dot(a2_ref[...], w2_ref[...], preferred_element_type=F32)
    o_ref[...] = r_ref[...] + acc


def matmul2_residual(a1, a2, w1, w2, res, tm=1024, tn=512):
    m, k1 = a1.shape
    k2 = a2.shape[1]
    n = w1.shape[1]
    return pl.pallas_call(
        _mm2_res_body,
        grid=(m // tm, n // tn),
        in_specs=[pl.BlockSpec((tm, k1), lambda i, j: (i, 0)), pl.BlockSpec((tm, k2), lambda i, j: (i, 0)),
                  pl.BlockSpec((k1, tn), lambda i, j: (0, j)), pl.BlockSpec((k2, tn), lambda i, j: (0, j)),
                  pl.BlockSpec((tm, tn), lambda i, j: (i, j))],
        out_specs=pl.BlockSpec((tm, tn), lambda i, j: (i, j)),
        out_shape=jax.ShapeDtypeStruct((m, n), F32),
        compiler_params=_params("parallel", "parallel"),
        name="out_proj",
    )(a1, a2, w1, w2, res)


def _mmk_res_body(a_ref, w_ref, r_ref, o_ref, acc_ref):
    k = pl.program_id(2)

    @pl.when(k == 0)
    def _():
        acc_ref[...] = jnp.zeros_like(acc_ref)

    acc_ref[...] += jnp.dot(a_ref[...], w_ref[...], preferred_element_type=F32)

    @pl.when(k == pl.num_programs(2) - 1)
    def _():
        o_ref[...] = r_ref[...] + acc_ref[...]


def matmul_ktiled_residual(a, w, res, tm=1024, tn=1024, tk=2048):
    m, k = a.shape
    n = w.shape[1]
    return pl.pallas_call(
        _mmk_res_body,
        grid=(m // tm, n // tn, k // tk),
        in_specs=[pl.BlockSpec((tm, tk), lambda i, j, kk: (i, kk)), pl.BlockSpec((tk, tn), lambda i, j, kk: (kk, j)),
                  pl.BlockSpec((tm, tn), lambda i, j, kk: (i, j))],
        out_specs=pl.BlockSpec((tm, tn), lambda i, j, kk: (i, j)),
        out_shape=jax.ShapeDtypeStruct((m, n), F32),
        scratch_shapes=[pltpu.VMEM((tm, tn), F32)],
        compiler_params=_params("parallel", "parallel", "arbitrary"),
        name="mlp_down",
    )(a, w, res)


def _gate_body(a_ref, w_ref, h_ref, e_ref, o_ref):
    acc = jnp.dot(a_ref[...], w_ref[...], preferred_element_type=F32)
    o_ref[...] = h_ref[...] + jax.nn.sigmoid(acc) * e_ref[...]


def gate_residual(a, w, h, e, tm=1024, tn=512):
    m, k = a.shape
    n = w.shape[1]
    return pl.pallas_call(
        _gate_body,
        grid=(m // tm, n // tn),
        in_specs=[pl.BlockSpec((tm, k), lambda i, j: (i, 0)), pl.BlockSpec((k, tn), lambda i, j: (0, j)),
                  pl.BlockSpec((tm, tn), lambda i, j: (i, j)), pl.BlockSpec((tm, tn), lambda i, j: (i, j))],
        out_specs=pl.BlockSpec((tm, tn), lambda i, j: (i, j)),
        out_shape=jax.ShapeDtypeStruct((m, n), F32),
        compiler_params=_params("parallel", "parallel"),
        name="ple_gate",
    )(a, w, h, e)


def _embed_body(p_ref, w_ref, g_ref, o_ref):
    y = jnp.dot(p_ref[...].astype(BF16), w_ref[...], preferred_element_type=F32)
    o_ref[...] = _rms(y, g_ref[...])


def embed_norm(p, w, g, tm=256):
    m, k = p.shape
    n = w.shape[1]
    return pl.pallas_call(
        _embed_body,
        grid=(m // tm,),
        in_specs=[pl.BlockSpec((tm, k), lambda i: (i, 0)), pl.BlockSpec((k, n), lambda i: (0, 0)),
                  pl.BlockSpec((1, n), lambda i: (0, 0))],
        out_specs=pl.BlockSpec((tm, n), lambda i: (i, 0)),
        out_shape=jax.ShapeDtypeStruct((m, n), F32),
        compiler_params=_params("parallel"),
        name="ple_embed",
    )(p, w, g.reshape(1, n))


def _mla_prep_body(pa_ref, pos_ref, gq_ref, gkv_ref, wq_ref, wkv_ref, freq_ref, sign_ref, q_ref, k_ref, v_ref):
    pa = pa_ref[...]
    cq = _rms(pa[:, :Q_LORA], gq_ref[...]).astype(BF16)
    ckv = _rms(pa[:, Q_LORA:Q_LORA + KV_LORA], gkv_ref[...]).astype(BF16)
    kr = pa[:, Q_LORA + KV_LORA:Q_LORA + KV_LORA + LANES]
    krs = pa[:, Q_LORA + KV_LORA + LANES:]
    ang = pos_ref[...].astype(F32) * freq_ref[...]
    cos = jnp.cos(ang)
    sin = jnp.sin(ang) * sign_ref[...]
    q = jnp.dot(cq, wq_ref[...], preferred_element_type=F32)
    kv = jnp.dot(ckv, wkv_ref[...], preferred_element_type=F32)
    krot = (kr * cos + krs * sin).astype(BF16)
    hn = MLA_HEADS * LANES
    for h in range(MLA_HEADS):
        lo, hi = h * LANES, (h + 1) * LANES
        qrot = q[:, hn + lo:hn + hi] * cos + q[:, 2 * hn + lo:2 * hn + hi] * sin
        q_ref[:, h * QK_PAD:h * QK_PAD + LANES] = q[:, lo:hi].astype(BF16)
        q_ref[:, h * QK_PAD + LANES:(h + 1) * QK_PAD] = qrot.astype(BF16)
        k_ref[:, h * QK_PAD:h * QK_PAD + LANES] = kv[:, lo:hi].astype(BF16)
        k_ref[:, h * QK_PAD + LANES:(h + 1) * QK_PAD] = krot
    v_ref[...] = kv[:, hn:].astype(BF16)


def mla_prep(proj, pos, gq, gkv, wq, wkv, freq, sign, tm=256):
    n = proj.shape[0]
    const = lambda i: (0, 0)
    return pl.pallas_call(
        _mla_prep_body,
        grid=(n // tm,),
        in_specs=[pl.BlockSpec((tm, A_WIDTH), lambda i: (i, 0)), pl.BlockSpec((tm, 1), lambda i: (i, 0)),
                  pl.BlockSpec((1, Q_LORA), const), pl.BlockSpec((1, KV_LORA), const),
                  pl.BlockSpec(wq.shape, const), pl.BlockSpec(wkv.shape, const),
                  pl.BlockSpec((1, LANES), const), pl.BlockSpec((1, LANES), const)],
        out_specs=[pl.BlockSpec((tm, MLA_HEADS * QK_PAD), lambda i: (i, 0)),
                   pl.BlockSpec((tm, MLA_HEADS * QK_PAD), lambda i: (i, 0)),
                   pl.BlockSpec((tm, MLA_HEADS * V_HEAD), lambda i: (i, 0))],
        out_shape=[jax.ShapeDtypeStruct((n, MLA_HEADS * QK_PAD), BF16),
                   jax.ShapeDtypeStruct((n, MLA_HEADS * QK_PAD), BF16),
                   jax.ShapeDtypeStruct((n, MLA_HEADS * V_HEAD), BF16)],
        compiler_params=_params("parallel"),
        name="mla_prep",
    )(proj, pos, gq.reshape(1, -1), gkv.reshape(1, -1), wq, wkv, freq, sign)


def _attn_body(q_ref, k_ref, v_ref, o_ref, *, tq, tk, scale):
    qi = pl.program_id(2)
    q = q_ref[...]

    def step(j, carry, masked):
        m, l, acc = carry
        start = pl.multiple_of(j * tk, tk)
        k = k_ref[pl.ds(start, tk), :]
        v = v_ref[pl.ds(start, tk), :]
        s = lax.dot_general(q, k, (((1,), (1,)), ((), ())), preferred_element_type=F32) * scale
        if masked:
            row = qi * tq + lax.broadcasted_iota(jnp.int32, (tq, tk), 0)
            col = j * tk + lax.broadcasted_iota(jnp.int32, (tq, tk), 1)
            s = jnp.where(col <= row, s, -jnp.inf)
        m_new = jnp.maximum(m, jnp.max(s, axis=-1, keepdims=True))
        alpha = jnp.exp(m - m_new)
        p = jnp.exp(s - m_new)
        l = alpha * l + jnp.sum(p, axis=-1, keepdims=True)
        acc = alpha * acc + jnp.dot(p.astype(BF16), v, preferred_element_type=F32)
        return m_new, l, acc

    init = (jnp.full((tq, 1), -jnp.inf, F32), jnp.zeros((tq, 1), F32), jnp.zeros((tq, V_HEAD), F32))
    n_full = qi * (tq // tk)
    carry = lax.fori_loop(0, n_full, lambda j, c: step(j, c, False), init)
    for d in range(tq // tk):
        carry = step(n_full + d, carry, True)
    _, l, acc = carry
    o_ref[...] = (acc / l).astype(o_ref.dtype)


def causal_attention(q, k, v, batch, seq, tq=256, tk=256):
    nq = seq // tq
    return pl.pallas_call(
        functools.partial(_attn_body, tq=tq, tk=tk, scale=QK_HEAD ** -0.5),
        grid=(batch, MLA_HEADS, nq),
        in_specs=[pl.BlockSpec((tq, QK_PAD), lambda b, h, i: (b * nq + i, h)),
                  pl.BlockSpec((seq, QK_PAD), lambda b, h, i: (b, h)),
                  pl.BlockSpec((seq, V_HEAD), lambda b, h, i: (b, h))],
        out_specs=pl.BlockSpec((tq, V_HEAD), lambda b, h, i: (b * nq + i, h)),
        out_shape=jax.ShapeDtypeStruct((batch * seq, MLA_HEADS * V_HEAD), BF16),
        compiler_params=_params("parallel", "parallel", "arbitrary"),
        name="mla_attention",
    )(q, k, v)


def _hgrn_constants(c):
    t = np.arange(c)
    blocks = [(t[None, :] <= t[:, None]).astype(np.float32)]
    nlev = int(np.log2(c))
    half = c // 2
    while half >= 1:
        w = np.zeros((c, c), np.float32)
        for r in range(c):
            ref = (r // (2 * half)) * 2 * half + half - 1
            if r & half:
                w[r, ref + 1:r + 1] = 1.0
            else:
                w[r, r + 1:ref + 1] = 1.0
        blocks.append(w)
        half //= 2
    blocks.append((t[None, :] > t[:, None]).astype(np.float32))
    wmat = np.concatenate(blocks, axis=0)
    x = t[:, None] ^ t[None, :]
    level = np.full((c, c), -1, np.int32)
    lower = t[:, None] > t[None, :]
    level[lower] = (nlev - 1) - np.floor(np.log2(x[lower])).astype(np.int32)
    level[t, t] = nlev
    return np.concatenate([wmat, wmat, wmat], axis=1), level, nlev


def _hgrn_body(hq_ref, hf_ref, hi_ref, hg_ref, lbp_ref, gn_ref, w3_ref, lev_ref, o_ref, st_ref, *, layer, tile, chunk, nlev):
    @pl.when(pl.program_id(2) == 0)
    def _():
        st_ref[...] = jnp.zeros_like(st_ref)

    hb = lbp_ref[...]
    ex = jnp.exp(hb - jnp.max(hb, axis=0, keepdims=True))
    sm = ex / jnp.sum(ex, axis=0, keepdims=True)
    lb = jnp.sum(sm[:layer + 1], axis=0, keepdims=True)
    oml = 1.0 - lb
    lev = lev_ref[...]
    gn = gn_ref[...]
    state = st_ref[...]
    trans_b = (((1,), (1,)), ((), ()))
    for c in range(tile // chunk):
        rows = slice(c * chunk, (c + 1) * chunk)
        z = hf_ref[rows, :]
        g = jnp.log(lb + oml * jax.nn.sigmoid(z))
        kk = oml * jax.nn.sigmoid(-z)
        hq = hq_ref[rows, :]
        q = hq * jax.nn.sigmoid(hq)
        v = hi_ref[rows, :].astype(BF16)
        g_hi = g.astype(BF16)
        r1 = g - g_hi.astype(F32)
        g_mid = r1.astype(BF16)
        g_lo = (r1 - g_mid.astype(F32)).astype(BF16)
        g3 = jnp.concatenate([g_hi, g_mid, g_lo], axis=0)
        e_all = jnp.dot(w3_ref[...], g3, preferred_element_type=F32)
        b = e_all[:chunk]
        o = lax.dot_general((q * jnp.exp(b)).astype(BF16), state.astype(BF16), trans_b, preferred_element_type=F32)
        a = jnp.zeros((chunk, chunk), F32)
        for lv in range(nlev):
            e = jnp.exp(e_all[(lv + 1) * chunk:(lv + 2) * chunk])
            p = lax.dot_general((q * e).astype(BF16), (kk * e).astype(BF16), trans_b, preferred_element_type=F32)
            a = jnp.where(lev == lv, p, a)
        p = lax.dot_general(q.astype(BF16), kk.astype(BF16), trans_b, preferred_element_type=F32)
        a = jnp.where(lev == nlev, p, a)
        o = o + jnp.dot(a.astype(BF16), v, preferred_element_type=F32)
        kh = (kk * jnp.exp(e_all[(nlev + 1) * chunk:])).astype(BF16)
        state = jnp.exp(b[chunk - 1:chunk, :]) * state + lax.dot_general(
            v, kh, (((0,), (0,)), ((), ())), preferred_element_type=F32)
        hg = hg_ref[rows, :]
        y = _rms(o, gn) * (hg * jax.nn.sigmoid(hg))
        o_ref[rows, :] = y.astype(o_ref.dtype)
    st_ref[...] = state


def hgrn2(proj, col0, lbp, gn, batch, seq, layer, tile=512):
    w3, level, nlev = _hgrn_constants(HG_CHUNK)
    w3 = jnp.asarray(w3, BF16)
    level = jnp.asarray(level)
    nt = seq // tile
    cb = col0 // LANES

    def col(group):
        return lambda b, h, i: (b * nt + i, cb + group * HG_HEADS + h)

    return pl.pallas_call(
        functools.partial(_hgrn_body, layer=layer, tile=tile, chunk=HG_CHUNK, nlev=nlev),
        grid=(batch, HG_HEADS, nt),
        in_specs=[pl.BlockSpec((tile, LANES), col(0)), pl.BlockSpec((tile, LANES), col(1)),
                  pl.BlockSpec((tile, LANES), col(2)), pl.BlockSpec((tile, LANES), col(3)),
                  pl.BlockSpec((lbp.shape[0], LANES), lambda b, h, i: (0, h)),
                  pl.BlockSpec((1, LANES), lambda b, h, i: (0, h)),
                  pl.BlockSpec(w3.shape, lambda b, h, i: (0, 0)),
                  pl.BlockSpec(level.shape, lambda b, h, i: (0, 0))],
        out_specs=pl.BlockSpec((tile, LANES), lambda b, h, i: (b * nt + i, h)),
        out_shape=jax.ShapeDtypeStruct((batch * seq, HG_VDIM), BF16),
        scratch_shapes=[pltpu.VMEM((HG_DV, HG_DK), F32)],
        compiler_params=_params("parallel", "parallel", "arbitrary"),
        name="hgrn2",
    )(proj, proj, proj, proj, lbp, gn.reshape(1, -1), w3, level)


def _swap_halves(w):
    half = w.shape[-1] // 2
    return jnp.concatenate([w[..., half:], w[..., :half]], axis=-1)


def _layer_weights(w_in, w_uq, w_ukv):
    d = w_in.shape[0]
    n_a = Q_LORA + KV_LORA
    kr = w_in[:, n_a:n_a + QK_ROPE]
    zero = jnp.zeros((d, LANES - QK_ROPE), w_in.dtype)
    w_cat = jnp.concatenate([w_in[:, :n_a + QK_ROPE], zero, _swap_halves(kr), zero, w_in[:, n_a + QK_ROPE:]], axis=1)
    wq = w_uq.reshape(Q_LORA, MLA_HEADS, QK_HEAD)
    rope = wq[:, :, QK_NOPE:]
    pad = jnp.zeros((Q_LORA, MLA_HEADS, LANES - QK_ROPE), w_uq.dtype)
    wq_all = jnp.concatenate([
        wq[:, :, :QK_NOPE].reshape(Q_LORA, -1),
        jnp.concatenate([rope, pad], axis=-1).reshape(Q_LORA, -1),
        jnp.concatenate([_swap_halves(rope), pad], axis=-1).reshape(Q_LORA, -1)], axis=1)
    wkv = w_ukv.reshape(KV_LORA, MLA_HEADS, QK_NOPE + V_HEAD)
    wkv_all = jnp.concatenate([wkv[:, :, :QK_NOPE].reshape(KV_LORA, -1), wkv[:, :, QK_NOPE:].reshape(KV_LORA, -1)], axis=1)
    return w_cat.astype(BF16), wq_all.astype(BF16), wkv_all.astype(BF16)


def _rope_rows():
    inv_freq = ROPE_THETA ** (-jnp.arange(0, QK_ROPE, 2, dtype=F32) / QK_ROPE)
    zero = jnp.zeros((LANES - QK_ROPE,), F32)
    freq = jnp.concatenate([inv_freq, inv_freq, zero]).reshape(1, LANES)
    half = jnp.ones((QK_ROPE // 2,), F32)
    sign = jnp.concatenate([-half, half, zero]).reshape(1, LANES)
    return freq, sign


def kernel(x, p, positions, norm_mix, w_in, q_a_norm, kv_a_norm, w_uq, w_ukv, hg_lower_bound, hg_out_norm, w_o,
           norm_mlp, w_up, w_down, norm_ple, w_ple_gate, w_ple, ple_post_norm, final_norm):
    batch, seq, d_model = x.shape
    n = batch * seq
    depth = w_in.shape[0]
    h = x.reshape(n, d_model)
    pos = positions.reshape(n, 1)
    freq, sign = _rope_rows()
    mla_width = MLA_HEADS * V_HEAD
    for i in range(depth):
        w_cat, wq_all, wkv_all = _layer_weights(w_in[i], w_uq[i], w_ukv[i])
        u = rmsnorm(h, norm_mix[i], BF16)
        proj = matmul(u, w_cat, F32)
        q, k, v = mla_prep(proj, pos, q_a_norm[i], kv_a_norm[i], wq_all, wkv_all, freq, sign)
        o_mla = causal_attention(q, k, v, batch, seq)
        o_hg = hgrn2(proj, A_WIDTH, hg_lower_bound, hg_out_norm[i], batch, seq, i)
        wo = w_o[i].astype(BF16)
        h = matmul2_residual(o_mla, o_hg, wo[:mla_width], wo[mla_width:], h)
        u = rmsnorm(h, norm_mlp[i], BF16)
        hidden = matmul(u, w_up[i].astype(BF16), BF16, relu2=True)
        h = matmul_ktiled_residual(hidden, w_down[i].astype(BF16), h)
        u = rmsnorm(h, norm_ple[i], BF16)
        e = embed_norm(p[i].reshape(n, -1), w_ple[i].astype(BF16), ple_post_norm[i])
        h = gate_residual(u, w_ple_gate[i].astype(BF16), h, e)
    return rmsnorm(h, final_norm, x.dtype).reshape(batch, seq, d_model)
```

```python
import functools

import numpy as np
import jax
import jax.numpy as jnp
from jax import lax
from jax.experimental import pallas as pl
from jax.experimental.pallas import tpu as pltpu

EPS = 1e-6
MLA_HEADS = 16
QK_NOPE = 128
QK_ROPE = 64
QK_HEAD = QK_NOPE + QK_ROPE
V_HEAD = 128
Q_LORA = 768
KV_LORA = 512
ROPE_THETA = 10000.0
HG_HEADS = 16
HG_DK = 128
HG_DV = 128
HG_FDIM = HG_HEADS * HG_DK
HG_VDIM = HG_HEADS * HG_DV

LANES = 128
QK_PAD = 2 * LANES
A_WIDTH = Q_LORA + KV_LORA + 2 * LANES
HG_CHUNK = 128
VMEM_LIMIT_BYTES = 56 * 1024 * 1024
LOG2_E = 1.4426950408889634

F32 = jnp.float32
BF16 = jnp.bfloat16


def _params(*semantics):
    return pltpu.CompilerParams(dimension_semantics=semantics, vmem_limit_bytes=VMEM_LIMIT_BYTES)


def _rms(x, g):
    return x * lax.rsqrt(jnp.mean(x * x, axis=-1, keepdims=True) + EPS) * g


def _rmsnorm_body(x_ref, g_ref, o_ref):
    o_ref[...] = _rms(x_ref[...], g_ref[...]).astype(o_ref.dtype)


def rmsnorm(x, g, out_dtype, tm=256):
    n, d = x.shape
    return pl.pallas_call(
        _rmsnorm_body,
        grid=(n // tm,),
        in_specs=[pl.BlockSpec((tm, d), lambda i: (i, 0)), pl.BlockSpec((1, d), lambda i: (0, 0))],
        out_specs=pl.BlockSpec((tm, d), lambda i: (i, 0)),
        out_shape=jax.ShapeDtypeStruct((n, d), out_dtype),
        compiler_params=_params("parallel"),
        name="rmsnorm",
    )(x, g.reshape(1, d))


def _mm_body(a_ref, w_ref, o_ref, *, relu2):
    acc = jnp.dot(a_ref[...], w_ref[...], preferred_element_type=F32)
    if relu2:
        acc = jnp.square(jnp.maximum(acc, 0.0))
    o_ref[...] = acc.astype(o_ref.dtype)


def matmul(a, w, out_dtype, relu2=False, tm=1024, tn=512):
    m, k = a.shape
    n = w.shape[1]
    return pl.pallas_call(
        functools.partial(_mm_body, relu2=relu2),
        grid=(m // tm, n // tn),
        in_specs=[pl.BlockSpec((tm, k), lambda i, j: (i, 0)), pl.BlockSpec((k, tn), lambda i, j: (0, j))],
        out_specs=pl.BlockSpec((tm, tn), lambda i, j: (i, j)),
        out_shape=jax.ShapeDtypeStruct((m, n), out_dtype),
        compiler_params=_params("parallel", "parallel"),
        name="matmul_relu2" if relu2 else "matmul",
    )(a, w)


def _mm2_res_body(a1_ref, a2_ref, w1_ref, w2_ref, r_ref, o_ref):
    acc = jnp.dot(a1_ref[...], w1_ref[...], preferred_element_type=F32)
    acc = acc + jnp.dot(a2_ref[...], w2_ref[...], preferred_element_type=F32)
    o_ref[...] = r_ref[...] + acc


def matmul2_residual(a1, a2, w1, w2, res, tm=1024, tn=512):
    m, k1 = a1.shape
    k2 = a2.shape[1]
    n = w1.shape[1]
    return pl.pallas_call(
        _mm2_res_body,
        grid=(m // tm, n // tn),
        in_specs=[pl.BlockSpec((tm, k1), lambda i, j: (i, 0)), pl.BlockSpec((tm, k2), lambda i, j: (i, 0)),
                  pl.BlockSpec((k1, tn), lambda i, j: (0, j)), pl.BlockSpec((k2, tn), lambda i, j: (0, j)),
                  pl.BlockSpec((tm, tn), lambda i, j: (i, j))],
        out_specs=pl.BlockSpec((tm, tn), lambda i, j: (i, j)),
        out_shape=jax.ShapeDtypeStruct((m, n), F32),
        compiler_params=_params("parallel", "parallel"),
        name="out_proj",
    )(a1, a2, w1, w2, res)


def _mmk_res_body(a_ref, w_ref, r_ref, o_ref, acc_ref):
    k = pl.program_id(2)

    @pl.when(k == 0)
    def _():
        acc_ref[...] = jnp.zeros_like(acc_ref)

    acc_ref[...] += jnp.dot(a_ref[...], w_ref[...], preferred_element_type=F32)

    @pl.when(k == pl.num_programs(2) - 1)
    def _():
        o_ref[...] = r_ref[...] + acc_ref[...]


def matmul_ktiled_residual(a, w, res, tm=1024, tn=1024, tk=2048):
    m, k = a.shape
    n = w.shape[1]
    return pl.pallas_call(
        _mmk_res_body,
        grid=(m // tm, n // tn, k // tk),
        in_specs=[pl.BlockSpec((tm, tk), lambda i, j, kk: (i, kk)), pl.BlockSpec((tk, tn), lambda i, j, kk: (kk, j)),
                  pl.BlockSpec((tm, tn), lambda i, j, kk: (i, j))],
        out_specs=pl.BlockSpec((tm, tn), lambda i, j, kk: (i, j)),
        out_shape=jax.ShapeDtypeStruct((m, n), F32),
        scratch_shapes=[pltpu.VMEM((tm, tn), F32)],
        compiler_params=_params("parallel", "parallel", "arbitrary"),
        name="mlp_down",
    )(a, w, res)


def _gate_body(a_ref, w_ref, h_ref, e_ref, o_ref):
    acc = jnp.dot(a_ref[...], w_ref[...], preferred_element_type=F32)
    o_ref[...] = h_ref[...] + jax.nn.sigmoid(acc) * e_ref[...]


def gate_residual(a, w, h, e, tm=1024, tn=512):
    m, k = a.shape
    n = w.shape[1]
    return pl.pallas_call(
        _gate_body,
        grid=(m // tm, n // tn),
        in_specs=[pl.BlockSpec((tm, k), lambda i, j: (i, 0)), pl.BlockSpec((k, tn), lambda i, j: (0, j)),
                  pl.BlockSpec((tm, tn), lambda i, j: (i, j)), pl.BlockSpec((tm, tn), lambda i, j: (i, j))],
        out_specs=pl.BlockSpec((tm, tn), lambda i, j: (i, j)),
        out_shape=jax.ShapeDtypeStruct((m, n), F32),
        compiler_params=_params("parallel", "parallel"),
        name="ple_gate",
    )(a, w, h, e)


def _embed_body(p_ref, w_ref, g_ref, o_ref):
    y = jnp.dot(p_ref[...].astype(BF16), w_ref[...], preferred_element_type=F32)
    o_ref[...] = _rms(y, g_ref[...])


def embed_norm(p, w, g, tm=256):
    m, k = p.shape
    n = w.shape[1]
    return pl.pallas_call(
        _embed_body,
        grid=(m // tm,),
        in_specs=[pl.BlockSpec((tm, k), lambda i: (i, 0)), pl.BlockSpec((k, n), lambda i: (0, 0)),
                  pl.BlockSpec((1, n), lambda i: (0, 0))],
        out_specs=pl.BlockSpec((tm, n), lambda i: (i, 0)),
        out_shape=jax.ShapeDtypeStruct((m, n), F32),
        compiler_params=_params("parallel"),
        name="ple_embed",
    )(p, w, g.reshape(1, n))


def _mla_prep_body(pa_ref, pos_ref, gq_ref, gkv_ref, wq_ref, wk_ref, wvt_ref, freq_ref, sign_ref, q_ref, k_ref, vt_ref):
    pa = pa_ref[...]
    cq = _rms(pa[:, :Q_LORA], gq_ref[...]).astype(BF16)
    ckv = _rms(pa[:, Q_LORA:Q_LORA + KV_LORA], gkv_ref[...]).astype(BF16)
    kr = pa[:, Q_LORA + KV_LORA:Q_LORA + KV_LORA + LANES]
    krs = pa[:, Q_LORA + KV_LORA + LANES:]
    ang = pos_ref[...].astype(F32) * freq_ref[...]
    cos = jnp.cos(ang)
    sin = jnp.sin(ang) * sign_ref[...]
    q = jnp.dot(cq, wq_ref[...], preferred_element_type=F32)
    kn = jnp.dot(ckv, wk_ref[...], preferred_element_type=F32)
    vt_ref[...] = lax.dot_general(wvt_ref[...], ckv, (((1,), (1,)), ((), ())),
                                  preferred_element_type=F32).astype(BF16)
    krot = (kr * cos + krs * sin).astype(BF16)
    qscale = (QK_HEAD ** -0.5) * LOG2_E
    hn = MLA_HEADS * LANES
    for h in range(MLA_HEADS):
        lo, hi = h * LANES, (h + 1) * LANES
        qrot = q[:, hn + lo:hn + hi] * cos + q[:, 2 * hn + lo:2 * hn + hi] * sin
        q_ref[:, h * QK_PAD:h * QK_PAD + LANES] = (q[:, lo:hi] * qscale).astype(BF16)
        q_ref[:, h * QK_PAD + LANES:(h + 1) * QK_PAD] = (qrot * qscale).astype(BF16)
        k_ref[:, h * QK_PAD:h * QK_PAD + LANES] = kn[:, lo:hi].astype(BF16)
        k_ref[:, h * QK_PAD + LANES:(h + 1) * QK_PAD] = krot


def mla_prep(proj, pos, gq, gkv, wq, wk, wvt, freq, sign, tm=256):
    n = proj.shape[0]
    const = lambda i: (0, 0)
    return pl.pallas_call(
        _mla_prep_body,
        grid=(n // tm,),
        in_specs=[pl.BlockSpec((tm, A_WIDTH), lambda i: (i, 0)), pl.BlockSpec((tm, 1), lambda i: (i, 0)),
                  pl.BlockSpec((1, Q_LORA), const), pl.BlockSpec((1, KV_LORA), const),
                  pl.BlockSpec(wq.shape, const), pl.BlockSpec(wk.shape, const), pl.BlockSpec(wvt.shape, const),
                  pl.BlockSpec((1, LANES), const), pl.BlockSpec((1, LANES), const)],
        out_specs=[pl.BlockSpec((tm, MLA_HEADS * QK_PAD), lambda i: (i, 0)),
                   pl.BlockSpec((tm, MLA_HEADS * QK_PAD), lambda i: (i, 0)),
                   pl.BlockSpec((MLA_HEADS * V_HEAD, tm), lambda i: (0, i))],
        out_shape=[jax.ShapeDtypeStruct((n, MLA_HEADS * QK_PAD), BF16),
                   jax.ShapeDtypeStruct((n, MLA_HEADS * QK_PAD), BF16),
                   jax.ShapeDtypeStruct((MLA_HEADS * V_HEAD, n), BF16)],
        compiler_params=_params("parallel"),
        name="mla_prep",
    )(proj, pos, gq.reshape(1, -1), gkv.reshape(1, -1), wq, wk, wvt, freq, sign)


def _attn_body(q_ref, k_ref, vt_ref, o_ref, s_ref, mx_ref, m_ref, l_ref, acc_ref, *, tq, tk):
    qi = pl.program_id(2)
    q = q_ref[...]
    n_diag = tq // tk
    n_full = qi * n_diag

    def produce(j):
        k = k_ref[pl.ds(pl.multiple_of(j * tk, tk), tk), :]
        s = lax.dot_general(k, q, (((1,), (1,)), ((), ())), preferred_element_type=F32)
        return s, jnp.max(s, axis=0, keepdims=True)

    def consume(j, diag):
        s = s_ref[...]
        if diag is None:
            tile_max = mx_ref[...]
        else:
            key = diag * tk + lax.broadcasted_iota(jnp.int32, (tk, tq), 0)
            qry = lax.broadcasted_iota(jnp.int32, (tk, tq), 1)
            s = jnp.where(key <= qry, s, -jnp.inf)
            tile_max = jnp.max(s, axis=0, keepdims=True)
        m = m_ref[...]
        m_new = jnp.maximum(m, tile_max)
        alpha = jnp.exp2(m - m_new)
        p = jnp.exp2(s - m_new)
        m_ref[...] = m_new
        l_ref[...] = alpha * l_ref[...] + jnp.sum(p, axis=0, keepdims=True)
        vt = vt_ref[:, pl.ds(pl.multiple_of(j * tk, tk), tk)]
        acc_ref[...] = alpha * acc_ref[...] + jnp.dot(vt, p.astype(BF16), preferred_element_type=F32)

    m_ref[...] = jnp.full(m_ref.shape, -jnp.inf, F32)
    l_ref[...] = jnp.zeros(l_ref.shape, F32)
    acc_ref[...] = jnp.zeros(acc_ref.shape, F32)
    s_ref[...], mx_ref[...] = produce(0)

    def body(j, carry):
        s_next, mx_next = produce(j + 1)
        consume(j, None)
        s_ref[...] = s_next
        mx_ref[...] = mx_next
        return carry

    lax.fori_loop(0, n_full, body, 0)
    for d in range(n_diag):
        if d + 1 < n_diag:
            s_next, _ = produce(n_full + d + 1)
        consume(n_full + d, d)
        if d + 1 < n_diag:
            s_ref[...] = s_next
    o_ref[...] = (acc_ref[...] / l_ref[...]).T.astype(o_ref.dtype)


def causal_attention(q, k, vt, batch, seq, tq=1024, tk=512):
    nq = seq // tq
    return pl.pallas_call(
        functools.partial(_attn_body, tq=tq, tk=tk),
        grid=(batch, MLA_HEADS, nq),
        in_specs=[pl.BlockSpec((tq, QK_PAD), lambda b, h, i: (b * nq + i, h)),
                  pl.BlockSpec((seq, QK_PAD), lambda b, h, i: (b, h)),
                  pl.BlockSpec((V_HEAD, seq), lambda b, h, i: (h, b))],
        out_specs=pl.BlockSpec((tq, V_HEAD), lambda b, h, i: (b * nq + i, h)),
        out_shape=jax.ShapeDtypeStruct((batch * seq, MLA_HEADS * V_HEAD), BF16),
        scratch_shapes=[pltpu.VMEM((tk, tq), F32)] + [pltpu.VMEM((1, tq), F32)] * 3 + [pltpu.VMEM((V_HEAD, tq), F32)],
        compiler_params=_params("parallel", "parallel", "arbitrary"),
        name="mla_attention",
    )(q, k, vt)


def _hgrn_constants(c):
    t = np.arange(c)
    ltri = (t[None, :] <= t[:, None]).astype(np.float32)
    nlev = int(np.log2(c))
    x = t[:, None] ^ t[None, :]
    level = np.full((c, c), -1, np.int32)
    lower = t[:, None] > t[None, :]
    level[lower] = (nlev - 1) - np.floor(np.log2(x[lower])).astype(np.int32)
    level[t, t] = nlev
    return np.concatenate([ltri, ltri, ltri], axis=1), level, nlev


def _block_ref(b, half):
    c, dk = b.shape
    if half >= 4:
        x = b.reshape(c // (2 * half), 2 * half, dk)
        return jnp.broadcast_to(x[:, half - 1:half, :], x.shape).reshape(c, dk)
    assert half == 2
    x = b.reshape(c // 8, 8, dk)
    sub = lax.broadcasted_iota(jnp.int32, x.shape, 1)
    return jnp.where(sub < 4, x[:, 1:2, :], x[:, 5:6, :]).reshape(c, dk)


def _hgrn_body(hq_ref, hf_ref, hi_ref, hg_ref, lbp_ref, gn_ref, w3_ref, lev_ref, o_ref, st_ref, *, layer, tile, chunk, nlev):
    @pl.when(pl.program_id(2) == 0)
    def _():
        st_ref[...] = jnp.zeros_like(st_ref)

    hb = lbp_ref[...]
    ex = jnp.exp(hb - jnp.max(hb, axis=0, keepdims=True))
    sm = ex / jnp.sum(ex, axis=0, keepdims=True)
    lb = jnp.sum(sm[:layer + 1], axis=0, keepdims=True)
    oml = 1.0 - lb
    lev = lev_ref[...]
    gn = gn_ref[...]
    trans_b = (((1,), (1,)), ((), ()))
    chunks = [slice(c * chunk, (c + 1) * chunk) for c in range(tile // chunk)]
    fs, qs, kks, vs, bs = [], [], [], [], []
    for rows in chunks:
        z = hf_ref[rows, :]
        f = lb + oml * jax.nn.sigmoid(z)
        g = jnp.log(f)
        hq = hq_ref[rows, :]
        fs.append(f)
        kks.append(oml * jax.nn.sigmoid(-z))
        qs.append(hq * jax.nn.sigmoid(hq))
        vs.append(hi_ref[rows, :].astype(BF16))
        g_hi = g.astype(BF16)
        r1 = g - g_hi.astype(F32)
        g_mid = r1.astype(BF16)
        g_lo = (r1 - g_mid.astype(F32)).astype(BF16)
        g3 = jnp.concatenate([g_hi, g_mid, g_lo], axis=0)
        bs.append(jnp.dot(w3_ref[...], g3, preferred_element_type=F32))
    scores = [jnp.zeros((chunk, chunk), F32) for _ in chunks]
    q16 = [q.astype(BF16) for q in qs]
    k16 = [kk.astype(BF16) for kk in kks]
    for lv in range(nlev + 1):
        half = chunk >> (lv + 1)
        for c, (f, q, kk, b) in enumerate(zip(fs, q16, k16, bs)):
            if half >= 2:
                e = jnp.exp(-jnp.abs(b - _block_ref(b, half))).astype(BF16)
                ql, kl = q * e, kk * e
            elif half == 1:
                ql, kl = q * f.astype(BF16), kk
            else:
                ql, kl = q, kk
            p = lax.dot_general(ql, kl, trans_b, preferred_element_type=F32)
            scores[c] = jnp.where(lev == lv, p, scores[c])
    intra, updates, decays, qbs = [], [], [], []
    for a, q, kk, v, b in zip(scores, qs, kks, vs, bs):
        intra.append(jnp.dot(a.astype(BF16), v, preferred_element_type=F32))
        b_last = b[chunk - 1:chunk, :]
        kh = (kk * jnp.exp(b_last - b)).astype(BF16)
        updates.append(lax.dot_general(v, kh, (((0,), (0,)), ((), ())), preferred_element_type=F32))
        decays.append(jnp.exp(b_last))
        qbs.append((q * jnp.exp(b)).astype(BF16))
    state = st_ref[...]
    for rows, qb, o_intra, upd, dec in zip(chunks, qbs, intra, updates, decays):
        o = lax.dot_general(qb, state.astype(BF16), trans_b, preferred_element_type=F32) + o_intra
        state = dec * state + upd
        hg = hg_ref[rows, :]
        y = _rms(o, gn) * (hg * jax.nn.sigmoid(hg))
        o_ref[rows, :] = y.astype(o_ref.dtype)
    st_ref[...] = state


def hgrn2(proj, col0, lbp, gn, batch, seq, layer, tile=512):
    w3, level, nlev = _hgrn_constants(HG_CHUNK)
    w3 = jnp.asarray(w3, BF16)
    level = jnp.asarray(level)
    nt = seq // tile
    cb = col0 // LANES

    def col(group):
        return lambda b, h, i: (b * nt + i, cb + group * HG_HEADS + h)

    return pl.pallas_call(
        functools.partial(_hgrn_body, layer=layer, tile=tile, chunk=HG_CHUNK, nlev=nlev),
        grid=(batch, HG_HEADS, nt),
        in_specs=[pl.BlockSpec((tile, LANES), col(0)), pl.BlockSpec((tile, LANES), col(1)),
                  pl.BlockSpec((tile, LANES), col(2)), pl.BlockSpec((tile, LANES), col(3)),
                  pl.BlockSpec((lbp.shape[0], LANES), lambda b, h, i: (0, h)),
                  pl.BlockSpec((1, LANES), lambda b, h, i: (0, h)),
                  pl.BlockSpec(w3.shape, lambda b, h, i: (0, 0)),
                  pl.BlockSpec(level.shape, lambda b, h, i: (0, 0))],
        out_specs=pl.BlockSpec((tile, LANES), lambda b, h, i: (b * nt + i, h)),
        out_shape=jax.ShapeDtypeStruct((batch * seq, HG_VDIM), BF16),
        scratch_shapes=[pltpu.VMEM((HG_DV, HG_DK), F32)],
        compiler_params=_params("parallel", "parallel", "arbitrary"),
        name="hgrn2",
    )(proj, proj, proj, proj, lbp, gn.reshape(1, -1), w3, level)


def _swap_halves(w):
    half = w.shape[-1] // 2
    return jnp.concatenate([w[..., half:], w[..., :half]], axis=-1)


def _layer_weights(w_in, w_uq, w_ukv):
    d = w_in.shape[0]
    n_a = Q_LORA + KV_LORA
    kr = w_in[:, n_a:n_a + QK_ROPE]
    zero = jnp.zeros((d, LANES - QK_ROPE), w_in.dtype)
    w_cat = jnp.concatenate([w_in[:, :n_a + QK_ROPE], zero, _swap_halves(kr), zero, w_in[:, n_a + QK_ROPE:]], axis=1)
    wq = w_uq.reshape(Q_LORA, MLA_HEADS, QK_HEAD)
    rope = wq[:, :, QK_NOPE:]
    pad = jnp.zeros((Q_LORA, MLA_HEADS, LANES - QK_ROPE), w_uq.dtype)
    wq_all = jnp.concatenate([
        wq[:, :, :QK_NOPE].reshape(Q_LORA, -1),
        jnp.concatenate([rope, pad], axis=-1).reshape(Q_LORA, -1),
        jnp.concatenate([_swap_halves(rope), pad], axis=-1).reshape(Q_LORA, -1)], axis=1)
    wkv = w_ukv.reshape(KV_LORA, MLA_HEADS, QK_NOPE + V_HEAD)
    wk = wkv[:, :, :QK_NOPE].reshape(KV_LORA, -1)
    wvt = wkv[:, :, QK_NOPE:].reshape(KV_LORA, -1).T
    return w_cat.astype(BF16), wq_all.astype(BF16), wk.astype(BF16), wvt.astype(BF16)


def _rope_rows():
    inv_freq = ROPE_THETA ** (-jnp.arange(0, QK_ROPE, 2, dtype=F32) / QK_ROPE)
    zero = jnp.zeros((LANES - QK_ROPE,), F32)
    freq = jnp.concatenate([inv_freq, inv_freq, zero]).reshape(1, LANES)
    half = jnp.ones((QK_ROPE // 2,), F32)
    sign = jnp.concatenate([-half, half, zero]).reshape(1, LANES)
    return freq, sign


def kernel(x, p, positions, norm_mix, w_in, q_a_norm, kv_a_norm, w_uq, w_ukv, hg_lower_bound, hg_out_norm, w_o,
           norm_mlp, w_up, w_down, norm_ple, w_ple_gate, w_ple, ple_post_norm, final_norm):
    batch, seq, d_model = x.shape
    n = batch * seq
    depth = w_in.shape[0]
    h = x.reshape(n, d_model)
    pos = positions.reshape(n, 1)
    freq, sign = _rope_rows()
    mla_width = MLA_HEADS * V_HEAD
    for i in range(depth):
        w_cat, wq_all, wk, wvt = _layer_weights(w_in[i], w_uq[i], w_ukv[i])
        u = rmsnorm(h, norm_mix[i], BF16)
        proj = matmul(u, w_cat, F32)
        q, k, vt = mla_prep(proj, pos, q_a_norm[i], kv_a_norm[i], wq_all, wk, wvt, freq, sign)
        o_mla = causal_attention(q, k, vt, batch, seq)
        o_hg = hgrn2(proj, A_WIDTH, hg_lower_bound, hg_out_norm[i], batch, seq, i)
        wo = w_o[i].astype(BF16)
        h = matmul2_residual(o_mla, o_hg, wo[:mla_width], wo[mla_width:], h)
        u = rmsnorm(h, norm_mlp[i], BF16)
        hidden = matmul(u, w_up[i].astype(BF16), BF16, relu2=True)
        h = matmul_ktiled_residual(hidden, w_down[i].astype(BF16), h)
        u = rmsnorm(h, norm_ple[i], BF16)
        e = embed_norm(p[i].reshape(n, -1), w_ple[i].astype(BF16), ple_post_norm[i])
        h = gate_residual(u, w_ple_gate[i].astype(BF16), h, e)
    return rmsnorm(h, final_norm, x.dtype).reshape(batch, seq, d_model)
```

```python
import functools

import numpy as np
import jax
import jax.numpy as jnp
from jax import lax
from jax.experimental import pallas as pl
from jax.experimental.pallas import tpu as pltpu

EPS = 1e-6
MLA_HEADS = 16
QK_NOPE = 128
QK_ROPE = 64
QK_HEAD = QK_NOPE + QK_ROPE
V_HEAD = 128
Q_LORA = 768
KV_LORA = 512
ROPE_THETA = 10000.0
HG_HEADS = 16
HG_DK = 128
HG_DV = 128
HG_FDIM = HG_HEADS * HG_DK
HG_VDIM = HG_HEADS * HG_DV

LANES = 128
QK_PAD = 2 * LANES
A_WIDTH = Q_LORA + KV_LORA + 2 * LANES
HG_CHUNK = 128
VMEM_LIMIT_BYTES = 56 * 1024 * 1024
LOG2_E = 1.4426950408889634

F32 = jnp.float32
BF16 = jnp.bfloat16


def _params(*semantics):
    return pltpu.CompilerParams(dimension_semantics=semantics, vmem_limit_bytes=VMEM_LIMIT_BYTES)


def _rms(x, g):
    return x * lax.rsqrt(jnp.mean(x * x, axis=-1, keepdims=True) + EPS) * g


def _rmsnorm_body(x_ref, g_ref, o_ref):
    o_ref[...] = _rms(x_ref[...], g_ref[...]).astype(o_ref.dtype)


def rmsnorm(x, g, out_dtype, tm=256):
    n, d = x.shape
    return pl.pallas_call(
        _rmsnorm_body,
        grid=(n // tm,),
        in_specs=[pl.BlockSpec((tm, d), lambda i: (i, 0)), pl.BlockSpec((1, d), lambda i: (0, 0))],
        out_specs=pl.BlockSpec((tm, d), lambda i: (i, 0)),
        out_shape=jax.ShapeDtypeStruct((n, d), out_dtype),
        compiler_params=_params("parallel"),
        name="rmsnorm",
    )(x, g.reshape(1, d))


def _emit_norm_inputs(h, g_ref, hb_ref, ss_ref, first):
    hb_ref[...] = (h * g_ref[...]).astype(BF16)

    @pl.when(first)
    def _():
        ss_ref[...] = jnp.zeros_like(ss_ref)

    ss_ref[...] += jnp.sum(h * h, axis=-1, keepdims=True)


def _row_scale(ss_ref, d):
    return lax.rsqrt(ss_ref[:, :1] / d + EPS)


def _mm_body(a_ref, w_ref, o_ref):
    o_ref[...] = jnp.dot(a_ref[...], w_ref[...], preferred_element_type=F32).astype(o_ref.dtype)


def matmul(a, w, out_dtype, tm=1024, tn=512):
    m, k = a.shape
    n = w.shape[1]
    return pl.pallas_call(
        _mm_body,
        grid=(m // tm, n // tn),
        in_specs=[pl.BlockSpec((tm, k), lambda i, j: (i, 0)), pl.BlockSpec((k, tn), lambda i, j: (0, j))],
        out_specs=pl.BlockSpec((tm, tn), lambda i, j: (i, j)),
        out_shape=jax.ShapeDtypeStruct((m, n), out_dtype),
        compiler_params=_params("parallel", "parallel"),
        name="in_proj",
    )(a, w)


def _mlp_up_body(a_ref, w_ref, ss_ref, o_ref):
    acc = jnp.dot(a_ref[...], w_ref[...], preferred_element_type=F32) * _row_scale(ss_ref, a_ref.shape[1])
    o_ref[...] = jnp.square(jnp.maximum(acc, 0.0)).astype(o_ref.dtype)


def mlp_up(hb, ss, w, tm=1024, tn=512):
    m, k = hb.shape
    n = w.shape[1]
    return pl.pallas_call(
        _mlp_up_body,
        grid=(m // tm, n // tn),
        in_specs=[pl.BlockSpec((tm, k), lambda i, j: (i, 0)), pl.BlockSpec((k, tn), lambda i, j: (0, j)),
                  pl.BlockSpec((tm, LANES), lambda i, j: (i, 0))],
        out_specs=pl.BlockSpec((tm, tn), lambda i, j: (i, j)),
        out_shape=jax.ShapeDtypeStruct((m, n), BF16),
        compiler_params=_params("parallel", "parallel"),
        name="mlp_up",
    )(hb, w, ss)


def _norm_outputs(m, n, tm, tn, index):
    specs = [pl.BlockSpec((tm, tn), index), pl.BlockSpec((tm, tn), index),
             pl.BlockSpec((tm, LANES), lambda i, *_: (i, 0))]
    shapes = [jax.ShapeDtypeStruct((m, n), F32), jax.ShapeDtypeStruct((m, n), BF16),
              jax.ShapeDtypeStruct((m, LANES), F32)]
    return specs, shapes


def _out_proj_body(a1_ref, a2_ref, w1_ref, w2_ref, r_ref, g_ref, o_ref, hb_ref, ss_ref):
    acc = jnp.dot(a1_ref[...], w1_ref[...], preferred_element_type=F32)
    acc = acc + jnp.dot(a2_ref[...], w2_ref[...], preferred_element_type=F32)
    h = r_ref[...] + acc
    o_ref[...] = h
    _emit_norm_inputs(h, g_ref, hb_ref, ss_ref, pl.program_id(1) == 0)


def out_proj_residual(a1, a2, w, res, gain, tm=1024, tn=512):
    m, k1 = a1.shape
    k2 = a2.shape[1]
    assert k1 == k2 and w.shape[0] == k1 + k2
    n = w.shape[1]
    tile = lambda i, j: (i, j)
    out_specs, out_shape = _norm_outputs(m, n, tm, tn, tile)
    return pl.pallas_call(
        _out_proj_body,
        grid=(m // tm, n // tn),
        in_specs=[pl.BlockSpec((tm, k1), lambda i, j: (i, 0)), pl.BlockSpec((tm, k2), lambda i, j: (i, 0)),
                  pl.BlockSpec((k1, tn), lambda i, j: (0, j)), pl.BlockSpec((k2, tn), lambda i, j: (1, j)),
                  pl.BlockSpec((tm, tn), tile), pl.BlockSpec((1, tn), lambda i, j: (0, j))],
        out_specs=out_specs,
        out_shape=out_shape,
        compiler_params=_params("parallel", "arbitrary"),
        name="out_proj",
    )(a1, a2, w, w, res, gain.reshape(1, n))


def _mlp_down_body(a_ref, w_ref, r_ref, g_ref, o_ref, hb_ref, ss_ref, acc_ref):
    k = pl.program_id(2)

    @pl.when(k == 0)
    def _():
        acc_ref[...] = jnp.zeros_like(acc_ref)

    acc_ref[...] += jnp.dot(a_ref[...], w_ref[...], preferred_element_type=F32)

    @pl.when(k == pl.num_programs(2) - 1)
    def _():
        h = r_ref[...] + acc_ref[...]
        o_ref[...] = h
        _emit_norm_inputs(h, g_ref, hb_ref, ss_ref, pl.program_id(1) == 0)


def mlp_down_residual(a, w, res, gain, tm=1024, tn=1024, tk=2048):
    m, k = a.shape
    n = w.shape[1]
    tile = lambda i, j, kk: (i, j)
    out_specs, out_shape = _norm_outputs(m, n, tm, tn, tile)
    return pl.pallas_call(
        _mlp_down_body,
        grid=(m // tm, n // tn, k // tk),
        in_specs=[pl.BlockSpec((tm, tk), lambda i, j, kk: (i, kk)), pl.BlockSpec((tk, tn), lambda i, j, kk: (kk, j)),
                  pl.BlockSpec((tm, tn), tile), pl.BlockSpec((1, tn), lambda i, j, kk: (0, j))],
        out_specs=out_specs,
        out_shape=out_shape,
        scratch_shapes=[pltpu.VMEM((tm, tn), F32)],
        compiler_params=_params("parallel", "arbitrary", "arbitrary"),
        name="mlp_down",
    )(a, w, res, gain.reshape(1, n))


def _gate_body(a_ref, w_ref, ss_ref, h_ref, e_ref, o_ref):
    acc = jnp.dot(a_ref[...], w_ref[...], preferred_element_type=F32) * _row_scale(ss_ref, a_ref.shape[1])
    o_ref[...] = h_ref[...] + jax.nn.sigmoid(acc) * e_ref[...]


def gate_residual(hb, ss, w, h, e, tm=1024, tn=512):
    m, k = hb.shape
    n = w.shape[1]
    return pl.pallas_call(
        _gate_body,
        grid=(m // tm, n // tn),
        in_specs=[pl.BlockSpec((tm, k), lambda i, j: (i, 0)), pl.BlockSpec((k, tn), lambda i, j: (0, j)),
                  pl.BlockSpec((tm, LANES), lambda i, j: (i, 0)),
                  pl.BlockSpec((tm, tn), lambda i, j: (i, j)), pl.BlockSpec((tm, tn), lambda i, j: (i, j))],
        out_specs=pl.BlockSpec((tm, tn), lambda i, j: (i, j)),
        out_shape=jax.ShapeDtypeStruct((m, n), F32),
        compiler_params=_params("parallel", "parallel"),
        name="ple_gate",
    )(hb, w, ss, h, e)


def _embed_body(p_ref, w_ref, g_ref, o_ref):
    y = jnp.dot(p_ref[...].astype(BF16), w_ref[...], preferred_element_type=F32)
    o_ref[...] = _rms(y, g_ref[...])


def embed_norm(p, w, g, tm=256):
    m, k = p.shape
    n = w.shape[1]
    return pl.pallas_call(
        _embed_body,
        grid=(m // tm,),
        in_specs=[pl.BlockSpec((tm, k), lambda i: (i, 0)), pl.BlockSpec((k, n), lambda i: (0, 0)),
                  pl.BlockSpec((1, n), lambda i: (0, 0))],
        out_specs=pl.BlockSpec((tm, n), lambda i: (i, 0)),
        out_shape=jax.ShapeDtypeStruct((m, n), F32),
        compiler_params=_params("parallel"),
        name="ple_embed",
    )(p, w, g.reshape(1, n))


def _mla_prep_body(pa_ref, pos_ref, gq_ref, gkv_ref, wq_ref, wk_ref, wvt_ref, freq_ref, sign_ref, q_ref, k_ref, vt_ref):
    pa = pa_ref[...]
    cq = _rms(pa[:, :Q_LORA], gq_ref[...]).astype(BF16)
    ckv = _rms(pa[:, Q_LORA:Q_LORA + KV_LORA], gkv_ref[...]).astype(BF16)
    kr = pa[:, Q_LORA + KV_LORA:Q_LORA + KV_LORA + LANES]
    krs = pa[:, Q_LORA + KV_LORA + LANES:]
    ang = pos_ref[...].astype(F32) * freq_ref[...]
    cos = jnp.cos(ang)
    sin = jnp.sin(ang) * sign_ref[...]
    q = jnp.dot(cq, wq_ref[...], preferred_element_type=F32)
    kn = jnp.dot(ckv, wk_ref[...], preferred_element_type=F32)
    vt_ref[...] = lax.dot_general(wvt_ref[...], ckv, (((1,), (1,)), ((), ())),
                                  preferred_element_type=F32).astype(BF16)
    krot = (kr * cos + krs * sin).astype(BF16)
    qscale = (QK_HEAD ** -0.5) * LOG2_E
    hn = MLA_HEADS * LANES
    for h in range(MLA_HEADS):
        lo, hi = h * LANES, (h + 1) * LANES
        qrot = q[:, hn + lo:hn + hi] * cos + q[:, 2 * hn + lo:2 * hn + hi] * sin
        q_ref[:, h * QK_PAD:h * QK_PAD + LANES] = (q[:, lo:hi] * qscale).astype(BF16)
        q_ref[:, h * QK_PAD + LANES:(h + 1) * QK_PAD] = (qrot * qscale).astype(BF16)
        k_ref[:, h * QK_PAD:h * QK_PAD + LANES] = kn[:, lo:hi].astype(BF16)
        k_ref[:, h * QK_PAD + LANES:(h + 1) * QK_PAD] = krot


def mla_prep(proj, pos, gq, gkv, wq, wk, wvt, freq, sign, tm=256):
    n = proj.shape[0]
    const = lambda i: (0, 0)
    return pl.pallas_call(
        _mla_prep_body,
        grid=(n // tm,),
        in_specs=[pl.BlockSpec((tm, A_WIDTH), lambda i: (i, 0)), pl.BlockSpec((tm, 1), lambda i: (i, 0)),
                  pl.BlockSpec((1, Q_LORA), const), pl.BlockSpec((1, KV_LORA), const),
                  pl.BlockSpec(wq.shape, const), pl.BlockSpec(wk.shape, const), pl.BlockSpec(wvt.shape, const),
                  pl.BlockSpec((1, LANES), const), pl.BlockSpec((1, LANES), const)],
        out_specs=[pl.BlockSpec((tm, MLA_HEADS * QK_PAD), lambda i: (i, 0)),
                   pl.BlockSpec((tm, MLA_HEADS * QK_PAD), lambda i: (i, 0)),
                   pl.BlockSpec((MLA_HEADS * V_HEAD, tm), lambda i: (0, i))],
        out_shape=[jax.ShapeDtypeStruct((n, MLA_HEADS * QK_PAD), BF16),
                   jax.ShapeDtypeStruct((n, MLA_HEADS * QK_PAD), BF16),
                   jax.ShapeDtypeStruct((MLA_HEADS * V_HEAD, n), BF16)],
        compiler_params=_params("parallel"),
        name="mla_prep",
    )(proj, pos, gq.reshape(1, -1), gkv.reshape(1, -1), wq, wk, wvt, freq, sign)


def _attn_body(q_ref, k_ref, vt_ref, o_ref, s_ref, mx_ref, m_ref, l_ref, acc_ref, *, tq, tk):
    qi = pl.program_id(2)
    q = q_ref[...]
    n_diag = tq // tk
    n_full = qi * n_diag

    def produce(j):
        k = k_ref[pl.ds(pl.multiple_of(j * tk, tk), tk), :]
        s = lax.dot_general(k, q, (((1,), (1,)), ((), ())), preferred_element_type=F32)
        return s, jnp.max(s, axis=0, keepdims=True)

    def consume(j, diag):
        s = s_ref[...]
        if diag is None:
            tile_max = mx_ref[...]
        else:
            key = diag * tk + lax.broadcasted_iota(jnp.int32, (tk, tq), 0)
            qry = lax.broadcasted_iota(jnp.int32, (tk, tq), 1)
            s = jnp.where(key <= qry, s, -jnp.inf)
            tile_max = jnp.max(s, axis=0, keepdims=True)
        m = m_ref[...]
        m_new = jnp.maximum(m, tile_max)
        alpha = jnp.exp2(m - m_new)
        p = jnp.exp2(s - m_new)
        m_ref[...] = m_new
        l_ref[...] = alpha * l_ref[...] + jnp.sum(p, axis=0, keepdims=True)
        vt = vt_ref[:, pl.ds(pl.multiple_of(j * tk, tk), tk)]
        acc_ref[...] = alpha * acc_ref[...] + jnp.dot(vt, p.astype(BF16), preferred_element_type=F32)

    m_ref[...] = jnp.full(m_ref.shape, -jnp.inf, F32)
    l_ref[...] = jnp.zeros(l_ref.shape, F32)
    acc_ref[...] = jnp.zeros(acc_ref.shape, F32)
    s_ref[...], mx_ref[...] = produce(0)

    def body(j, carry):
        s_next, mx_next = produce(j + 1)
        consume(j, None)
        s_ref[...] = s_next
        mx_ref[...] = mx_next
        return carry

    lax.fori_loop(0, n_full, body, 0)
    for d in range(n_diag):
        if d + 1 < n_diag:
            s_next, _ = produce(n_full + d + 1)
        consume(n_full + d, d)
        if d + 1 < n_diag:
            s_ref[...] = s_next
    o_ref[...] = (acc_ref[...] / l_ref[...]).T.astype(o_ref.dtype)


def causal_attention(q, k, vt, batch, seq, tq=1024, tk=512):
    nq = seq // tq
    return pl.pallas_call(
        functools.partial(_attn_body, tq=tq, tk=tk),
        grid=(batch, MLA_HEADS, nq),
        in_specs=[pl.BlockSpec((tq, QK_PAD), lambda b, h, i: (b * nq + i, h)),
                  pl.BlockSpec((seq, QK_PAD), lambda b, h, i: (b, h)),
                  pl.BlockSpec((V_HEAD, seq), lambda b, h, i: (h, b))],
        out_specs=pl.BlockSpec((tq, V_HEAD), lambda b, h, i: (b * nq + i, h)),
        out_shape=jax.ShapeDtypeStruct((batch * seq, MLA_HEADS * V_HEAD), BF16),
        scratch_shapes=[pltpu.VMEM((tk, tq), F32)] + [pltpu.VMEM((1, tq), F32)] * 3 + [pltpu.VMEM((V_HEAD, tq), F32)],
        compiler_params=_params("parallel", "parallel", "arbitrary"),
        name="mla_attention",
    )(q, k, vt)


def _hgrn_constants(c):
    t = np.arange(c)
    ltri = (t[None, :] <= t[:, None]).astype(np.float32)
    nlev = int(np.log2(c))
    x = t[:, None] ^ t[None, :]
    level = np.full((c, c), -1, np.int32)
    lower = t[:, None] > t[None, :]
    level[lower] = (nlev - 1) - np.floor(np.log2(x[lower])).astype(np.int32)
    level[t, t] = nlev
    return np.concatenate([ltri, ltri, ltri], axis=1), level, nlev


def _block_ref(b, half):
    c, dk = b.shape
    if half >= 4:
        x = b.reshape(c // (2 * half), 2 * half, dk)
        return jnp.broadcast_to(x[:, half - 1:half, :], x.shape).reshape(c, dk)
    assert half == 2
    x = b.reshape(c // 8, 8, dk)
    sub = lax.broadcasted_iota(jnp.int32, x.shape, 1)
    return jnp.where(sub < 4, x[:, 1:2, :], x[:, 5:6, :]).reshape(c, dk)


def _hgrn_body(hq_ref, hf_ref, hi_ref, hg_ref, lbp_ref, gn_ref, w3_ref, lev_ref, o_ref, st_ref, *, layer, tile, chunk, nlev):
    @pl.when(pl.program_id(2) == 0)
    def _():
        st_ref[...] = jnp.zeros_like(st_ref)

    hb = lbp_ref[...]
    ex = jnp.exp(hb - jnp.max(hb, axis=0, keepdims=True))
    sm = ex / jnp.sum(ex, axis=0, keepdims=True)
    lb = jnp.sum(sm[:layer + 1], axis=0, keepdims=True)
    oml = 1.0 - lb
    lev = lev_ref[...]
    gn = gn_ref[...]
    trans_b = (((1,), (1,)), ((), ()))
    chunks = [slice(c * chunk, (c + 1) * chunk) for c in range(tile // chunk)]
    fs, qs, kks, vs, bs = [], [], [], [], []
    for rows in chunks:
        z = hf_ref[rows, :]
        f = lb + oml * jax.nn.sigmoid(z)
        g = jnp.log(f)
        hq = hq_ref[rows, :]
        fs.append(f)
        kks.append(oml * jax.nn.sigmoid(-z))
        qs.append(hq * jax.nn.sigmoid(hq))
        vs.append(hi_ref[rows, :].astype(BF16))
        g_hi = g.astype(BF16)
        r1 = g - g_hi.astype(F32)
        g_mid = r1.astype(BF16)
        g_lo = (r1 - g_mid.astype(F32)).astype(BF16)
        g3 = jnp.concatenate([g_hi, g_mid, g_lo], axis=0)
        bs.append(jnp.dot(w3_ref[...], g3, preferred_element_type=F32))
    scores = [jnp.zeros((chunk, chunk), F32) for _ in chunks]
    q16 = [q.astype(BF16) for q in qs]
    k16 = [kk.astype(BF16) for kk in kks]
    for lv in range(nlev + 1):
        half = chunk >> (lv + 1)
        for c, (f, q, kk, b) in enumerate(zip(fs, q16, k16, bs)):
            if half >= 2:
                e = jnp.exp(-jnp.abs(b - _block_ref(b, half))).astype(BF16)
                ql, kl = q * e, kk * e
            elif half == 1:
                ql, kl = q * f.astype(BF16), kk
            else:
                ql, kl = q, kk
            p = lax.dot_general(ql, kl, trans_b, preferred_element_type=F32)
            scores[c] = jnp.where(lev == lv, p, scores[c])
    intra, updates, decays, qbs = [], [], [], []
    for a, q, kk, v, b in zip(scores, qs, kks, vs, bs):
        intra.append(jnp.dot(a.astype(BF16), v, preferred_element_type=F32))
        b_last = b[chunk - 1:chunk, :]
        kh = (kk * jnp.exp(b_last - b)).astype(BF16)
        updates.append(lax.dot_general(v, kh, (((0,), (0,)), ((), ())), preferred_element_type=F32))
        decays.append(jnp.exp(b_last))
        qbs.append((q * jnp.exp(b)).astype(BF16))
    state = st_ref[...]
    for rows, qb, o_intra, upd, dec in zip(chunks, qbs, intra, updates, decays):
        o = lax.dot_general(qb, state.astype(BF16), trans_b, preferred_element_type=F32) + o_intra
        state = dec * state + upd
        hg = hg_ref[rows, :]
        y = _rms(o, gn) * (hg * jax.nn.sigmoid(hg))
        o_ref[rows, :] = y.astype(o_ref.dtype)
    st_ref[...] = state


def hgrn2(proj, col0, lbp, gn, batch, seq, layer, tile=512):
    w3, level, nlev = _hgrn_constants(HG_CHUNK)
    w3 = jnp.asarray(w3, BF16)
    level = jnp.asarray(level)
    nt = seq // tile
    cb = col0 // LANES

    def col(group):
        return lambda b, h, i: (b * nt + i, cb + group * HG_HEADS + h)

    return pl.pallas_call(
        functools.partial(_hgrn_body, layer=layer, tile=tile, chunk=HG_CHUNK, nlev=nlev),
        grid=(batch, HG_HEADS, nt),
        in_specs=[pl.BlockSpec((tile, LANES), col(0)), pl.BlockSpec((tile, LANES), col(1)),
                  pl.BlockSpec((tile, LANES), col(2)), pl.BlockSpec((tile, LANES), col(3)),
                  pl.BlockSpec((lbp.shape[0], LANES), lambda b, h, i: (0, h)),
                  pl.BlockSpec((1, LANES), lambda b, h, i: (0, h)),
                  pl.BlockSpec(w3.shape, lambda b, h, i: (0, 0)),
                  pl.BlockSpec(level.shape, lambda b, h, i: (0, 0))],
        out_specs=pl.BlockSpec((tile, LANES), lambda b, h, i: (b * nt + i, h)),
        out_shape=jax.ShapeDtypeStruct((batch * seq, HG_VDIM), BF16),
        scratch_shapes=[pltpu.VMEM((HG_DV, HG_DK), F32)],
        compiler_params=_params("parallel", "parallel", "arbitrary"),
        name="hgrn2",
    )(proj, proj, proj, proj, lbp, gn.reshape(1, -1), w3, level)


def _swap_halves(w):
    half = w.shape[-1] // 2
    return jnp.concatenate([w[..., half:], w[..., :half]], axis=-1)


def _layer_weights(w_in, w_uq, w_ukv):
    d = w_in.shape[0]
    n_a = Q_LORA + KV_LORA
    kr = w_in[:, n_a:n_a + QK_ROPE]
    zero = jnp.zeros((d, LANES - QK_ROPE), w_in.dtype)
    w_a = jnp.concatenate([w_in[:, :n_a + QK_ROPE], zero, _swap_halves(kr), zero], axis=1)
    w_h = w_in[:, n_a + QK_ROPE:]
    wq = w_uq.reshape(Q_LORA, MLA_HEADS, QK_HEAD)
    rope = wq[:, :, QK_NOPE:]
    pad = jnp.zeros((Q_LORA, MLA_HEADS, LANES - QK_ROPE), w_uq.dtype)
    wq_all = jnp.concatenate([
        wq[:, :, :QK_NOPE].reshape(Q_LORA, -1),
        jnp.concatenate([rope, pad], axis=-1).reshape(Q_LORA, -1),
        jnp.concatenate([_swap_halves(rope), pad], axis=-1).reshape(Q_LORA, -1)], axis=1)
    wkv = w_ukv.reshape(KV_LORA, MLA_HEADS, QK_NOPE + V_HEAD)
    wk = wkv[:, :, :QK_NOPE].reshape(KV_LORA, -1)
    wvt = wkv[:, :, QK_NOPE:].reshape(KV_LORA, -1).T
    return w_a.astype(BF16), w_h.astype(BF16), wq_all.astype(BF16), wk.astype(BF16), wvt.astype(BF16)


def _rope_rows():
    inv_freq = ROPE_THETA ** (-jnp.arange(0, QK_ROPE, 2, dtype=F32) / QK_ROPE)
    zero = jnp.zeros((LANES - QK_ROPE,), F32)
    freq = jnp.concatenate([inv_freq, inv_freq, zero]).reshape(1, LANES)
    half = jnp.ones((QK_ROPE // 2,), F32)
    sign = jnp.concatenate([-half, half, zero]).reshape(1, LANES)
    return freq, sign


def kernel(x, p, positions, norm_mix, w_in, q_a_norm, kv_a_norm, w_uq, w_ukv, hg_lower_bound, hg_out_norm, w_o,
           norm_mlp, w_up, w_down, norm_ple, w_ple_gate, w_ple, ple_post_norm, final_norm):
    batch, seq, d_model = x.shape
    n = batch * seq
    depth = w_in.shape[0]
    h = x.reshape(n, d_model)
    pos = positions.reshape(n, 1)
    freq, sign = _rope_rows()
    for i in range(depth):
        w_a, w_h, wq_all, wk, wvt = _layer_weights(w_in[i], w_uq[i], w_ukv[i])
        u = rmsnorm(h, norm_mix[i], BF16)
        proj_a = matmul(u, w_a, F32)
        proj_h = matmul(u, w_h, F32, tn=1024)
        q, k, vt = mla_prep(proj_a, pos, q_a_norm[i], kv_a_norm[i], wq_all, wk, wvt, freq, sign)
        o_mla = causal_attention(q, k, vt, batch, seq)
        o_hg = hgrn2(proj_h, 0, hg_lower_bound, hg_out_norm[i], batch, seq, i)
        h, hb, ss = out_proj_residual(o_mla, o_hg, w_o[i].astype(BF16), h, norm_mlp[i])
        hidden = mlp_up(hb, ss, w_up[i].astype(BF16))
        h, hb, ss = mlp_down_residual(hidden, w_down[i].astype(BF16), h, norm_ple[i])
        e = embed_norm(p[i].reshape(n, -1), w_ple[i].astype(BF16), ple_post_norm[i])
        h = gate_residual(hb, ss, w_ple_gate[i].astype(BF16), h, e)
    return rmsnorm(h, final_norm, x.dtype).reshape(batch, seq, d_model)
```

```python
import functools

import numpy as np
import jax
import jax.numpy as jnp
from jax import lax
from jax.experimental import pallas as pl
from jax.experimental.pallas import tpu as pltpu

EPS = 1e-6
MLA_HEADS = 16
QK_NOPE = 128
QK_ROPE = 64
QK_HEAD = QK_NOPE + QK_ROPE
V_HEAD = 128
Q_LORA = 768
KV_LORA = 512
ROPE_THETA = 10000.0
HG_HEADS = 16
HG_DK = 128
HG_DV = 128
HG_FDIM = HG_HEADS * HG_DK
HG_VDIM = HG_HEADS * HG_DV

LANES = 128
QK_PAD = 2 * LANES
A_WIDTH = Q_LORA + KV_LORA + 2 * LANES
HG_CHUNK = 128
VMEM_LIMIT_BYTES = 56 * 1024 * 1024
LOG2_E = 1.4426950408889634

F32 = jnp.float32
BF16 = jnp.bfloat16


def _params(*semantics):
    return pltpu.CompilerParams(dimension_semantics=semantics, vmem_limit_bytes=VMEM_LIMIT_BYTES)


def _rms(x, g):
    return x * lax.rsqrt(jnp.mean(x * x, axis=-1, keepdims=True) + EPS) * g


def _rmsnorm_body(x_ref, g_ref, o_ref):
    o_ref[...] = _rms(x_ref[...], g_ref[...]).astype(o_ref.dtype)


def rmsnorm(x, g, out_dtype, tm=256):
    n, d = x.shape
    return pl.pallas_call(
        _rmsnorm_body,
        grid=(n // tm,),
        in_specs=[pl.BlockSpec((tm, d), lambda i: (i, 0)), pl.BlockSpec((1, d), lambda i: (0, 0))],
        out_specs=pl.BlockSpec((tm, d), lambda i: (i, 0)),
        out_shape=jax.ShapeDtypeStruct((n, d), out_dtype),
        compiler_params=_params("parallel"),
        name="rmsnorm",
    )(x, g.reshape(1, d))


def _emit_norm_inputs(h, g_ref, hb_ref, ss_ref, first):
    hb_ref[...] = (h * g_ref[...]).astype(BF16)

    @pl.when(first)
    def _():
        ss_ref[...] = jnp.zeros_like(ss_ref)

    ss_ref[...] += jnp.sum(h * h, axis=-1, keepdims=True)


def _row_scale(ss_ref, d):
    return lax.rsqrt(ss_ref[:, :1] / d + EPS)


def _cast_specs(ws, n_steps, step_of):
    specs, shapes = [], []
    for w in ws:
        rows = w.shape[0] // n_steps
        assert rows * n_steps == w.shape[0] and rows % 16 == 0
        specs.append(pl.BlockSpec((rows, w.shape[1]), lambda *ids: (step_of(*ids), 0)))
        shapes.append(jax.ShapeDtypeStruct(w.shape, BF16))
    return specs, shapes


def _cast_slabs(src_refs, dst_refs):
    for src, dst in zip(src_refs, dst_refs):
        dst[...] = src[...].astype(dst.dtype)


def _mm_body(a_ref, w_ref, *refs):
    n_cast = len(refs) // 2
    o_ref = refs[n_cast]
    o_ref[...] = jnp.dot(a_ref[...], w_ref[...], preferred_element_type=F32).astype(o_ref.dtype)
    _cast_slabs(refs[:n_cast], refs[n_cast + 1:])


def matmul(a, w, out_dtype, tm=1024, tn=512, cast=()):
    m, k = a.shape
    n = w.shape[1]
    nj = n // tn
    cast_specs, cast_shapes = _cast_specs(cast, (m // tm) * nj, lambda i, j: i * nj + j)
    outs = pl.pallas_call(
        _mm_body,
        grid=(m // tm, nj),
        in_specs=[pl.BlockSpec((tm, k), lambda i, j: (i, 0)), pl.BlockSpec((k, tn), lambda i, j: (0, j))] + cast_specs,
        out_specs=[pl.BlockSpec((tm, tn), lambda i, j: (i, j))] + cast_specs,
        out_shape=[jax.ShapeDtypeStruct((m, n), out_dtype)] + cast_shapes,
        compiler_params=_params("parallel", "parallel"),
        name="in_proj",
    )(a, w, *cast)
    return outs[0], outs[1:]


def _mlp_up_body(a_ref, w_ref, ss_ref, *refs):
    n_cast = len(refs) // 2
    o_ref = refs[n_cast]
    acc = jnp.dot(a_ref[...], w_ref[...], preferred_element_type=F32) * _row_scale(ss_ref, a_ref.shape[1])
    o_ref[...] = jnp.square(jnp.maximum(acc, 0.0)).astype(o_ref.dtype)
    _cast_slabs(refs[:n_cast], refs[n_cast + 1:])


def mlp_up(hb, ss, w, tm=1024, tn=512, cast=()):
    m, k = hb.shape
    n = w.shape[1]
    nj = n // tn
    cast_specs, cast_shapes = _cast_specs(cast, (m // tm) * nj, lambda i, j: i * nj + j)
    outs = pl.pallas_call(
        _mlp_up_body,
        grid=(m // tm, nj),
        in_specs=[pl.BlockSpec((tm, k), lambda i, j: (i, 0)), pl.BlockSpec((k, tn), lambda i, j: (0, j)),
                  pl.BlockSpec((tm, LANES), lambda i, j: (i, 0))] + cast_specs,
        out_specs=[pl.BlockSpec((tm, tn), lambda i, j: (i, j))] + cast_specs,
        out_shape=[jax.ShapeDtypeStruct((m, n), BF16)] + cast_shapes,
        compiler_params=_params("parallel", "parallel"),
        name="mlp_up",
    )(hb, w, ss, *cast)
    return outs[0], outs[1:]


def _norm_outputs(m, n, tm, tn, index):
    specs = [pl.BlockSpec((tm, tn), index), pl.BlockSpec((tm, tn), index),
             pl.BlockSpec((tm, LANES), lambda i, *_: (i, 0))]
    shapes = [jax.ShapeDtypeStruct((m, n), F32), jax.ShapeDtypeStruct((m, n), BF16),
              jax.ShapeDtypeStruct((m, LANES), F32)]
    return specs, shapes


def _out_proj_body(a1_ref, a2_ref, w1_ref, w2_ref, r_ref, g_ref, o_ref, hb_ref, ss_ref):
    acc = jnp.dot(a1_ref[...], w1_ref[...], preferred_element_type=F32)
    acc = acc + jnp.dot(a2_ref[...], w2_ref[...], preferred_element_type=F32)
    h = r_ref[...] + acc
    o_ref[...] = h
    _emit_norm_inputs(h, g_ref, hb_ref, ss_ref, pl.program_id(1) == 0)


def out_proj_residual(a1, a2, w, res, gain, tm=1024, tn=512):
    m, k1 = a1.shape
    k2 = a2.shape[1]
    assert k1 == k2 and w.shape[0] == k1 + k2
    n = w.shape[1]
    tile = lambda i, j: (i, j)
    out_specs, out_shape = _norm_outputs(m, n, tm, tn, tile)
    return pl.pallas_call(
        _out_proj_body,
        grid=(m // tm, n // tn),
        in_specs=[pl.BlockSpec((tm, k1), lambda i, j: (i, 0)), pl.BlockSpec((tm, k2), lambda i, j: (i, 0)),
                  pl.BlockSpec((k1, tn), lambda i, j: (0, j)), pl.BlockSpec((k2, tn), lambda i, j: (1, j)),
                  pl.BlockSpec((tm, tn), tile), pl.BlockSpec((1, tn), lambda i, j: (0, j))],
        out_specs=out_specs,
        out_shape=out_shape,
        compiler_params=_params("parallel", "arbitrary"),
        name="out_proj",
    )(a1, a2, w, w, res, gain.reshape(1, n))


def _mlp_down_body(a_ref, w_ref, r_ref, g_ref, o_ref, hb_ref, ss_ref, acc_ref):
    k = pl.program_id(2)

    @pl.when(k == 0)
    def _():
        acc_ref[...] = jnp.zeros_like(acc_ref)

    acc_ref[...] += jnp.dot(a_ref[...], w_ref[...], preferred_element_type=F32)

    @pl.when(k == pl.num_programs(2) - 1)
    def _():
        h = r_ref[...] + acc_ref[...]
        o_ref[...] = h
        _emit_norm_inputs(h, g_ref, hb_ref, ss_ref, pl.program_id(1) == 0)


def mlp_down_residual(a, w, res, gain, tm=1024, tn=1024, tk=2048):
    m, k = a.shape
    n = w.shape[1]
    tile = lambda i, j, kk: (i, j)
    out_specs, out_shape = _norm_outputs(m, n, tm, tn, tile)
    return pl.pallas_call(
        _mlp_down_body,
        grid=(m // tm, n // tn, k // tk),
        in_specs=[pl.BlockSpec((tm, tk), lambda i, j, kk: (i, kk)), pl.BlockSpec((tk, tn), lambda i, j, kk: (kk, j)),
                  pl.BlockSpec((tm, tn), tile), pl.BlockSpec((1, tn), lambda i, j, kk: (0, j))],
        out_specs=out_specs,
        out_shape=out_shape,
        scratch_shapes=[pltpu.VMEM((tm, tn), F32)],
        compiler_params=_params("parallel", "arbitrary", "arbitrary"),
        name="mlp_down",
    )(a, w, res, gain.reshape(1, n))


def _gate_body(a_ref, w_ref, ss_ref, h_ref, e_ref, o_ref):
    acc = jnp.dot(a_ref[...], w_ref[...], preferred_element_type=F32) * _row_scale(ss_ref, a_ref.shape[1])
    o_ref[...] = h_ref[...] + jax.nn.sigmoid(acc) * e_ref[...]


def gate_residual(hb, ss, w, h, e, tm=1024, tn=512):
    m, k = hb.shape
    n = w.shape[1]
    return pl.pallas_call(
        _gate_body,
        grid=(m // tm, n // tn),
        in_specs=[pl.BlockSpec((tm, k), lambda i, j: (i, 0)), pl.BlockSpec((k, tn), lambda i, j: (0, j)),
                  pl.BlockSpec((tm, LANES), lambda i, j: (i, 0)),
                  pl.BlockSpec((tm, tn), lambda i, j: (i, j)), pl.BlockSpec((tm, tn), lambda i, j: (i, j))],
        out_specs=pl.BlockSpec((tm, tn), lambda i, j: (i, j)),
        out_shape=jax.ShapeDtypeStruct((m, n), F32),
        compiler_params=_params("parallel", "parallel"),
        name="ple_gate",
    )(hb, w, ss, h, e)


def _embed_body(p_ref, w_ref, g_ref, o_ref):
    y = jnp.dot(p_ref[...].astype(BF16), w_ref[...], preferred_element_type=F32)
    o_ref[...] = _rms(y, g_ref[...])


def embed_norm(p, w, g, tm=256):
    m, k = p.shape
    n = w.shape[1]
    return pl.pallas_call(
        _embed_body,
        grid=(m // tm,),
        in_specs=[pl.BlockSpec((tm, k), lambda i: (i, 0)), pl.BlockSpec((k, n), lambda i: (0, 0)),
                  pl.BlockSpec((1, n), lambda i: (0, 0))],
        out_specs=pl.BlockSpec((tm, n), lambda i: (i, 0)),
        out_shape=jax.ShapeDtypeStruct((m, n), F32),
        compiler_params=_params("parallel"),
        name="ple_embed",
    )(p, w, g.reshape(1, n))


def _mla_prep_body(pa_ref, pos_ref, gq_ref, gkv_ref, wq_ref, wk_ref, wvt_ref, freq_ref, sign_ref, q_ref, k_ref, vt_ref):
    pa = pa_ref[...]
    cq = _rms(pa[:, :Q_LORA], gq_ref[...]).astype(BF16)
    ckv = _rms(pa[:, Q_LORA:Q_LORA + KV_LORA], gkv_ref[...]).astype(BF16)
    kr = pa[:, Q_LORA + KV_LORA:Q_LORA + KV_LORA + LANES]
    krs = pa[:, Q_LORA + KV_LORA + LANES:]
    ang = pos_ref[...].astype(F32) * freq_ref[...]
    cos = jnp.cos(ang)
    sin = jnp.sin(ang) * sign_ref[...]
    q = jnp.dot(cq, wq_ref[...], preferred_element_type=F32)
    kn = jnp.dot(ckv, wk_ref[...], preferred_element_type=F32)
    vt_ref[...] = lax.dot_general(wvt_ref[...], ckv, (((1,), (1,)), ((), ())),
                                  preferred_element_type=F32).astype(BF16)
    krot = (kr * cos + krs * sin).astype(BF16)
    qscale = (QK_HEAD ** -0.5) * LOG2_E
    hn = MLA_HEADS * LANES
    for h in range(MLA_HEADS):
        lo, hi = h * LANES, (h + 1) * LANES
        qrot = q[:, hn + lo:hn + hi] * cos + q[:, 2 * hn + lo:2 * hn + hi] * sin
        q_ref[:, h * QK_PAD:h * QK_PAD + LANES] = (q[:, lo:hi] * qscale).astype(BF16)
        q_ref[:, h * QK_PAD + LANES:(h + 1) * QK_PAD] = (qrot * qscale).astype(BF16)
        k_ref[:, h * QK_PAD:h * QK_PAD + LANES] = kn[:, lo:hi].astype(BF16)
        k_ref[:, h * QK_PAD + LANES:(h + 1) * QK_PAD] = krot


def mla_prep(proj, pos, gq, gkv, wq, wk, wvt, freq, sign, tm=256):
    n = proj.shape[0]
    const = lambda i: (0, 0)
    return pl.pallas_call(
        _mla_prep_body,
        grid=(n // tm,),
        in_specs=[pl.BlockSpec((tm, A_WIDTH), lambda i: (i, 0)), pl.BlockSpec((tm, 1), lambda i: (i, 0)),
                  pl.BlockSpec((1, Q_LORA), const), pl.BlockSpec((1, KV_LORA), const),
                  pl.BlockSpec(wq.shape, const), pl.BlockSpec(wk.shape, const), pl.BlockSpec(wvt.shape, const),
                  pl.BlockSpec((1, LANES), const), pl.BlockSpec((1, LANES), const)],
        out_specs=[pl.BlockSpec((tm, MLA_HEADS * QK_PAD), lambda i: (i, 0)),
                   pl.BlockSpec((tm, MLA_HEADS * QK_PAD), lambda i: (i, 0)),
                   pl.BlockSpec((MLA_HEADS * V_HEAD, tm), lambda i: (0, i))],
        out_shape=[jax.ShapeDtypeStruct((n, MLA_HEADS * QK_PAD), BF16),
                   jax.ShapeDtypeStruct((n, MLA_HEADS * QK_PAD), BF16),
                   jax.ShapeDtypeStruct((MLA_HEADS * V_HEAD, n), BF16)],
        compiler_params=_params("parallel"),
        name="mla_prep",
    )(proj, pos, gq.reshape(1, -1), gkv.reshape(1, -1), wq, wk, wvt, freq, sign)


def _attn_body(q_ref, k_ref, vt_ref, *refs, tq, tk, n_cast):
    o_ref = refs[n_cast]
    s_ref, mx_ref, m_ref, l_ref, acc_ref = refs[2 * n_cast + 1:]
    _cast_slabs(refs[:n_cast], refs[n_cast + 1:2 * n_cast + 1])
    qi = pl.program_id(2)
    q = q_ref[...]
    n_diag = tq // tk
    n_full = qi * n_diag

    def produce(j):
        k = k_ref[pl.ds(pl.multiple_of(j * tk, tk), tk), :]
        s = lax.dot_general(k, q, (((1,), (1,)), ((), ())), preferred_element_type=F32)
        return s, jnp.max(s, axis=0, keepdims=True)

    def consume(j, diag):
        s = s_ref[...]
        if diag is None:
            tile_max = mx_ref[...]
        else:
            key = diag * tk + lax.broadcasted_iota(jnp.int32, (tk, tq), 0)
            qry = lax.broadcasted_iota(jnp.int32, (tk, tq), 1)
            s = jnp.where(key <= qry, s, -jnp.inf)
            tile_max = jnp.max(s, axis=0, keepdims=True)
        m = m_ref[...]
        m_new = jnp.maximum(m, tile_max)
        alpha = jnp.exp2(m - m_new)
        p = jnp.exp2(s - m_new)
        m_ref[...] = m_new
        l_ref[...] = alpha * l_ref[...] + jnp.sum(p, axis=0, keepdims=True)
        vt = vt_ref[:, pl.ds(pl.multiple_of(j * tk, tk), tk)]
        acc_ref[...] = alpha * acc_ref[...] + jnp.dot(vt, p.astype(BF16), preferred_element_type=F32)

    m_ref[...] = jnp.full(m_ref.shape, -jnp.inf, F32)
    l_ref[...] = jnp.zeros(l_ref.shape, F32)
    acc_ref[...] = jnp.zeros(acc_ref.shape, F32)
    s_ref[...], mx_ref[...] = produce(0)

    def body(j, carry):
        s_next, mx_next = produce(j + 1)
        consume(j, None)
        s_ref[...] = s_next
        mx_ref[...] = mx_next
        return carry

    lax.fori_loop(0, n_full, body, 0)
    for d in range(n_diag):
        if d + 1 < n_diag:
            s_next, _ = produce(n_full + d + 1)
        consume(n_full + d, d)
        if d + 1 < n_diag:
            s_ref[...] = s_next
    o_ref[...] = (acc_ref[...] / l_ref[...]).T.astype(o_ref.dtype)


def causal_attention(q, k, vt, batch, seq, tq=1024, tk=512, cast=()):
    nq = seq // tq
    cast_specs, cast_shapes = _cast_specs(cast, batch * MLA_HEADS * nq, lambda b, h, i: (b * MLA_HEADS + h) * nq + i)
    outs = pl.pallas_call(
        functools.partial(_attn_body, tq=tq, tk=tk, n_cast=len(cast)),
        grid=(batch, MLA_HEADS, nq),
        in_specs=[pl.BlockSpec((tq, QK_PAD), lambda b, h, i: (b * nq + i, h)),
                  pl.BlockSpec((seq, QK_PAD), lambda b, h, i: (b, h)),
                  pl.BlockSpec((V_HEAD, seq), lambda b, h, i: (h, b))] + cast_specs,
        out_specs=[pl.BlockSpec((tq, V_HEAD), lambda b, h, i: (b * nq + i, h))] + cast_specs,
        out_shape=[jax.ShapeDtypeStruct((batch * seq, MLA_HEADS * V_HEAD), BF16)] + cast_shapes,
        scratch_shapes=[pltpu.VMEM((tk, tq), F32)] + [pltpu.VMEM((1, tq), F32)] * 3 + [pltpu.VMEM((V_HEAD, tq), F32)],
        compiler_params=_params("parallel", "parallel", "arbitrary"),
        name="mla_attention",
    )(q, k, vt, *cast)
    return outs[0], outs[1:]


def _hgrn_constants(c):
    t = np.arange(c)
    ltri = (t[None, :] <= t[:, None]).astype(np.float32)
    nlev = int(np.log2(c))
    x = t[:, None] ^ t[None, :]
    level = np.full((c, c), -1, np.int32)
    lower = t[:, None] > t[None, :]
    level[lower] = (nlev - 1) - np.floor(np.log2(x[lower])).astype(np.int32)
    level[t, t] = nlev
    return np.concatenate([ltri, ltri, ltri], axis=1), level, nlev


def _block_ref(b, half):
    c, dk = b.shape
    if half >= 4:
        x = b.reshape(c // (2 * half), 2 * half, dk)
        return jnp.broadcast_to(x[:, half - 1:half, :], x.shape).reshape(c, dk)
    assert half == 2
    x = b.reshape(c // 8, 8, dk)
    sub = lax.broadcasted_iota(jnp.int32, x.shape, 1)
    return jnp.where(sub < 4, x[:, 1:2, :], x[:, 5:6, :]).reshape(c, dk)


def _hgrn_body(hq_ref, hf_ref, hi_ref, hg_ref, lbp_ref, gn_ref, w3_ref, lev_ref, o_ref, st_ref, *, layer, tile, chunk, nlev):
    @pl.when(pl.program_id(2) == 0)
    def _():
        st_ref[...] = jnp.zeros_like(st_ref)

    hb = lbp_ref[...]
    ex = jnp.exp(hb - jnp.max(hb, axis=0, keepdims=True))
    sm = ex / jnp.sum(ex, axis=0, keepdims=True)
    lb = jnp.sum(sm[:layer + 1], axis=0, keepdims=True)
    oml = 1.0 - lb
    lev = lev_ref[...]
    gn = gn_ref[...]
    trans_b = (((1,), (1,)), ((), ()))
    chunks = [slice(c * chunk, (c + 1) * chunk) for c in range(tile // chunk)]
    fs, qs, kks, vs, bs = [], [], [], [], []
    for rows in chunks:
        z = hf_ref[rows, :]
        f = lb + oml * jax.nn.sigmoid(z)
        g = jnp.log(f)
        hq = hq_ref[rows, :]
        fs.append(f)
        kks.append(oml * jax.nn.sigmoid(-z))
        qs.append(hq * jax.nn.sigmoid(hq))
        vs.append(hi_ref[rows, :].astype(BF16))
        g_hi = g.astype(BF16)
        r1 = g - g_hi.astype(F32)
        g_mid = r1.astype(BF16)
        g_lo = (r1 - g_mid.astype(F32)).astype(BF16)
        g3 = jnp.concatenate([g_hi, g_mid, g_lo], axis=0)
        bs.append(jnp.dot(w3_ref[...], g3, preferred_element_type=F32))
    scores = [jnp.zeros((chunk, chunk), F32) for _ in chunks]
    q16 = [q.astype(BF16) for q in qs]
    k16 = [kk.astype(BF16) for kk in kks]
    for lv in range(nlev + 1):
        half = chunk >> (lv + 1)
        for c, (f, q, kk, b) in enumerate(zip(fs, q16, k16, bs)):
            if half >= 2:
                e = jnp.exp(-jnp.abs(b - _block_ref(b, half))).astype(BF16)
                ql, kl = q * e, kk * e
            elif half == 1:
                ql, kl = q * f.astype(BF16), kk
            else:
                ql, kl = q, kk
            p = lax.dot_general(ql, kl, trans_b, preferred_element_type=F32)
            scores[c] = jnp.where(lev == lv, p, scores[c])
    intra, updates, decays, qbs = [], [], [], []
    for a, q, kk, v, b in zip(scores, qs, kks, vs, bs):
        intra.append(jnp.dot(a.astype(BF16), v, preferred_element_type=F32))
        b_last = b[chunk - 1:chunk, :]
        kh = (kk * jnp.exp(b_last - b)).astype(BF16)
        updates.append(lax.dot_general(v, kh, (((0,), (0,)), ((), ())), preferred_element_type=F32))
        decays.append(jnp.exp(b_last))
        qbs.append((q * jnp.exp(b)).astype(BF16))
    state = st_ref[...]
    for rows, qb, o_intra, upd, dec in zip(chunks, qbs, intra, updates, decays):
        o = lax.dot_general(qb, state.astype(BF16), trans_b, preferred_element_type=F32) + o_intra
        state = dec * state + upd
        hg = hg_ref[rows, :]
        y = _rms(o, gn) * (hg * jax.nn.sigmoid(hg))
        o_ref[rows, :] = y.astype(o_ref.dtype)
    st_ref[...] = state


def hgrn2(proj, col0, lbp, gn, batch, seq, layer, tile=1024):
    w3, level, nlev = _hgrn_constants(HG_CHUNK)
    w3 = jnp.asarray(w3, BF16)
    level = jnp.asarray(level)
    nt = seq // tile
    cb = col0 // LANES

    def col(group):
        return lambda b, h, i: (b * nt + i, cb + group * HG_HEADS + h)

    return pl.pallas_call(
        functools.partial(_hgrn_body, layer=layer, tile=tile, chunk=HG_CHUNK, nlev=nlev),
        grid=(batch, HG_HEADS, nt),
        in_specs=[pl.BlockSpec((tile, LANES), col(0)), pl.BlockSpec((tile, LANES), col(1)),
                  pl.BlockSpec((tile, LANES), col(2)), pl.BlockSpec((tile, LANES), col(3)),
                  pl.BlockSpec((lbp.shape[0], LANES), lambda b, h, i: (0, h)),
                  pl.BlockSpec((1, LANES), lambda b, h, i: (0, h)),
                  pl.BlockSpec(w3.shape, lambda b, h, i: (0, 0)),
                  pl.BlockSpec(level.shape, lambda b, h, i: (0, 0))],
        out_specs=pl.BlockSpec((tile, LANES), lambda b, h, i: (b * nt + i, h)),
        out_shape=jax.ShapeDtypeStruct((batch * seq, HG_VDIM), BF16),
        scratch_shapes=[pltpu.VMEM((HG_DV, HG_DK), F32)],
        compiler_params=_params("parallel", "parallel", "arbitrary"),
        name="hgrn2",
    )(proj, proj, proj, proj, lbp, gn.reshape(1, -1), w3, level)


def _swap_halves(w):
    half = w.shape[-1] // 2
    return jnp.concatenate([w[..., half:], w[..., :half]], axis=-1)


def _layer_weights(w_in, w_uq, w_ukv):
    d = w_in.shape[0]
    n_a = Q_LORA + KV_LORA
    kr = w_in[:, n_a:n_a + QK_ROPE]
    zero = jnp.zeros((d, LANES - QK_ROPE), w_in.dtype)
    w_a = jnp.concatenate([w_in[:, :n_a + QK_ROPE], zero, _swap_halves(kr), zero], axis=1)
    w_h = w_in[:, n_a + QK_ROPE:]
    wq = w_uq.reshape(Q_LORA, MLA_HEADS, QK_HEAD)
    rope = wq[:, :, QK_NOPE:]
    pad = jnp.zeros((Q_LORA, MLA_HEADS, LANES - QK_ROPE), w_uq.dtype)
    wq_all = jnp.concatenate([
        wq[:, :, :QK_NOPE].reshape(Q_LORA, -1),
        jnp.concatenate([rope, pad], axis=-1).reshape(Q_LORA, -1),
        jnp.concatenate([_swap_halves(rope), pad], axis=-1).reshape(Q_LORA, -1)], axis=1)
    wkv = w_ukv.reshape(KV_LORA, MLA_HEADS, QK_NOPE + V_HEAD)
    wk = wkv[:, :, :QK_NOPE].reshape(KV_LORA, -1)
    wvt = wkv[:, :, QK_NOPE:].reshape(KV_LORA, -1).T
    return w_a.astype(BF16), w_h.astype(BF16), wq_all.astype(BF16), wk.astype(BF16), wvt.astype(BF16)


def _rope_rows():
    inv_freq = ROPE_THETA ** (-jnp.arange(0, QK_ROPE, 2, dtype=F32) / QK_ROPE)
    zero = jnp.zeros((LANES - QK_ROPE,), F32)
    freq = jnp.concatenate([inv_freq, inv_freq, zero]).reshape(1, LANES)
    half = jnp.ones((QK_ROPE // 2,), F32)
    sign = jnp.concatenate([-half, half, zero]).reshape(1, LANES)
    return freq, sign


def kernel(x, p, positions, norm_mix, w_in, q_a_norm, kv_a_norm, w_uq, w_ukv, hg_lower_bound, hg_out_norm, w_o,
           norm_mlp, w_up, w_down, norm_ple, w_ple_gate, w_ple, ple_post_norm, final_norm):
    batch, seq, d_model = x.shape
    n = batch * seq
    depth = w_in.shape[0]
    h = x.reshape(n, d_model)
    pos = positions.reshape(n, 1)
    freq, sign = _rope_rows()
    for i in range(depth):
        w_a, w_h, wq_all, wk, wvt = _layer_weights(w_in[i], w_uq[i], w_ukv[i])
        u = rmsnorm(h, norm_mix[i], BF16)
        proj_a, _ = matmul(u, w_a, F32)
        proj_h, (w_up_b,) = matmul(u, w_h, F32, tn=1024, cast=(w_up[i],))
        q, k, vt = mla_prep(proj_a, pos, q_a_norm[i], kv_a_norm[i], wq_all, wk, wvt, freq, sign)
        o_mla, (w_o_b, w_pg_b) = causal_attention(q, k, vt, batch, seq, cast=(w_o[i], w_ple_gate[i]))
        o_hg = hgrn2(proj_h, 0, hg_lower_bound, hg_out_norm[i], batch, seq, i)
        h, hb, ss = out_proj_residual(o_mla, o_hg, w_o_b, h, norm_mlp[i])
        hidden, (w_down_b,) = mlp_up(hb, ss, w_up_b, cast=(w_down[i],))
        h, hb, ss = mlp_down_residual(hidden, w_down_b, h, norm_ple[i])
        e = embed_norm(p[i].reshape(n, -1), w_ple[i].astype(BF16), ple_post_norm[i])
        h = gate_residual(hb, ss, w_pg_b, h, e)
    return rmsnorm(h, final_norm, x.dtype).reshape(batch, seq, d_model)
```

```python
import functools

import numpy as np
import jax
import jax.numpy as jnp
from jax import lax
from jax.experimental import pallas as pl
from jax.experimental.pallas import tpu as pltpu

EPS = 1e-6
MLA_HEADS = 16
QK_NOPE = 128
QK_ROPE = 64
QK_HEAD = QK_NOPE + QK_ROPE
V_HEAD = 128
Q_LORA = 768
KV_LORA = 512
ROPE_THETA = 10000.0
HG_HEADS = 16
HG_DK = 128
HG_DV = 128
HG_FDIM = HG_HEADS * HG_DK
HG_VDIM = HG_HEADS * HG_DV

LANES = 128
QK_PAD = 2 * LANES
A_WIDTH = Q_LORA + KV_LORA + 2 * LANES
HG_CHUNK = 128
VMEM_LIMIT_BYTES = 56 * 1024 * 1024
LOG2_E = 1.4426950408889634

F32 = jnp.float32
BF16 = jnp.bfloat16


def _params(*semantics):
    return pltpu.CompilerParams(dimension_semantics=semantics, vmem_limit_bytes=VMEM_LIMIT_BYTES)


def _rms(x, g):
    return x * lax.rsqrt(jnp.mean(x * x, axis=-1, keepdims=True) + EPS) * g


def _rmsnorm_body(x_ref, g_ref, o_ref):
    o_ref[...] = _rms(x_ref[...], g_ref[...]).astype(o_ref.dtype)


def rmsnorm(x, g, out_dtype, tm=256):
    n, d = x.shape
    return pl.pallas_call(
        _rmsnorm_body,
        grid=(n // tm,),
        in_specs=[pl.BlockSpec((tm, d), lambda i: (i, 0)), pl.BlockSpec((1, d), lambda i: (0, 0))],
        out_specs=pl.BlockSpec((tm, d), lambda i: (i, 0)),
        out_shape=jax.ShapeDtypeStruct((n, d), out_dtype),
        compiler_params=_params("parallel"),
        name="rmsnorm",
    )(x, g.reshape(1, d))


def _emit_norm_inputs(h, g_ref, hb_ref, ss_ref, first):
    hb_ref[...] = (h * g_ref[...]).astype(BF16)

    @pl.when(first)
    def _():
        ss_ref[...] = jnp.zeros_like(ss_ref)

    ss_ref[...] += jnp.sum(h * h, axis=-1, keepdims=True)


def _row_scale(ss_ref, d):
    return lax.rsqrt(ss_ref[:, :1] / d + EPS)


def _cast_specs(ws, n_steps, step_of):
    specs, shapes = [], []
    for w in ws:
        rows = w.shape[0] // n_steps
        assert rows * n_steps == w.shape[0] and rows % 16 == 0
        specs.append(pl.BlockSpec((rows, w.shape[1]), lambda *ids: (step_of(*ids), 0)))
        shapes.append(jax.ShapeDtypeStruct(w.shape, BF16))
    return specs, shapes


def _cast_slabs(src_refs, dst_refs):
    for src, dst in zip(src_refs, dst_refs):
        dst[...] = src[...].astype(dst.dtype)


def _scaled_mm_body(a_ref, w_ref, ss_ref, *refs, relu2):
    n_cast = len(refs) // 2
    o_ref = refs[n_cast]
    acc = jnp.dot(a_ref[...], w_ref[...], preferred_element_type=F32) * _row_scale(ss_ref, a_ref.shape[1])
    if relu2:
        acc = jnp.square(jnp.maximum(acc, 0.0))
    o_ref[...] = acc.astype(o_ref.dtype)
    _cast_slabs(refs[:n_cast], refs[n_cast + 1:])


def scaled_matmul(hb, ss, w, out_dtype, name, relu2=False, tm=1024, tn=512, cast=()):
    m, k = hb.shape
    n = w.shape[1]
    nj = n // tn
    cast_specs, cast_shapes = _cast_specs(cast, (m // tm) * nj, lambda i, j: i * nj + j)
    outs = pl.pallas_call(
        functools.partial(_scaled_mm_body, relu2=relu2),
        grid=(m // tm, nj),
        in_specs=[pl.BlockSpec((tm, k), lambda i, j: (i, 0)), pl.BlockSpec((k, tn), lambda i, j: (0, j)),
                  pl.BlockSpec((tm, LANES), lambda i, j: (i, 0))] + cast_specs,
        out_specs=[pl.BlockSpec((tm, tn), lambda i, j: (i, j))] + cast_specs,
        out_shape=[jax.ShapeDtypeStruct((m, n), out_dtype)] + cast_shapes,
        compiler_params=_params("parallel", "parallel"),
        name=name,
    )(hb, w, ss, *cast)
    return outs[0], outs[1:]


def _front_body(x_ref, g_ref, wa_ref, win_ref, pa_ref, xb_ref, ss_ref, wh_ref, *, h_col0):
    @pl.when(pl.program_id(1) == 0)
    def _():
        x = x_ref[...]
        xb_ref[...] = (x * g_ref[...]).astype(BF16)
        ss_ref[...] = jnp.broadcast_to(jnp.sum(x * x, axis=-1, keepdims=True), ss_ref.shape)

    acc = jnp.dot(xb_ref[...], wa_ref[...], preferred_element_type=F32)
    pa_ref[...] = acc * _row_scale(ss_ref, x_ref.shape[1])
    wh_ref[...] = win_ref[...][:, h_col0:].astype(BF16)


def front(x, gain, w_a, w_in, h_col0, tm=256, tn=768):
    m, d = x.shape
    n = w_a.shape[1]
    nj = n // tn
    steps = (m // tm) * nj
    rows = w_in.shape[0] // steps
    assert rows * steps == w_in.shape[0] and rows % 16 == 0
    h_width = w_in.shape[1] - h_col0
    slab = lambda i, j: (i * nj + j, 0)
    row = lambda i, j: (i, 0)
    return pl.pallas_call(
        functools.partial(_front_body, h_col0=h_col0),
        grid=(m // tm, nj),
        in_specs=[pl.BlockSpec((tm, d), row), pl.BlockSpec((1, d), lambda i, j: (0, 0)),
                  pl.BlockSpec((d, tn), lambda i, j: (0, j)), pl.BlockSpec((rows, w_in.shape[1]), slab)],
        out_specs=[pl.BlockSpec((tm, tn), lambda i, j: (i, j)), pl.BlockSpec((tm, d), row),
                   pl.BlockSpec((tm, LANES), row), pl.BlockSpec((rows, h_width), slab)],
        out_shape=[jax.ShapeDtypeStruct((m, n), F32), jax.ShapeDtypeStruct((m, d), BF16),
                   jax.ShapeDtypeStruct((m, LANES), F32), jax.ShapeDtypeStruct((w_in.shape[0], h_width), BF16)],
        compiler_params=_params("parallel", "arbitrary"),
        name="front",
    )(x, gain.reshape(1, d), w_a, w_in)


def _norm_outputs(m, n, tm, tn, index):
    specs = [pl.BlockSpec((tm, tn), index), pl.BlockSpec((tm, tn), index),
             pl.BlockSpec((tm, LANES), lambda i, *_: (i, 0))]
    shapes = [jax.ShapeDtypeStruct((m, n), F32), jax.ShapeDtypeStruct((m, n), BF16),
              jax.ShapeDtypeStruct((m, LANES), F32)]
    return specs, shapes


def _out_proj_body(a1_ref, a2_ref, w1_ref, w2_ref, r_ref, g_ref, o_ref, hb_ref, ss_ref):
    acc = jnp.dot(a1_ref[...], w1_ref[...], preferred_element_type=F32)
    acc = acc + jnp.dot(a2_ref[...], w2_ref[...], preferred_element_type=F32)
    h = r_ref[...] + acc
    o_ref[...] = h
    _emit_norm_inputs(h, g_ref, hb_ref, ss_ref, pl.program_id(1) == 0)


def out_proj_residual(a1, a2, w, res, gain, tm=1024, tn=512):
    m, k1 = a1.shape
    k2 = a2.shape[1]
    assert k1 == k2 and w.shape[0] == k1 + k2
    n = w.shape[1]
    tile = lambda i, j: (i, j)
    out_specs, out_shape = _norm_outputs(m, n, tm, tn, tile)
    return pl.pallas_call(
        _out_proj_body,
        grid=(m // tm, n // tn),
        in_specs=[pl.BlockSpec((tm, k1), lambda i, j: (i, 0)), pl.BlockSpec((tm, k2), lambda i, j: (i, 0)),
                  pl.BlockSpec((k1, tn), lambda i, j: (0, j)), pl.BlockSpec((k2, tn), lambda i, j: (1, j)),
                  pl.BlockSpec((tm, tn), tile), pl.BlockSpec((1, tn), lambda i, j: (0, j))],
        out_specs=out_specs,
        out_shape=out_shape,
        compiler_params=_params("parallel", "arbitrary"),
        name="out_proj",
    )(a1, a2, w, w, res, gain.reshape(1, n))


def _mlp_down_body(a_ref, w_ref, r_ref, g_ref, o_ref, hb_ref, ss_ref, acc_ref):
    k = pl.program_id(2)

    @pl.when(k == 0)
    def _():
        acc_ref[...] = jnp.zeros_like(acc_ref)

    acc_ref[...] += jnp.dot(a_ref[...], w_ref[...], preferred_element_type=F32)

    @pl.when(k == pl.num_programs(2) - 1)
    def _():
        h = r_ref[...] + acc_ref[...]
        o_ref[...] = h
        _emit_norm_inputs(h, g_ref, hb_ref, ss_ref, pl.program_id(1) == 0)


def mlp_down_residual(a, w, res, gain, tm=1024, tn=1024, tk=2048):
    m, k = a.shape
    n = w.shape[1]
    tile = lambda i, j, kk: (i, j)
    out_specs, out_shape = _norm_outputs(m, n, tm, tn, tile)
    return pl.pallas_call(
        _mlp_down_body,
        grid=(m // tm, n // tn, k // tk),
        in_specs=[pl.BlockSpec((tm, tk), lambda i, j, kk: (i, kk)), pl.BlockSpec((tk, tn), lambda i, j, kk: (kk, j)),
                  pl.BlockSpec((tm, tn), tile), pl.BlockSpec((1, tn), lambda i, j, kk: (0, j))],
        out_specs=out_specs,
        out_shape=out_shape,
        scratch_shapes=[pltpu.VMEM((tm, tn), F32)],
        compiler_params=_params("parallel", "arbitrary", "arbitrary"),
        name="mlp_down",
    )(a, w, res, gain.reshape(1, n))


def _gate_body(a_ref, w_ref, ss_ref, h_ref, e_ref, o_ref):
    acc = jnp.dot(a_ref[...], w_ref[...], preferred_element_type=F32) * _row_scale(ss_ref, a_ref.shape[1])
    o_ref[...] = h_ref[...] + jax.nn.sigmoid(acc) * e_ref[...]


def gate_residual(hb, ss, w, h, e, tm=1024, tn=512):
    m, k = hb.shape
    n = w.shape[1]
    return pl.pallas_call(
        _gate_body,
        grid=(m // tm, n // tn),
        in_specs=[pl.BlockSpec((tm, k), lambda i, j: (i, 0)), pl.BlockSpec((k, tn), lambda i, j: (0, j)),
                  pl.BlockSpec((tm, LANES), lambda i, j: (i, 0)),
                  pl.BlockSpec((tm, tn), lambda i, j: (i, j)), pl.BlockSpec((tm, tn), lambda i, j: (i, j))],
        out_specs=pl.BlockSpec((tm, tn), lambda i, j: (i, j)),
        out_shape=jax.ShapeDtypeStruct((m, n), F32),
        compiler_params=_params("parallel", "parallel"),
        name="ple_gate",
    )(hb, w, ss, h, e)


def _embed_body(p_ref, w_ref, g_ref, o_ref):
    y = jnp.dot(p_ref[...].astype(BF16), w_ref[...], preferred_element_type=F32)
    o_ref[...] = _rms(y, g_ref[...])


def embed_norm(p, w, g, tm=256):
    m, k = p.shape
    n = w.shape[1]
    return pl.pallas_call(
        _embed_body,
        grid=(m // tm,),
        in_specs=[pl.BlockSpec((tm, k), lambda i: (i, 0)), pl.BlockSpec((k, n), lambda i: (0, 0)),
                  pl.BlockSpec((1, n), lambda i: (0, 0))],
        out_specs=pl.BlockSpec((tm, n), lambda i: (i, 0)),
        out_shape=jax.ShapeDtypeStruct((m, n), F32),
        compiler_params=_params("parallel"),
        name="ple_embed",
    )(p, w, g.reshape(1, n))


def _mla_prep_body(pa_ref, pos_ref, gq_ref, gkv_ref, wq_ref, wk_ref, wvt_ref, freq_ref, sign_ref, q_ref, k_ref, vt_ref):
    pa = pa_ref[...]
    cq = _rms(pa[:, :Q_LORA], gq_ref[...]).astype(BF16)
    ckv = _rms(pa[:, Q_LORA:Q_LORA + KV_LORA], gkv_ref[...]).astype(BF16)
    kr = pa[:, Q_LORA + KV_LORA:Q_LORA + KV_LORA + LANES]
    krs = pa[:, Q_LORA + KV_LORA + LANES:]
    ang = pos_ref[...].astype(F32) * freq_ref[...]
    cos = jnp.cos(ang)
    sin = jnp.sin(ang) * sign_ref[...]
    q = jnp.dot(cq, wq_ref[...], preferred_element_type=F32)
    kn = jnp.dot(ckv, wk_ref[...], preferred_element_type=F32)
    vt_ref[...] = lax.dot_general(wvt_ref[...], ckv, (((1,), (1,)), ((), ())),
                                  preferred_element_type=F32).astype(BF16)
    krot = (kr * cos + krs * sin).astype(BF16)
    qscale = (QK_HEAD ** -0.5) * LOG2_E
    hn = MLA_HEADS * LANES
    for h in range(MLA_HEADS):
        lo, hi = h * LANES, (h + 1) * LANES
        qrot = q[:, hn + lo:hn + hi] * cos + q[:, 2 * hn + lo:2 * hn + hi] * sin
        q_ref[:, h * QK_PAD:h * QK_PAD + LANES] = (q[:, lo:hi] * qscale).astype(BF16)
        q_ref[:, h * QK_PAD + LANES:(h + 1) * QK_PAD] = (qrot * qscale).astype(BF16)
        k_ref[:, h * QK_PAD:h * QK_PAD + LANES] = kn[:, lo:hi].astype(BF16)
        k_ref[:, h * QK_PAD + LANES:(h + 1) * QK_PAD] = krot


def mla_prep(proj, pos, gq, gkv, wq, wk, wvt, freq, sign, tm=256):
    n = proj.shape[0]
    const = lambda i: (0, 0)
    return pl.pallas_call(
        _mla_prep_body,
        grid=(n // tm,),
        in_specs=[pl.BlockSpec((tm, A_WIDTH), lambda i: (i, 0)), pl.BlockSpec((tm, 1), lambda i: (i, 0)),
                  pl.BlockSpec((1, Q_LORA), const), pl.BlockSpec((1, KV_LORA), const),
                  pl.BlockSpec(wq.shape, const), pl.BlockSpec(wk.shape, const), pl.BlockSpec(wvt.shape, const),
                  pl.BlockSpec((1, LANES), const), pl.BlockSpec((1, LANES), const)],
        out_specs=[pl.BlockSpec((tm, MLA_HEADS * QK_PAD), lambda i: (i, 0)),
                   pl.BlockSpec((tm, MLA_HEADS * QK_PAD), lambda i: (i, 0)),
                   pl.BlockSpec((MLA_HEADS * V_HEAD, tm), lambda i: (0, i))],
        out_shape=[jax.ShapeDtypeStruct((n, MLA_HEADS * QK_PAD), BF16),
                   jax.ShapeDtypeStruct((n, MLA_HEADS * QK_PAD), BF16),
                   jax.ShapeDtypeStruct((MLA_HEADS * V_HEAD, n), BF16)],
        compiler_params=_params("parallel"),
        name="mla_prep",
    )(proj, pos, gq.reshape(1, -1), gkv.reshape(1, -1), wq, wk, wvt, freq, sign)


def _attn_body(q_ref, k_ref, vt_ref, *refs, tq, tk, n_cast):
    o_ref = refs[n_cast]
    s_ref, mx_ref, m_ref, l_ref, acc_ref = refs[2 * n_cast + 1:]
    _cast_slabs(refs[:n_cast], refs[n_cast + 1:2 * n_cast + 1])
    seq = q_ref.shape[0]
    n_diag = tq // tk
    gw = 2 * LANES
    tiles = [(qi, j) for qi in range(seq // tq) for j in range((qi + 1) * n_diag)]

    def keys_needed(qi, j, c):
        diag = j - qi * n_diag
        return tk if diag < 0 else max(0, min(tk, c + gw - diag * tk))

    def produce(slot, qi, j):
        for c in range(0, tq, gw):
            nk = keys_needed(qi, j, c)
            if nk == 0:
                continue
            q = q_ref[qi * tq + c:qi * tq + c + gw, :]
            s = lax.dot_general(k_ref[j * tk:j * tk + nk, :], q, (((1,), (1,)), ((), ())),
                                preferred_element_type=F32)
            s_ref[slot, :nk, c:c + gw] = s
            mx_ref[slot, :, c:c + gw] = jnp.max(s, axis=0, keepdims=True)

    def consume(slot, qi, j):
        for c in range(0, tq, gw):
            cols = slice(c, c + gw)
            nk = keys_needed(qi, j, c)
            if nk == 0:
                continue
            s = s_ref[slot, :nk, cols]
            first_key = (j - qi * n_diag) * tk
            if first_key + nk - 1 <= c:
                tile_max = mx_ref[slot, :, cols]
            else:
                key = first_key + lax.broadcasted_iota(jnp.int32, (nk, gw), 0)
                qry = c + lax.broadcasted_iota(jnp.int32, (nk, gw), 1)
                s = jnp.where(key <= qry, s, -jnp.inf)
                tile_max = jnp.max(s, axis=0, keepdims=True)
            vt = vt_ref[:, j * tk:j * tk + nk]
            if j == 0:
                m_new = tile_max
                p = jnp.exp2(s - m_new)
                l_ref[:, cols] = jnp.sum(p, axis=0, keepdims=True)
                acc_ref[:, cols] = jnp.dot(vt, p.astype(BF16), preferred_element_type=F32)
            else:
                m = m_ref[:, cols]
                m_new = jnp.maximum(m, tile_max)
                alpha = jnp.exp2(m - m_new)
                p = jnp.exp2(s - m_new)
                l_ref[:, cols] = alpha * l_ref[:, cols] + jnp.sum(p, axis=0, keepdims=True)
                acc_ref[:, cols] = alpha * acc_ref[:, cols] + jnp.dot(vt, p.astype(BF16), preferred_element_type=F32)
            m_ref[:, cols] = m_new

    produce(0, *tiles[0])
    for t, (qi, j) in enumerate(tiles):
        if t + 1 < len(tiles):
            produce((t + 1) % 2, *tiles[t + 1])
        consume(t % 2, qi, j)
        if j == (qi + 1) * n_diag - 1:
            o_ref[qi * tq:(qi + 1) * tq, :] = (acc_ref[...] / l_ref[...]).T.astype(o_ref.dtype)


def causal_attention(q, k, vt, batch, seq, tq=1024, tk=512, cast=()):
    cast_specs, cast_shapes = _cast_specs(cast, batch * MLA_HEADS, lambda b, h: b * MLA_HEADS + h)
    outs = pl.pallas_call(
        functools.partial(_attn_body, tq=tq, tk=tk, n_cast=len(cast)),
        grid=(batch, MLA_HEADS),
        in_specs=[pl.BlockSpec((seq, QK_PAD), lambda b, h: (b, h)), pl.BlockSpec((seq, QK_PAD), lambda b, h: (b, h)),
                  pl.BlockSpec((V_HEAD, seq), lambda b, h: (h, b))] + cast_specs,
        out_specs=[pl.BlockSpec((seq, V_HEAD), lambda b, h: (b, h))] + cast_specs,
        out_shape=[jax.ShapeDtypeStruct((batch * seq, MLA_HEADS * V_HEAD), BF16)] + cast_shapes,
        scratch_shapes=[pltpu.VMEM((2, tk, tq), F32), pltpu.VMEM((2, 1, tq), F32)] + [pltpu.VMEM((1, tq), F32)] * 2
        + [pltpu.VMEM((V_HEAD, tq), F32)],
        compiler_params=_params("parallel", "parallel"),
        name="mla_attention",
    )(q, k, vt, *cast)
    return outs[0], outs[1:]


def _hgrn_constants(c):
    t = np.arange(c)
    ltri = (t[None, :] <= t[:, None]).astype(np.float32)
    nlev = int(np.log2(c))
    x = t[:, None] ^ t[None, :]
    level = np.full((c, c), -1, np.int32)
    lower = t[:, None] > t[None, :]
    level[lower] = (nlev - 1) - np.floor(np.log2(x[lower])).astype(np.int32)
    level[t, t] = nlev
    return np.concatenate([ltri, ltri, ltri], axis=1), level, nlev


def _block_ref(b, half):
    c, dk = b.shape
    if half >= 4:
        x = b.reshape(c // (2 * half), 2 * half, dk)
        return jnp.broadcast_to(x[:, half - 1:half, :], x.shape).reshape(c, dk)
    assert half == 2
    x = b.reshape(c // 8, 8, dk)
    sub = lax.broadcasted_iota(jnp.int32, x.shape, 1)
    return jnp.where(sub < 4, x[:, 1:2, :], x[:, 5:6, :]).reshape(c, dk)


def _hgrn_body(hq_ref, hf_ref, hi_ref, hg_ref, lbp_ref, gn_ref, w3_ref, lev_ref, o_ref, st_ref, *, layer, tile, chunk, nlev):
    @pl.when(pl.program_id(2) == 0)
    def _():
        st_ref[...] = jnp.zeros_like(st_ref)

    hb = lbp_ref[...]
    ex = jnp.exp(hb - jnp.max(hb, axis=0, keepdims=True))
    sm = ex / jnp.sum(ex, axis=0, keepdims=True)
    lb = jnp.sum(sm[:layer + 1], axis=0, keepdims=True)
    oml = 1.0 - lb
    lev = lev_ref[...]
    gn = gn_ref[...]
    trans_b = (((1,), (1,)), ((), ()))
    chunks = [slice(c * chunk, (c + 1) * chunk) for c in range(tile // chunk)]
    fs, qs, kks, vs, bs = [], [], [], [], []
    for rows in chunks:
        z = hf_ref[rows, :]
        f = lb + oml * jax.nn.sigmoid(z)
        g = jnp.log(f)
        hq = hq_ref[rows, :]
        fs.append(f)
        kks.append(oml * jax.nn.sigmoid(-z))
        qs.append(hq * jax.nn.sigmoid(hq))
        vs.append(hi_ref[rows, :].astype(BF16))
        g_hi = g.astype(BF16)
        r1 = g - g_hi.astype(F32)
        g_mid = r1.astype(BF16)
        g_lo = (r1 - g_mid.astype(F32)).astype(BF16)
        g3 = jnp.concatenate([g_hi, g_mid, g_lo], axis=0)
        bs.append(jnp.dot(w3_ref[...], g3, preferred_element_type=F32))
    scores = [jnp.zeros((chunk, chunk), F32) for _ in chunks]
    q16 = [q.astype(BF16) for q in qs]
    k16 = [kk.astype(BF16) for kk in kks]
    for lv in range(nlev + 1):
        half = chunk >> (lv + 1)
        for c, (f, q, kk, b) in enumerate(zip(fs, q16, k16, bs)):
            if half >= 2:
                e = jnp.exp(-jnp.abs(b - _block_ref(b, half))).astype(BF16)
                ql, kl = q * e, kk * e
            elif half == 1:
                ql, kl = q * f.astype(BF16), kk
            else:
                ql, kl = q, kk
            p = lax.dot_general(ql, kl, trans_b, preferred_element_type=F32)
            scores[c] = jnp.where(lev == lv, p, scores[c])
    intra, updates, decays, qbs = [], [], [], []
    for a, q, kk, v, b in zip(scores, qs, kks, vs, bs):
        intra.append(jnp.dot(a.astype(BF16), v, preferred_element_type=F32))
        b_last = b[chunk - 1:chunk, :]
        kh = (kk * jnp.exp(b_last - b)).astype(BF16)
        updates.append(lax.dot_general(v, kh, (((0,), (0,)), ((), ())), preferred_element_type=F32))
        decays.append(jnp.exp(b_last))
        qbs.append((q * jnp.exp(b)).astype(BF16))
    state = st_ref[...]
    for rows, qb, o_intra, upd, dec in zip(chunks, qbs, intra, updates, decays):
        o = lax.dot_general(qb, state.astype(BF16), trans_b, preferred_element_type=F32) + o_intra
        state = dec * state + upd
        hg = hg_ref[rows, :]
        y = _rms(o, gn) * (hg * jax.nn.sigmoid(hg))
        o_ref[rows, :] = y.astype(o_ref.dtype)
    st_ref[...] = state


def hgrn2(proj, col0, lbp, gn, batch, seq, layer, tile=1024):
    w3, level, nlev = _hgrn_constants(HG_CHUNK)
    w3 = jnp.asarray(w3, BF16)
    level = jnp.asarray(level)
    nt = seq // tile
    cb = col0 // LANES

    def col(group):
        return lambda b, h, i: (b * nt + i, cb + group * HG_HEADS + h)

    return pl.pallas_call(
        functools.partial(_hgrn_body, layer=layer, tile=tile, chunk=HG_CHUNK, nlev=nlev),
        grid=(batch, HG_HEADS, nt),
        in_specs=[pl.BlockSpec((tile, LANES), col(0)), pl.BlockSpec((tile, LANES), col(1)),
                  pl.BlockSpec((tile, LANES), col(2)), pl.BlockSpec((tile, LANES), col(3)),
                  pl.BlockSpec((lbp.shape[0], LANES), lambda b, h, i: (0, h)),
                  pl.BlockSpec((1, LANES), lambda b, h, i: (0, h)),
                  pl.BlockSpec(w3.shape, lambda b, h, i: (0, 0)),
                  pl.BlockSpec(level.shape, lambda b, h, i: (0, 0))],
        out_specs=pl.BlockSpec((tile, LANES), lambda b, h, i: (b * nt + i, h)),
        out_shape=jax.ShapeDtypeStruct((batch * seq, HG_VDIM), BF16),
        scratch_shapes=[pltpu.VMEM((HG_DV, HG_DK), F32)],
        compiler_params=_params("parallel", "parallel", "arbitrary"),
        name="hgrn2",
    )(proj, proj, proj, proj, lbp, gn.reshape(1, -1), w3, level)


def _swap_halves(w):
    half = w.shape[-1] // 2
    return jnp.concatenate([w[..., half:], w[..., :half]], axis=-1)


def _layer_weights(w_in, w_uq, w_ukv):
    d = w_in.shape[0]
    n_a = Q_LORA + KV_LORA
    kr = w_in[:, n_a:n_a + QK_ROPE]
    zero = jnp.zeros((d, LANES - QK_ROPE), w_in.dtype)
    w_a = jnp.concatenate([w_in[:, :n_a + QK_ROPE], zero, _swap_halves(kr), zero], axis=1)
    wq = w_uq.reshape(Q_LORA, MLA_HEADS, QK_HEAD)
    rope = wq[:, :, QK_NOPE:]
    pad = jnp.zeros((Q_LORA, MLA_HEADS, LANES - QK_ROPE), w_uq.dtype)
    wq_all = jnp.concatenate([
        wq[:, :, :QK_NOPE].reshape(Q_LORA, -1),
        jnp.concatenate([rope, pad], axis=-1).reshape(Q_LORA, -1),
        jnp.concatenate([_swap_halves(rope), pad], axis=-1).reshape(Q_LORA, -1)], axis=1)
    wkv = w_ukv.reshape(KV_LORA, MLA_HEADS, QK_NOPE + V_HEAD)
    wk = wkv[:, :, :QK_NOPE].reshape(KV_LORA, -1)
    wvt = wkv[:, :, QK_NOPE:].reshape(KV_LORA, -1).T
    return w_a.astype(BF16), wq_all.astype(BF16), wk.astype(BF16), wvt.astype(BF16)


def _rope_rows():
    inv_freq = ROPE_THETA ** (-jnp.arange(0, QK_ROPE, 2, dtype=F32) / QK_ROPE)
    zero = jnp.zeros((LANES - QK_ROPE,), F32)
    freq = jnp.concatenate([inv_freq, inv_freq, zero]).reshape(1, LANES)
    half = jnp.ones((QK_ROPE // 2,), F32)
    sign = jnp.concatenate([-half, half, zero]).reshape(1, LANES)
    return freq, sign


def kernel(x, p, positions, norm_mix, w_in, q_a_norm, kv_a_norm, w_uq, w_ukv, hg_lower_bound, hg_out_norm, w_o,
           norm_mlp, w_up, w_down, norm_ple, w_ple_gate, w_ple, ple_post_norm, final_norm):
    batch, seq, d_model = x.shape
    n = batch * seq
    depth = w_in.shape[0]
    h = x.reshape(n, d_model)
    pos = positions.reshape(n, 1)
    freq, sign = _rope_rows()
    for i in range(depth):
        w_a, wq_all, wk, wvt = _layer_weights(w_in[i], w_uq[i], w_ukv[i])
        proj_a, xb, ss, w_h = front(h, norm_mix[i], w_a, w_in[i], Q_LORA + KV_LORA + QK_ROPE)
        proj_h, (w_up_b,) = scaled_matmul(xb, ss, w_h, F32, "in_proj_h", tn=1024, cast=(w_up[i],))
        q, k, vt = mla_prep(proj_a, pos, q_a_norm[i], kv_a_norm[i], wq_all, wk, wvt, freq, sign)
        o_mla, (w_o_b, w_pg_b) = causal_attention(q, k, vt, batch, seq, cast=(w_o[i], w_ple_gate[i]))
        o_hg = hgrn2(proj_h, 0, hg_lower_bound, hg_out_norm[i], batch, seq, i)
        h, hb, ss = out_proj_residual(o_mla, o_hg, w_o_b, h, norm_mlp[i])
        hidden, (w_down_b,) = scaled_matmul(hb, ss, w_up_b, BF16, "mlp_up", relu2=True, cast=(w_down[i],))
        h, hb, ss = mlp_down_residual(hidden, w_down_b, h, norm_ple[i])
        e = embed_norm(p[i].reshape(n, -1), w_ple[i].astype(BF16), ple_post_norm[i])
        h = gate_residual(hb, ss, w_pg_b, h, e)
    return rmsnorm(h, final_norm, x.dtype).reshape(batch, seq, d_model)
```

```python
import functools

import numpy as np
import jax
import jax.numpy as jnp
from jax import lax
from jax.experimental import pallas as pl
from jax.experimental.pallas import tpu as pltpu

EPS = 1e-6
MLA_HEADS = 16
QK_NOPE = 128
QK_ROPE = 64
QK_HEAD = QK_NOPE + QK_ROPE
V_HEAD = 128
Q_LORA = 768
KV_LORA = 512
ROPE_THETA = 10000.0
HG_HEADS = 16
HG_DK = 128
HG_DV = 128
HG_FDIM = HG_HEADS * HG_DK
HG_VDIM = HG_HEADS * HG_DV

LANES = 128
QK_PAD = 2 * LANES
A_WIDTH = Q_LORA + KV_LORA + 2 * LANES
HG_CHUNK = 128
VMEM_LIMIT_BYTES = 56 * 1024 * 1024
LOG2_E = 1.4426950408889634

F32 = jnp.float32
BF16 = jnp.bfloat16


def _params(*semantics):
    return pltpu.CompilerParams(dimension_semantics=semantics, vmem_limit_bytes=VMEM_LIMIT_BYTES)


def _rms(x, g):
    return x * lax.rsqrt(jnp.mean(x * x, axis=-1, keepdims=True) + EPS) * g


def _rmsnorm_body(x_ref, g_ref, o_ref):
    o_ref[...] = _rms(x_ref[...], g_ref[...]).astype(o_ref.dtype)


def rmsnorm(x, g, out_dtype, tm=256):
    n, d = x.shape
    return pl.pallas_call(
        _rmsnorm_body,
        grid=(n // tm,),
        in_specs=[pl.BlockSpec((tm, d), lambda i: (i, 0)), pl.BlockSpec((1, d), lambda i: (0, 0))],
        out_specs=pl.BlockSpec((tm, d), lambda i: (i, 0)),
        out_shape=jax.ShapeDtypeStruct((n, d), out_dtype),
        compiler_params=_params("parallel"),
        name="rmsnorm",
    )(x, g.reshape(1, d))


def _emit_norm_inputs(h, g_ref, hb_ref, ss_ref, first):
    hb_ref[...] = (h * g_ref[...]).astype(BF16)

    @pl.when(first)
    def _():
        ss_ref[...] = jnp.zeros_like(ss_ref)

    ss_ref[...] += jnp.sum(h * h, axis=-1, keepdims=True)


def _row_scale(ss_ref, d):
    return lax.rsqrt(ss_ref[:, :1] / d + EPS)


def _cast_specs(ws, n_steps, step_of):
    specs, shapes = [], []
    for w in ws:
        rows = w.shape[0] // n_steps
        assert rows * n_steps == w.shape[0] and rows % 16 == 0
        specs.append(pl.BlockSpec((rows, w.shape[1]), lambda *ids: (step_of(*ids), 0)))
        shapes.append(jax.ShapeDtypeStruct(w.shape, BF16))
    return specs, shapes


def _cast_slabs(src_refs, dst_refs):
    for src, dst in zip(src_refs, dst_refs):
        dst[...] = src[...].astype(dst.dtype)


TRANS_B = (((1,), (1,)), ((), ()))


def _scaled_mm_body(a_ref, w_ref, ss_ref, *refs, relu2, w_transposed):
    n_cast = len(refs) // 2
    o_ref = refs[n_cast]
    if w_transposed:
        acc = lax.dot_general(a_ref[...], w_ref[...], TRANS_B, preferred_element_type=F32)
    else:
        acc = jnp.dot(a_ref[...], w_ref[...], preferred_element_type=F32)
    acc = acc * _row_scale(ss_ref, a_ref.shape[1])
    if relu2:
        acc = jnp.square(jnp.maximum(acc, 0.0))
    o_ref[...] = acc.astype(o_ref.dtype)
    _cast_slabs(refs[:n_cast], refs[n_cast + 1:])


def scaled_matmul(hb, ss, w, out_dtype, name, relu2=False, w_transposed=False, tm=1024, tn=512, cast=()):
    m, k = hb.shape
    n = w.shape[0] if w_transposed else w.shape[1]
    nj = n // tn
    cast_specs, cast_shapes = _cast_specs(cast, (m // tm) * nj, lambda i, j: i * nj + j)
    w_spec = pl.BlockSpec((tn, k), lambda i, j: (j, 0)) if w_transposed else pl.BlockSpec((k, tn), lambda i, j: (0, j))
    outs = pl.pallas_call(
        functools.partial(_scaled_mm_body, relu2=relu2, w_transposed=w_transposed),
        grid=(m // tm, nj),
        in_specs=[pl.BlockSpec((tm, k), lambda i, j: (i, 0)), w_spec,
                  pl.BlockSpec((tm, LANES), lambda i, j: (i, 0))] + cast_specs,
        out_specs=[pl.BlockSpec((tm, tn), lambda i, j: (i, j))] + cast_specs,
        out_shape=[jax.ShapeDtypeStruct((m, n), out_dtype)] + cast_shapes,
        compiler_params=_params("parallel", "parallel"),
        name=name,
    )(hb, w, ss, *cast)
    return outs[0], outs[1:]


def _front_body(x_ref, g_ref, wat_ref, wt_lo_ref, wt_hi_ref, pa_ref, xb_ref, ss_ref, wht_ref):
    x = x_ref[...]
    xb = (x * g_ref[...]).astype(BF16)
    xb_ref[...] = xb
    ss = jnp.sum(x * x, axis=-1, keepdims=True)
    ss_ref[...] = jnp.broadcast_to(ss, ss_ref.shape)
    acc = lax.dot_general(xb, wat_ref[...], TRANS_B, preferred_element_type=F32)
    pa_ref[...] = acc * lax.rsqrt(ss / x.shape[1] + EPS)
    half = wt_lo_ref.shape[0]
    wht_ref[:half, :] = wt_lo_ref[...].astype(BF16)
    wht_ref[half:, :] = wt_hi_ref[...].astype(BF16)


def front(x, gain, w_at, w_t, h_row0, tm=256):
    m, d = x.shape
    n = w_at.shape[0]
    steps = m // tm
    h_rows = w_t.shape[0] - h_row0
    half = h_rows // (2 * steps)
    assert 2 * half * steps == h_rows and half % 16 == 0 and h_row0 % half == 0
    first = h_row0 // half
    row = lambda i: (i, 0)
    return pl.pallas_call(
        _front_body,
        grid=(steps,),
        in_specs=[pl.BlockSpec((tm, d), row), pl.BlockSpec((1, d), lambda i: (0, 0)),
                  pl.BlockSpec((n, d), lambda i: (0, 0), pipeline_mode=pl.Buffered(1)),
                  pl.BlockSpec((half, d), lambda i: (first + 2 * i, 0)),
                  pl.BlockSpec((half, d), lambda i: (first + 2 * i + 1, 0))],
        out_specs=[pl.BlockSpec((tm, n), row), pl.BlockSpec((tm, d), row), pl.BlockSpec((tm, LANES), row),
                   pl.BlockSpec((2 * half, d), row)],
        out_shape=[jax.ShapeDtypeStruct((m, n), F32), jax.ShapeDtypeStruct((m, d), BF16),
                   jax.ShapeDtypeStruct((m, LANES), F32), jax.ShapeDtypeStruct((h_rows, d), BF16)],
        compiler_params=_params("parallel"),
        name="front",
    )(x, gain.reshape(1, d), w_at, w_t, w_t)


def _norm_outputs(m, n, tm, tn, index):
    specs = [pl.BlockSpec((tm, tn), index), pl.BlockSpec((tm, tn), index),
             pl.BlockSpec((tm, LANES), lambda i, *_: (i, 0))]
    shapes = [jax.ShapeDtypeStruct((m, n), F32), jax.ShapeDtypeStruct((m, n), BF16),
              jax.ShapeDtypeStruct((m, LANES), F32)]
    return specs, shapes


def _out_proj_body(a1_ref, a2_ref, w1_ref, w2_ref, r_ref, g_ref, o_ref, hb_ref, ss_ref):
    acc = jnp.dot(a1_ref[...], w1_ref[...], preferred_element_type=F32)
    acc = acc + jnp.dot(a2_ref[...], w2_ref[...], preferred_element_type=F32)
    h = r_ref[...] + acc
    o_ref[...] = h
    _emit_norm_inputs(h, g_ref, hb_ref, ss_ref, pl.program_id(1) == 0)


def out_proj_residual(a1, a2, w, res, gain, tm=1024, tn=512):
    m, k1 = a1.shape
    k2 = a2.shape[1]
    assert k1 == k2 and w.shape[0] == k1 + k2
    n = w.shape[1]
    tile = lambda i, j: (i, j)
    out_specs, out_shape = _norm_outputs(m, n, tm, tn, tile)
    return pl.pallas_call(
        _out_proj_body,
        grid=(m // tm, n // tn),
        in_specs=[pl.BlockSpec((tm, k1), lambda i, j: (i, 0)), pl.BlockSpec((tm, k2), lambda i, j: (i, 0)),
                  pl.BlockSpec((k1, tn), lambda i, j: (0, j)), pl.BlockSpec((k2, tn), lambda i, j: (1, j)),
                  pl.BlockSpec((tm, tn), tile), pl.BlockSpec((1, tn), lambda i, j: (0, j))],
        out_specs=out_specs,
        out_shape=out_shape,
        compiler_params=_params("parallel", "arbitrary"),
        name="out_proj",
    )(a1, a2, w, w, res, gain.reshape(1, n))


def _mlp_down_body(a_ref, w_ref, r_ref, g_ref, o_ref, hb_ref, ss_ref, acc_ref):
    k = pl.program_id(2)

    @pl.when(k == 0)
    def _():
        acc_ref[...] = jnp.zeros_like(acc_ref)

    acc_ref[...] += jnp.dot(a_ref[...], w_ref[...], preferred_element_type=F32)

    @pl.when(k == pl.num_programs(2) - 1)
    def _():
        h = r_ref[...] + acc_ref[...]
        o_ref[...] = h
        _emit_norm_inputs(h, g_ref, hb_ref, ss_ref, pl.program_id(1) == 0)


def mlp_down_residual(a, w, res, gain, tm=1024, tn=1024, tk=2048):
    m, k = a.shape
    n = w.shape[1]
    tile = lambda i, j, kk: (i, j)
    out_specs, out_shape = _norm_outputs(m, n, tm, tn, tile)
    return pl.pallas_call(
        _mlp_down_body,
        grid=(m // tm, n // tn, k // tk),
        in_specs=[pl.BlockSpec((tm, tk), lambda i, j, kk: (i, kk)), pl.BlockSpec((tk, tn), lambda i, j, kk: (kk, j)),
                  pl.BlockSpec((tm, tn), tile), pl.BlockSpec((1, tn), lambda i, j, kk: (0, j))],
        out_specs=out_specs,
        out_shape=out_shape,
        scratch_shapes=[pltpu.VMEM((tm, tn), F32)],
        compiler_params=_params("parallel", "arbitrary", "arbitrary"),
        name="mlp_down",
    )(a, w, res, gain.reshape(1, n))


def _gate_body(a_ref, w_ref, ss_ref, h_ref, e_ref, o_ref):
    acc = jnp.dot(a_ref[...], w_ref[...], preferred_element_type=F32) * _row_scale(ss_ref, a_ref.shape[1])
    o_ref[...] = h_ref[...] + jax.nn.sigmoid(acc) * e_ref[...]


def gate_residual(hb, ss, w, h, e, tm=1024, tn=512):
    m, k = hb.shape
    n = w.shape[1]
    return pl.pallas_call(
        _gate_body,
        grid=(m // tm, n // tn),
        in_specs=[pl.BlockSpec((tm, k), lambda i, j: (i, 0)), pl.BlockSpec((k, tn), lambda i, j: (0, j)),
                  pl.BlockSpec((tm, LANES), lambda i, j: (i, 0)),
                  pl.BlockSpec((tm, tn), lambda i, j: (i, j)), pl.BlockSpec((tm, tn), lambda i, j: (i, j))],
        out_specs=pl.BlockSpec((tm, tn), lambda i, j: (i, j)),
        out_shape=jax.ShapeDtypeStruct((m, n), F32),
        compiler_params=_params("parallel", "parallel"),
        name="ple_gate",
    )(hb, w, ss, h, e)


def _embed_body(p_ref, w_ref, g_ref, o_ref):
    y = jnp.dot(p_ref[...].astype(BF16), w_ref[...], preferred_element_type=F32)
    o_ref[...] = _rms(y, g_ref[...])


def embed_norm(p, w, g, tm=256):
    m, k = p.shape
    n = w.shape[1]
    return pl.pallas_call(
        _embed_body,
        grid=(m // tm,),
        in_specs=[pl.BlockSpec((tm, k), lambda i: (i, 0)), pl.BlockSpec((k, n), lambda i: (0, 0)),
                  pl.BlockSpec((1, n), lambda i: (0, 0))],
        out_specs=pl.BlockSpec((tm, n), lambda i: (i, 0)),
        out_shape=jax.ShapeDtypeStruct((m, n), F32),
        compiler_params=_params("parallel"),
        name="ple_embed",
    )(p, w, g.reshape(1, n))


def _mla_prep_body(pa_ref, pos_ref, gq_ref, gkv_ref, wq_ref, wk_ref, wvt_ref, freq_ref, sign_ref, q_ref, k_ref, vt_ref):
    pa = pa_ref[...]
    cq = _rms(pa[:, :Q_LORA], gq_ref[...]).astype(BF16)
    ckv = _rms(pa[:, Q_LORA:Q_LORA + KV_LORA], gkv_ref[...]).astype(BF16)
    kr = pa[:, Q_LORA + KV_LORA:Q_LORA + KV_LORA + LANES]
    krs = pa[:, Q_LORA + KV_LORA + LANES:]
    ang = pos_ref[...].astype(F32) * freq_ref[...]
    cos = jnp.cos(ang)
    sin = jnp.sin(ang) * sign_ref[...]
    q = jnp.dot(cq, wq_ref[...], preferred_element_type=F32)
    kn = jnp.dot(ckv, wk_ref[...], preferred_element_type=F32)
    vt_ref[...] = lax.dot_general(wvt_ref[...], ckv, (((1,), (1,)), ((), ())),
                                  preferred_element_type=F32).astype(BF16)
    krot = (kr * cos + krs * sin).astype(BF16)
    qscale = (QK_HEAD ** -0.5) * LOG2_E
    hn = MLA_HEADS * LANES
    for h in range(MLA_HEADS):
        lo, hi = h * LANES, (h + 1) * LANES
        qrot = q[:, hn + lo:hn + hi] * cos + q[:, 2 * hn + lo:2 * hn + hi] * sin
        q_ref[:, h * QK_PAD:h * QK_PAD + LANES] = (q[:, lo:hi] * qscale).astype(BF16)
        q_ref[:, h * QK_PAD + LANES:(h + 1) * QK_PAD] = (qrot * qscale).astype(BF16)
        k_ref[:, h * QK_PAD:h * QK_PAD + LANES] = kn[:, lo:hi].astype(BF16)
        k_ref[:, h * QK_PAD + LANES:(h + 1) * QK_PAD] = krot


def mla_prep(proj, pos, gq, gkv, wq, wk, wvt, freq, sign, tm=256):
    n = proj.shape[0]
    const = lambda i: (0, 0)
    return pl.pallas_call(
        _mla_prep_body,
        grid=(n // tm,),
        in_specs=[pl.BlockSpec((tm, A_WIDTH), lambda i: (i, 0)), pl.BlockSpec((tm, 1), lambda i: (i, 0)),
                  pl.BlockSpec((1, Q_LORA), const), pl.BlockSpec((1, KV_LORA), const),
                  pl.BlockSpec(wq.shape, const), pl.BlockSpec(wk.shape, const), pl.BlockSpec(wvt.shape, const),
                  pl.BlockSpec((1, LANES), const), pl.BlockSpec((1, LANES), const)],
        out_specs=[pl.BlockSpec((tm, MLA_HEADS * QK_PAD), lambda i: (i, 0)),
                   pl.BlockSpec((tm, MLA_HEADS * QK_PAD), lambda i: (i, 0)),
                   pl.BlockSpec((MLA_HEADS * V_HEAD, tm), lambda i: (0, i))],
        out_shape=[jax.ShapeDtypeStruct((n, MLA_HEADS * QK_PAD), BF16),
                   jax.ShapeDtypeStruct((n, MLA_HEADS * QK_PAD), BF16),
                   jax.ShapeDtypeStruct((MLA_HEADS * V_HEAD, n), BF16)],
        compiler_params=_params("parallel"),
        name="mla_prep",
    )(proj, pos, gq.reshape(1, -1), gkv.reshape(1, -1), wq, wk, wvt, freq, sign)


def _attn_body(q_ref, k_ref, vt_ref, *refs, tq, tk, n_cast):
    o_ref = refs[n_cast]
    s_ref, mx_ref, m_ref, l_ref, acc_ref = refs[2 * n_cast + 1:]
    _cast_slabs(refs[:n_cast], refs[n_cast + 1:2 * n_cast + 1])
    seq = q_ref.shape[0]
    n_diag = tq // tk
    gw = 2 * LANES
    tiles = [(qi, j) for qi in range(seq // tq) for j in range((qi + 1) * n_diag)]

    def keys_needed(qi, j, c):
        diag = j - qi * n_diag
        return tk if diag < 0 else max(0, min(tk, c + gw - diag * tk))

    def produce(slot, qi, j):
        for c in range(0, tq, gw):
            nk = keys_needed(qi, j, c)
            if nk == 0:
                continue
            q = q_ref[qi * tq + c:qi * tq + c + gw, :]
            s = lax.dot_general(k_ref[j * tk:j * tk + nk, :], q, (((1,), (1,)), ((), ())),
                                preferred_element_type=F32)
            s_ref[slot, :nk, c:c + gw] = s
            mx_ref[slot, :, c:c + gw] = jnp.max(s, axis=0, keepdims=True)

    def consume(slot, qi, j):
        for c in range(0, tq, gw):
            cols = slice(c, c + gw)
            nk = keys_needed(qi, j, c)
            if nk == 0:
                continue
            s = s_ref[slot, :nk, cols]
            first_key = (j - qi * n_diag) * tk
            if first_key + nk - 1 <= c:
                tile_max = mx_ref[slot, :, cols]
            else:
                key = first_key + lax.broadcasted_iota(jnp.int32, (nk, gw), 0)
                qry = c + lax.broadcasted_iota(jnp.int32, (nk, gw), 1)
                s = jnp.where(key <= qry, s, -jnp.inf)
                tile_max = jnp.max(s, axis=0, keepdims=True)
            vt = vt_ref[:, j * tk:j * tk + nk]
            if j == 0:
                m_new = tile_max
                p = jnp.exp2(s - m_new)
                l_ref[:, cols] = jnp.sum(p, axis=0, keepdims=True)
                acc_ref[:, cols] = jnp.dot(vt, p.astype(BF16), preferred_element_type=F32)
            else:
                m = m_ref[:, cols]
                m_new = jnp.maximum(m, tile_max)
                alpha = jnp.exp2(m - m_new)
                p = jnp.exp2(s - m_new)
                l_ref[:, cols] = alpha * l_ref[:, cols] + jnp.sum(p, axis=0, keepdims=True)
                acc_ref[:, cols] = alpha * acc_ref[:, cols] + jnp.dot(vt, p.astype(BF16), preferred_element_type=F32)
            m_ref[:, cols] = m_new

    produce(0, *tiles[0])
    for t, (qi, j) in enumerate(tiles):
        if t + 1 < len(tiles):
            produce((t + 1) % 2, *tiles[t + 1])
        consume(t % 2, qi, j)
        if j == (qi + 1) * n_diag - 1:
            o_ref[qi * tq:(qi + 1) * tq, :] = (acc_ref[...] / l_ref[...]).T.astype(o_ref.dtype)


def causal_attention(q, k, vt, batch, seq, tq=1024, tk=512, cast=()):
    cast_specs, cast_shapes = _cast_specs(cast, batch * MLA_HEADS, lambda b, h: b * MLA_HEADS + h)
    outs = pl.pallas_call(
        functools.partial(_attn_body, tq=tq, tk=tk, n_cast=len(cast)),
        grid=(batch, MLA_HEADS),
        in_specs=[pl.BlockSpec((seq, QK_PAD), lambda b, h: (b, h)), pl.BlockSpec((seq, QK_PAD), lambda b, h: (b, h)),
                  pl.BlockSpec((V_HEAD, seq), lambda b, h: (h, b))] + cast_specs,
        out_specs=[pl.BlockSpec((seq, V_HEAD), lambda b, h: (b, h))] + cast_specs,
        out_shape=[jax.ShapeDtypeStruct((batch * seq, MLA_HEADS * V_HEAD), BF16)] + cast_shapes,
        scratch_shapes=[pltpu.VMEM((2, tk, tq), F32), pltpu.VMEM((2, 1, tq), F32)] + [pltpu.VMEM((1, tq), F32)] * 2
        + [pltpu.VMEM((V_HEAD, tq), F32)],
        compiler_params=_params("parallel", "parallel"),
        name="mla_attention",
    )(q, k, vt, *cast)
    return outs[0], outs[1:]


def _hgrn_constants(c):
    t = np.arange(c)
    ltri = (t[None, :] <= t[:, None]).astype(np.float32)
    nlev = int(np.log2(c))
    x = t[:, None] ^ t[None, :]
    level = np.full((c, c), -1, np.int32)
    lower = t[:, None] > t[None, :]
    level[lower] = (nlev - 1) - np.floor(np.log2(x[lower])).astype(np.int32)
    level[t, t] = nlev
    return np.concatenate([ltri, ltri, ltri], axis=1), level, nlev


def _block_ref(b, half):
    c, dk = b.shape
    if half >= 4:
        x = b.reshape(c // (2 * half), 2 * half, dk)
        return jnp.broadcast_to(x[:, half - 1:half, :], x.shape).reshape(c, dk)
    assert half == 2
    x = b.reshape(c // 8, 8, dk)
    sub = lax.broadcasted_iota(jnp.int32, x.shape, 1)
    return jnp.where(sub < 4, x[:, 1:2, :], x[:, 5:6, :]).reshape(c, dk)


def _hgrn_body(hq_ref, hf_ref, hi_ref, hg_ref, lbp_ref, gn_ref, w3_ref, lev_ref, o_ref, st_ref, *, layer, tile, chunk, nlev):
    @pl.when(pl.program_id(2) == 0)
    def _():
        st_ref[...] = jnp.zeros_like(st_ref)

    hb = lbp_ref[...]
    ex = jnp.exp(hb - jnp.max(hb, axis=0, keepdims=True))
    sm = ex / jnp.sum(ex, axis=0, keepdims=True)
    lb = jnp.sum(sm[:layer + 1], axis=0, keepdims=True)
    oml = 1.0 - lb
    lev = lev_ref[...]
    gn = gn_ref[...]
    trans_b = (((1,), (1,)), ((), ()))
    chunks = [slice(c * chunk, (c + 1) * chunk) for c in range(tile // chunk)]
    fs, qs, kks, vs, bs = [], [], [], [], []
    for rows in chunks:
        z = hf_ref[rows, :]
        f = lb + oml * jax.nn.sigmoid(z)
        g = jnp.log(f)
        hq = hq_ref[rows, :]
        fs.append(f)
        kks.append(oml * jax.nn.sigmoid(-z))
        qs.append(hq * jax.nn.sigmoid(hq))
        vs.append(hi_ref[rows, :].astype(BF16))
        g_hi = g.astype(BF16)
        r1 = g - g_hi.astype(F32)
        g_mid = r1.astype(BF16)
        g_lo = (r1 - g_mid.astype(F32)).astype(BF16)
        g3 = jnp.concatenate([g_hi, g_mid, g_lo], axis=0)
        bs.append(jnp.dot(w3_ref[...], g3, preferred_element_type=F32))
    scores = [jnp.zeros((chunk, chunk), F32) for _ in chunks]
    q16 = [q.astype(BF16) for q in qs]
    k16 = [kk.astype(BF16) for kk in kks]
    for lv in range(nlev + 1):
        half = chunk >> (lv + 1)
        for c, (f, q, kk, b) in enumerate(zip(fs, q16, k16, bs)):
            if half >= 2:
                e = jnp.exp(-jnp.abs(b - _block_ref(b, half))).astype(BF16)
                ql, kl = q * e, kk * e
            elif half == 1:
                ql, kl = q * f.astype(BF16), kk
            else:
                ql, kl = q, kk
            p = lax.dot_general(ql, kl, trans_b, preferred_element_type=F32)
            scores[c] = jnp.where(lev == lv, p, scores[c])
    intra, updates, decays, qbs = [], [], [], []
    for a, q, kk, v, b in zip(scores, qs, kks, vs, bs):
        intra.append(jnp.dot(a.astype(BF16), v, preferred_element_type=F32))
        b_last = b[chunk - 1:chunk, :]
        kh = (kk * jnp.exp(b_last - b)).astype(BF16)
        updates.append(lax.dot_general(v, kh, (((0,), (0,)), ((), ())), preferred_element_type=F32))
        decays.append(jnp.exp(b_last))
        qbs.append((q * jnp.exp(b)).astype(BF16))
    state = st_ref[...]
    for rows, qb, o_intra, upd, dec in zip(chunks, qbs, intra, updates, decays):
        o = lax.dot_general(qb, state.astype(BF16), trans_b, preferred_element_type=F32) + o_intra
        state = dec * state + upd
        hg = hg_ref[rows, :]
        y = _rms(o, gn) * (hg * jax.nn.sigmoid(hg))
        o_ref[rows, :] = y.astype(o_ref.dtype)
    st_ref[...] = state


def hgrn2(proj, col0, lbp, gn, batch, seq, layer, tile=1024):
    w3, level, nlev = _hgrn_constants(HG_CHUNK)
    w3 = jnp.asarray(w3, BF16)
    level = jnp.asarray(level)
    nt = seq // tile
    cb = col0 // LANES

    def col(group):
        return lambda b, h, i: (b * nt + i, cb + group * HG_HEADS + h)

    return pl.pallas_call(
        functools.partial(_hgrn_body, layer=layer, tile=tile, chunk=HG_CHUNK, nlev=nlev),
        grid=(batch, HG_HEADS, nt),
        in_specs=[pl.BlockSpec((tile, LANES), col(0)), pl.BlockSpec((tile, LANES), col(1)),
                  pl.BlockSpec((tile, LANES), col(2)), pl.BlockSpec((tile, LANES), col(3)),
                  pl.BlockSpec((lbp.shape[0], LANES), lambda b, h, i: (0, h)),
                  pl.BlockSpec((1, LANES), lambda b, h, i: (0, h)),
                  pl.BlockSpec(w3.shape, lambda b, h, i: (0, 0)),
                  pl.BlockSpec(level.shape, lambda b, h, i: (0, 0))],
        out_specs=pl.BlockSpec((tile, LANES), lambda b, h, i: (b * nt + i, h)),
        out_shape=jax.ShapeDtypeStruct((batch * seq, HG_VDIM), BF16),
        scratch_shapes=[pltpu.VMEM((HG_DV, HG_DK), F32)],
        compiler_params=_params("parallel", "parallel", "arbitrary"),
        name="hgrn2",
    )(proj, proj, proj, proj, lbp, gn.reshape(1, -1), w3, level)


def _swap_halves(w):
    half = w.shape[-1] // 2
    return jnp.concatenate([w[..., half:], w[..., :half]], axis=-1)


def _mla_in_weight_body(wt_ref, o_ref):
    n_a = Q_LORA + KV_LORA
    half = QK_ROPE // 2
    top = wt_ref[...].astype(BF16)
    o_ref[...] = jnp.zeros_like(o_ref)
    o_ref[:n_a + QK_ROPE, :] = top
    o_ref[n_a + LANES:n_a + LANES + half, :] = top[n_a + half:]
    o_ref[n_a + LANES + half:n_a + LANES + QK_ROPE, :] = top[n_a:n_a + half]


def mla_in_weight(w_in_t, tn=512):
    d = w_in_t.shape[1]
    rows = Q_LORA + KV_LORA + QK_ROPE
    return pl.pallas_call(
        _mla_in_weight_body,
        grid=(d // tn,),
        in_specs=[pl.BlockSpec((rows, tn), lambda j: (0, j))],
        out_specs=pl.BlockSpec((A_WIDTH, tn), lambda j: (0, j)),
        out_shape=jax.ShapeDtypeStruct((A_WIDTH, d), BF16),
        compiler_params=_params("parallel"),
        name="mla_in_weight",
    )(w_in_t)


def _layer_weights(w_uq, w_ukv):
    wq = w_uq.reshape(Q_LORA, MLA_HEADS, QK_HEAD)
    rope = wq[:, :, QK_NOPE:]
    pad = jnp.zeros((Q_LORA, MLA_HEADS, LANES - QK_ROPE), w_uq.dtype)
    wq_all = jnp.concatenate([
        wq[:, :, :QK_NOPE].reshape(Q_LORA, -1),
        jnp.concatenate([rope, pad], axis=-1).reshape(Q_LORA, -1),
        jnp.concatenate([_swap_halves(rope), pad], axis=-1).reshape(Q_LORA, -1)], axis=1)
    wkv = w_ukv.reshape(KV_LORA, MLA_HEADS, QK_NOPE + V_HEAD)
    wk = wkv[:, :, :QK_NOPE].reshape(KV_LORA, -1)
    wvt = wkv[:, :, QK_NOPE:].reshape(KV_LORA, -1).T
    return wq_all.astype(BF16), wk.astype(BF16), wvt.astype(BF16)


def _rope_rows():
    inv_freq = ROPE_THETA ** (-jnp.arange(0, QK_ROPE, 2, dtype=F32) / QK_ROPE)
    zero = jnp.zeros((LANES - QK_ROPE,), F32)
    freq = jnp.concatenate([inv_freq, inv_freq, zero]).reshape(1, LANES)
    half = jnp.ones((QK_ROPE // 2,), F32)
    sign = jnp.concatenate([-half, half, zero]).reshape(1, LANES)
    return freq, sign


def kernel(x, p, positions, norm_mix, w_in, q_a_norm, kv_a_norm, w_uq, w_ukv, hg_lower_bound, hg_out_norm, w_o,
           norm_mlp, w_up, w_down, norm_ple, w_ple_gate, w_ple, ple_post_norm, final_norm):
    batch, seq, d_model = x.shape
    n = batch * seq
    depth = w_in.shape[0]
    h = x.reshape(n, d_model)
    pos = positions.reshape(n, 1)
    freq, sign = _rope_rows()
    for i in range(depth):
        w_in_t = w_in[i].T
        wq_all, wk, wvt = _layer_weights(w_uq[i], w_ukv[i])
        proj_a, xb, ss, w_ht = front(h, norm_mix[i], mla_in_weight(w_in_t), w_in_t, Q_LORA + KV_LORA + QK_ROPE)
        proj_h, (w_up_b,) = scaled_matmul(xb, ss, w_ht, F32, "in_proj_h", w_transposed=True, tn=1024,
                                          cast=(w_up[i],))
        q, k, vt = mla_prep(proj_a, pos, q_a_norm[i], kv_a_norm[i], wq_all, wk, wvt, freq, sign)
        o_mla, (w_o_b, w_pg_b) = causal_attention(q, k, vt, batch, seq, cast=(w_o[i], w_ple_gate[i]))
        o_hg = hgrn2(proj_h, 0, hg_lower_bound, hg_out_norm[i], batch, seq, i)
        h, hb, ss = out_proj_residual(o_mla, o_hg, w_o_b, h, norm_mlp[i])
        hidden, (w_down_b,) = scaled_matmul(hb, ss, w_up_b, BF16, "mlp_up", relu2=True, cast=(w_down[i],))
        h, hb, ss = mlp_down_residual(hidden, w_down_b, h, norm_ple[i])
        e = embed_norm(p[i].reshape(n, -1), w_ple[i].astype(BF16), ple_post_norm[i])
        h = gate_residual(hb, ss, w_pg_b, h, e)
    return rmsnorm(h, final_norm, x.dtype).reshape(batch, seq, d_model)
```

```python
import functools

import numpy as np
import jax
import jax.numpy as jnp
from jax import lax
from jax.experimental import pallas as pl
from jax.experimental.pallas import tpu as pltpu

EPS = 1e-6
MLA_HEADS = 16
QK_NOPE = 128
QK_ROPE = 64
QK_HEAD = QK_NOPE + QK_ROPE
V_HEAD = 128
Q_LORA = 768
KV_LORA = 512
ROPE_THETA = 10000.0
HG_HEADS = 16
HG_DK = 128
HG_DV = 128
HG_FDIM = HG_HEADS * HG_DK
HG_VDIM = HG_HEADS * HG_DV

LANES = 128
MXU_WIDTH = 256
QK_PAD = 2 * LANES
A_WIDTH = Q_LORA + KV_LORA + 2 * LANES
HG_CHUNK = 128
VMEM_LIMIT_BYTES = 56 * 1024 * 1024
LOG2_E = 1.4426950408889634

F32 = jnp.float32
BF16 = jnp.bfloat16


def _params(*semantics):
    return pltpu.CompilerParams(dimension_semantics=semantics, vmem_limit_bytes=VMEM_LIMIT_BYTES)


def _rms(x, g):
    return x * lax.rsqrt(jnp.mean(x * x, axis=-1, keepdims=True) + EPS) * g


def _emit_norm_inputs(h, g_ref, hb_ref, ss_ref, first):
    hb_ref[...] = (h * g_ref[...]).astype(BF16)

    @pl.when(first)
    def _():
        ss_ref[...] = jnp.zeros_like(ss_ref)

    ss_ref[...] += jnp.sum(h * h, axis=-1, keepdims=True)


def _row_scale(ss_ref, d):
    return lax.rsqrt(ss_ref[:, :1] / d + EPS)


def _cast_specs(ws, n_steps, step_of):
    specs, shapes = [], []
    for w in ws:
        rows = w.shape[0] // n_steps
        assert rows * n_steps == w.shape[0] and rows % 16 == 0
        specs.append(pl.BlockSpec((rows, w.shape[1]), lambda *ids: (step_of(*ids), 0)))
        shapes.append(jax.ShapeDtypeStruct(w.shape, BF16))
    return specs, shapes


def _cast_slabs(src_refs, dst_refs):
    for src, dst in zip(src_refs, dst_refs):
        dst[...] = src[...].astype(dst.dtype)


TRANS_B = (((1,), (1,)), ((), ()))


def _scaled_mm_body(a_ref, w_ref, ss_ref, *refs, relu2, w_transposed):
    n_cast = len(refs) // 2
    o_ref = refs[n_cast]
    if w_transposed:
        acc = lax.dot_general(a_ref[...], w_ref[...], TRANS_B, preferred_element_type=F32)
    else:
        acc = jnp.dot(a_ref[...], w_ref[...], preferred_element_type=F32)
    acc = acc * _row_scale(ss_ref, a_ref.shape[1])
    if relu2:
        acc = jnp.square(jnp.maximum(acc, 0.0))
    o_ref[...] = acc.astype(o_ref.dtype)
    _cast_slabs(refs[:n_cast], refs[n_cast + 1:])


def scaled_matmul(hb, ss, w, out_dtype, name, relu2=False, w_transposed=False, tm=1024, tn=512, cast=()):
    m, k = hb.shape
    n = w.shape[0] if w_transposed else w.shape[1]
    nj = n // tn
    cast_specs, cast_shapes = _cast_specs(cast, (m // tm) * nj, lambda i, j: i * nj + j)
    w_spec = pl.BlockSpec((tn, k), lambda i, j: (j, 0)) if w_transposed else pl.BlockSpec((k, tn), lambda i, j: (0, j))
    outs = pl.pallas_call(
        functools.partial(_scaled_mm_body, relu2=relu2, w_transposed=w_transposed),
        grid=(m // tm, nj),
        in_specs=[pl.BlockSpec((tm, k), lambda i, j: (i, 0)), w_spec,
                  pl.BlockSpec((tm, LANES), lambda i, j: (i, 0))] + cast_specs,
        out_specs=[pl.BlockSpec((tm, tn), lambda i, j: (i, j))] + cast_specs,
        out_shape=[jax.ShapeDtypeStruct((m, n), out_dtype)] + cast_shapes,
        compiler_params=_params("parallel", "parallel"),
        name=name,
    )(hb, w, ss, *cast)
    return outs[0], outs[1:]


def _front_body(x_ref, g_ref, wat_ref, wt_lo_ref, wt_hi_ref, pa_ref, xb_ref, ss_ref, wht_ref):
    x = x_ref[...]
    xb = (x * g_ref[...]).astype(BF16)
    xb_ref[...] = xb
    ss = jnp.sum(x * x, axis=-1, keepdims=True)
    ss_ref[...] = jnp.broadcast_to(ss, ss_ref.shape)
    acc = lax.dot_general(xb, wat_ref[...], TRANS_B, preferred_element_type=F32)
    pa_ref[...] = acc * lax.rsqrt(ss / x.shape[1] + EPS)
    half = wt_lo_ref.shape[0]
    wht_ref[:half, :] = wt_lo_ref[...].astype(BF16)
    wht_ref[half:, :] = wt_hi_ref[...].astype(BF16)


def front(x, gain, w_at, w_t, h_row0, tm=256):
    m, d = x.shape
    n = w_at.shape[0]
    steps = m // tm
    h_rows = w_t.shape[0] - h_row0
    half = h_rows // (2 * steps)
    assert 2 * half * steps == h_rows and half % 16 == 0 and h_row0 % half == 0
    first = h_row0 // half
    row = lambda i: (i, 0)
    return pl.pallas_call(
        _front_body,
        grid=(steps,),
        in_specs=[pl.BlockSpec((tm, d), row), pl.BlockSpec((1, d), lambda i: (0, 0)),
                  pl.BlockSpec((n, d), lambda i: (0, 0), pipeline_mode=pl.Buffered(1)),
                  pl.BlockSpec((half, d), lambda i: (first + 2 * i, 0)),
                  pl.BlockSpec((half, d), lambda i: (first + 2 * i + 1, 0))],
        out_specs=[pl.BlockSpec((tm, n), row), pl.BlockSpec((tm, d), row), pl.BlockSpec((tm, LANES), row),
                   pl.BlockSpec((2 * half, d), row)],
        out_shape=[jax.ShapeDtypeStruct((m, n), F32), jax.ShapeDtypeStruct((m, d), BF16),
                   jax.ShapeDtypeStruct((m, LANES), F32), jax.ShapeDtypeStruct((h_rows, d), BF16)],
        compiler_params=_params("parallel"),
        name="front",
    )(x, gain.reshape(1, d), w_at, w_t, w_t)


def _norm_outputs(m, n, tm, tn, index):
    specs = [pl.BlockSpec((tm, tn), index), pl.BlockSpec((tm, tn), index),
             pl.BlockSpec((tm, LANES), lambda i, *_: (i, 0))]
    shapes = [jax.ShapeDtypeStruct((m, n), F32), jax.ShapeDtypeStruct((m, n), BF16),
              jax.ShapeDtypeStruct((m, LANES), F32)]
    return specs, shapes


def _out_proj_body(a1_ref, a2_ref, w1_ref, w2_ref, r_ref, g_ref, o_ref, hb_ref, ss_ref):
    acc = jnp.dot(a1_ref[...], w1_ref[...], preferred_element_type=F32)
    acc = acc + jnp.dot(a2_ref[...], w2_ref[...], preferred_element_type=F32)
    h = r_ref[...] + acc
    o_ref[...] = h
    _emit_norm_inputs(h, g_ref, hb_ref, ss_ref, pl.program_id(1) == 0)


def out_proj_residual(a1, a2, w, res, gain, tm=1024, tn=512):
    m, k1 = a1.shape
    k2 = a2.shape[1]
    assert k1 == k2 and w.shape[0] == k1 + k2
    n = w.shape[1]
    tile = lambda i, j: (i, j)
    out_specs, out_shape = _norm_outputs(m, n, tm, tn, tile)
    return pl.pallas_call(
        _out_proj_body,
        grid=(m // tm, n // tn),
        in_specs=[pl.BlockSpec((tm, k1), lambda i, j: (i, 0)), pl.BlockSpec((tm, k2), lambda i, j: (i, 0)),
                  pl.BlockSpec((k1, tn), lambda i, j: (0, j)), pl.BlockSpec((k2, tn), lambda i, j: (1, j)),
                  pl.BlockSpec((tm, tn), tile), pl.BlockSpec((1, tn), lambda i, j: (0, j))],
        out_specs=out_specs,
        out_shape=out_shape,
        compiler_params=_params("parallel", "arbitrary"),
        name="out_proj",
    )(a1, a2, w, w, res, gain.reshape(1, n))


def _mlp_down_body(a_ref, w_ref, r_ref, g_ref, o_ref, hb_ref, ss_ref, acc_ref):
    k = pl.program_id(2)

    @pl.when(k == 0)
    def _():
        acc_ref[...] = jnp.zeros_like(acc_ref)

    acc_ref[...] += jnp.dot(a_ref[...], w_ref[...], preferred_element_type=F32)

    @pl.when(k == pl.num_programs(2) - 1)
    def _():
        h = r_ref[...] + acc_ref[...]
        o_ref[...] = h
        _emit_norm_inputs(h, g_ref, hb_ref, ss_ref, pl.program_id(1) == 0)


def mlp_down_residual(a, w, res, gain, tm=1024, tn=1024, tk=2048):
    m, k = a.shape
    n = w.shape[1]
    tile = lambda i, j, kk: (i, j)
    out_specs, out_shape = _norm_outputs(m, n, tm, tn, tile)
    return pl.pallas_call(
        _mlp_down_body,
        grid=(m // tm, n // tn, k // tk),
        in_specs=[pl.BlockSpec((tm, tk), lambda i, j, kk: (i, kk)), pl.BlockSpec((tk, tn), lambda i, j, kk: (kk, j)),
                  pl.BlockSpec((tm, tn), tile), pl.BlockSpec((1, tn), lambda i, j, kk: (0, j))],
        out_specs=out_specs,
        out_shape=out_shape,
        scratch_shapes=[pltpu.VMEM((tm, tn), F32)],
        compiler_params=_params("parallel", "arbitrary", "arbitrary"),
        name="mlp_down",
    )(a, w, res, gain.reshape(1, n))


def _gate_body(a_ref, w_ref, ss_ref, h_ref, e_ref, o_ref):
    acc = jnp.dot(a_ref[...], w_ref[...], preferred_element_type=F32) * _row_scale(ss_ref, a_ref.shape[1])
    o_ref[...] = h_ref[...] + jax.nn.sigmoid(acc) * e_ref[...]


def gate_residual(hb, ss, w, h, e, tm=1024, tn=512):
    m, k = hb.shape
    n = w.shape[1]
    return pl.pallas_call(
        _gate_body,
        grid=(m // tm, n // tn),
        in_specs=[pl.BlockSpec((tm, k), lambda i, j: (i, 0)), pl.BlockSpec((k, tn), lambda i, j: (0, j)),
                  pl.BlockSpec((tm, LANES), lambda i, j: (i, 0)),
                  pl.BlockSpec((tm, tn), lambda i, j: (i, j)), pl.BlockSpec((tm, tn), lambda i, j: (i, j))],
        out_specs=pl.BlockSpec((tm, tn), lambda i, j: (i, j)),
        out_shape=jax.ShapeDtypeStruct((m, n), F32),
        compiler_params=_params("parallel", "parallel"),
        name="ple_gate",
    )(hb, w, ss, h, e)


def _gate_final_body(a_ref, w_ref, ss_ref, h_ref, e_ref, gf_ref, o_ref, s3_ref):
    j = pl.program_id(1)
    tn = w_ref.shape[1]
    r = _row_scale(ss_ref, a_ref.shape[1])

    @pl.when(j == 0)
    def _():
        s3_ref[...] = jnp.zeros_like(s3_ref)

    a = a_ref[...]
    part = jnp.zeros((a.shape[0], 1), F32)
    for c in range(0, tn, MXU_WIDTH):
        cols = slice(c, c + MXU_WIDTH)
        acc = jnp.dot(a, w_ref[:, cols], preferred_element_type=F32) * r
        h3 = h_ref[:, cols] + jax.nn.sigmoid(acc) * e_ref[:, cols]
        o_ref[:, pl.ds(pl.multiple_of(j * tn + c, MXU_WIDTH), MXU_WIDTH)] = h3
        part = part + jnp.sum(h3 * h3, axis=-1, keepdims=True)
    s3_ref[...] += part

    @pl.when(j == pl.num_programs(1) - 1)
    def _():
        o_ref[...] = o_ref[...] * _row_scale(s3_ref, o_ref.shape[1]) * gf_ref[...]


def gate_residual_final_norm(hb, ss, w, h, e, final_gain, tm=512, tn=512):
    m, k = hb.shape
    n = w.shape[1]
    return pl.pallas_call(
        _gate_final_body,
        grid=(m // tm, n // tn),
        in_specs=[pl.BlockSpec((tm, k), lambda i, j: (i, 0)), pl.BlockSpec((k, tn), lambda i, j: (0, j)),
                  pl.BlockSpec((tm, LANES), lambda i, j: (i, 0)),
                  pl.BlockSpec((tm, tn), lambda i, j: (i, j)), pl.BlockSpec((tm, tn), lambda i, j: (i, j)),
                  pl.BlockSpec((1, n), lambda i, j: (0, 0))],
        out_specs=pl.BlockSpec((tm, n), lambda i, j: (i, 0)),
        out_shape=jax.ShapeDtypeStruct((m, n), F32),
        scratch_shapes=[pltpu.VMEM((tm, LANES), F32)],
        compiler_params=_params("parallel", "arbitrary"),
        name="ple_gate_final",
    )(hb, w, ss, h, e, final_gain.reshape(1, n))


def _embed_body(p_ref, w_ref, g_ref, o_ref):
    y = jnp.dot(p_ref[...].astype(BF16), w_ref[...], preferred_element_type=F32)
    o_ref[...] = _rms(y, g_ref[...])


def embed_norm(p, w, g, tm=256):
    m, k = p.shape
    n = w.shape[1]
    return pl.pallas_call(
        _embed_body,
        grid=(m // tm,),
        in_specs=[pl.BlockSpec((tm, k), lambda i: (i, 0)), pl.BlockSpec((k, n), lambda i: (0, 0)),
                  pl.BlockSpec((1, n), lambda i: (0, 0))],
        out_specs=pl.BlockSpec((tm, n), lambda i: (i, 0)),
        out_shape=jax.ShapeDtypeStruct((m, n), F32),
        compiler_params=_params("parallel"),
        name="ple_embed",
    )(p, w, g.reshape(1, n))


def _mla_prep_body(pa_ref, pos_ref, gq_ref, gkv_ref, wq_ref, wk_ref, wvt_ref, freq_ref, sign_ref, q_ref, k_ref, vt_ref):
    pa = pa_ref[...]
    cq = _rms(pa[:, :Q_LORA], gq_ref[...]).astype(BF16)
    ckv = _rms(pa[:, Q_LORA:Q_LORA + KV_LORA], gkv_ref[...]).astype(BF16)
    kr = pa[:, Q_LORA + KV_LORA:Q_LORA + KV_LORA + LANES]
    krs = pa[:, Q_LORA + KV_LORA + LANES:]
    ang = pos_ref[...].astype(F32) * freq_ref[...]
    cos = jnp.cos(ang)
    sin = jnp.sin(ang) * sign_ref[...]
    q = jnp.dot(cq, wq_ref[...], preferred_element_type=F32)
    kn = jnp.dot(ckv, wk_ref[...], preferred_element_type=F32)
    vt_ref[...] = lax.dot_general(wvt_ref[...], ckv, (((1,), (1,)), ((), ())),
                                  preferred_element_type=F32).astype(BF16)
    krot = (kr * cos + krs * sin).astype(BF16)
    qscale = (QK_HEAD ** -0.5) * LOG2_E
    hn = MLA_HEADS * LANES
    for h in range(MLA_HEADS):
        lo, hi = h * LANES, (h + 1) * LANES
        qrot = q[:, hn + lo:hn + hi] * cos + q[:, 2 * hn + lo:2 * hn + hi] * sin
        q_ref[:, h * QK_PAD:h * QK_PAD + LANES] = (q[:, lo:hi] * qscale).astype(BF16)
        q_ref[:, h * QK_PAD + LANES:(h + 1) * QK_PAD] = (qrot * qscale).astype(BF16)
        k_ref[:, h * QK_PAD:h * QK_PAD + LANES] = kn[:, lo:hi].astype(BF16)
        k_ref[:, h * QK_PAD + LANES:(h + 1) * QK_PAD] = krot


def mla_prep(proj, pos, gq, gkv, wq, wk, wvt, freq, sign, tm=256):
    n = proj.shape[0]
    const = lambda i: (0, 0)
    return pl.pallas_call(
        _mla_prep_body,
        grid=(n // tm,),
        in_specs=[pl.BlockSpec((tm, A_WIDTH), lambda i: (i, 0)), pl.BlockSpec((tm, 1), lambda i: (i, 0)),
                  pl.BlockSpec((1, Q_LORA), const), pl.BlockSpec((1, KV_LORA), const),
                  pl.BlockSpec(wq.shape, const), pl.BlockSpec(wk.shape, const), pl.BlockSpec(wvt.shape, const),
                  pl.BlockSpec((1, LANES), const), pl.BlockSpec((1, LANES), const)],
        out_specs=[pl.BlockSpec((tm, MLA_HEADS * QK_PAD), lambda i: (i, 0)),
                   pl.BlockSpec((tm, MLA_HEADS * QK_PAD), lambda i: (i, 0)),
                   pl.BlockSpec((MLA_HEADS * V_HEAD, tm), lambda i: (0, i))],
        out_shape=[jax.ShapeDtypeStruct((n, MLA_HEADS * QK_PAD), BF16),
                   jax.ShapeDtypeStruct((n, MLA_HEADS * QK_PAD), BF16),
                   jax.ShapeDtypeStruct((MLA_HEADS * V_HEAD, n), BF16)],
        compiler_params=_params("parallel"),
        name="mla_prep",
    )(proj, pos, gq.reshape(1, -1), gkv.reshape(1, -1), wq, wk, wvt, freq, sign)


def _attn_body(q_ref, k_ref, vt_ref, *refs, tq, tk, n_cast):
    o_ref = refs[n_cast]
    s_ref, mx_ref, m_ref, l_ref, acc_ref = refs[2 * n_cast + 1:]
    _cast_slabs(refs[:n_cast], refs[n_cast + 1:2 * n_cast + 1])
    seq = q_ref.shape[0]
    n_diag = tq // tk
    gw = 2 * LANES
    tiles = [(qi, j) for qi in range(seq // tq) for j in range((qi + 1) * n_diag)]

    def keys_needed(qi, j, c):
        diag = j - qi * n_diag
        return tk if diag < 0 else max(0, min(tk, c + gw - diag * tk))

    def produce(slot, qi, j):
        for c in range(0, tq, gw):
            nk = keys_needed(qi, j, c)
            if nk == 0:
                continue
            q = q_ref[qi * tq + c:qi * tq + c + gw, :]
            s = lax.dot_general(k_ref[j * tk:j * tk + nk, :], q, (((1,), (1,)), ((), ())),
                                preferred_element_type=F32)
            s_ref[slot, :nk, c:c + gw] = s
            mx_ref[slot, :, c:c + gw] = jnp.max(s, axis=0, keepdims=True)

    def consume(slot, qi, j):
        for c in range(0, tq, gw):
            cols = slice(c, c + gw)
            nk = keys_needed(qi, j, c)
            if nk == 0:
                continue
            s = s_ref[slot, :nk, cols]
            first_key = (j - qi * n_diag) * tk
            if first_key + nk - 1 <= c:
                tile_max = mx_ref[slot, :, cols]
            else:
                key = first_key + lax.broadcasted_iota(jnp.int32, (nk, gw), 0)
                qry = c + lax.broadcasted_iota(jnp.int32, (nk, gw), 1)
                s = jnp.where(key <= qry, s, -jnp.inf)
                tile_max = jnp.max(s, axis=0, keepdims=True)
            vt = vt_ref[:, j * tk:j * tk + nk]
            if j == 0:
                m_new = tile_max
                p = jnp.exp2(s - m_new)
                l_ref[:, cols] = jnp.sum(p, axis=0, keepdims=True)
                acc_ref[:, cols] = jnp.dot(vt, p.astype(BF16), preferred_element_type=F32)
            else:
                m = m_ref[:, cols]
                m_new = jnp.maximum(m, tile_max)
                alpha = jnp.exp2(m - m_new)
                p = jnp.exp2(s - m_new)
                l_ref[:, cols] = alpha * l_ref[:, cols] + jnp.sum(p, axis=0, keepdims=True)
                acc_ref[:, cols] = alpha * acc_ref[:, cols] + jnp.dot(vt, p.astype(BF16), preferred_element_type=F32)
            m_ref[:, cols] = m_new

    produce(0, *tiles[0])
    for t, (qi, j) in enumerate(tiles):
        if t + 1 < len(tiles):
            produce((t + 1) % 2, *tiles[t + 1])
        consume(t % 2, qi, j)
        if j == (qi + 1) * n_diag - 1:
            o_ref[qi * tq:(qi + 1) * tq, :] = (acc_ref[...] / l_ref[...]).T.astype(o_ref.dtype)


def causal_attention(q, k, vt, batch, seq, tq=1024, tk=512, cast=()):
    cast_specs, cast_shapes = _cast_specs(cast, batch * MLA_HEADS, lambda b, h: b * MLA_HEADS + h)
    outs = pl.pallas_call(
        functools.partial(_attn_body, tq=tq, tk=tk, n_cast=len(cast)),
        grid=(batch, MLA_HEADS),
        in_specs=[pl.BlockSpec((seq, QK_PAD), lambda b, h: (b, h)), pl.BlockSpec((seq, QK_PAD), lambda b, h: (b, h)),
                  pl.BlockSpec((V_HEAD, seq), lambda b, h: (h, b))] + cast_specs,
        out_specs=[pl.BlockSpec((seq, V_HEAD), lambda b, h: (b, h))] + cast_specs,
        out_shape=[jax.ShapeDtypeStruct((batch * seq, MLA_HEADS * V_HEAD), BF16)] + cast_shapes,
        scratch_shapes=[pltpu.VMEM((2, tk, tq), F32), pltpu.VMEM((2, 1, tq), F32)] + [pltpu.VMEM((1, tq), F32)] * 2
        + [pltpu.VMEM((V_HEAD, tq), F32)],
        compiler_params=_params("parallel", "parallel"),
        name="mla_attention",
    )(q, k, vt, *cast)
    return outs[0], outs[1:]


def _hgrn_constants(c):
    t = np.arange(c)
    ltri = (t[None, :] <= t[:, None]).astype(np.float32)
    nlev = int(np.log2(c))
    x = t[:, None] ^ t[None, :]
    level = np.full((c, c), -1, np.int32)
    lower = t[:, None] > t[None, :]
    level[lower] = (nlev - 1) - np.floor(np.log2(x[lower])).astype(np.int32)
    level[t, t] = nlev
    return np.concatenate([ltri, ltri, ltri], axis=1), level, nlev


def _block_ref(b, half):
    c, dk = b.shape
    if half >= 4:
        x = b.reshape(c // (2 * half), 2 * half, dk)
        return jnp.broadcast_to(x[:, half - 1:half, :], x.shape).reshape(c, dk)
    assert half == 2
    x = b.reshape(c // 8, 8, dk)
    sub = lax.broadcasted_iota(jnp.int32, x.shape, 1)
    return jnp.where(sub < 4, x[:, 1:2, :], x[:, 5:6, :]).reshape(c, dk)


def _hgrn_body(hq_ref, hf_ref, hi_ref, hg_ref, lbp_ref, gn_ref, w3_ref, lev_ref, o_ref, st_ref, *, layer, tile, chunk, nlev):
    @pl.when(pl.program_id(2) == 0)
    def _():
        st_ref[...] = jnp.zeros_like(st_ref)

    hb = lbp_ref[...]
    ex = jnp.exp(hb - jnp.max(hb, axis=0, keepdims=True))
    sm = ex / jnp.sum(ex, axis=0, keepdims=True)
    lb = jnp.sum(sm[:layer + 1], axis=0, keepdims=True)
    oml = 1.0 - lb
    lev = lev_ref[...]
    gn = gn_ref[...]
    trans_b = (((1,), (1,)), ((), ()))
    chunks = [slice(c * chunk, (c + 1) * chunk) for c in range(tile // chunk)]
    fs, qs, kks, vs, bs = [], [], [], [], []
    for rows in chunks:
        z = hf_ref[rows, :]
        f = lb + oml * jax.nn.sigmoid(z)
        g = jnp.log(f)
        hq = hq_ref[rows, :]
        fs.append(f)
        kks.append(oml * jax.nn.sigmoid(-z))
        qs.append(hq * jax.nn.sigmoid(hq))
        vs.append(hi_ref[rows, :].astype(BF16))
        g_hi = g.astype(BF16)
        r1 = g - g_hi.astype(F32)
        g_mid = r1.astype(BF16)
        g_lo = (r1 - g_mid.astype(F32)).astype(BF16)
        g3 = jnp.concatenate([g_hi, g_mid, g_lo], axis=0)
        bs.append(jnp.dot(w3_ref[...], g3, preferred_element_type=F32))
    scores = [jnp.zeros((chunk, chunk), F32) for _ in chunks]
    q16 = [q.astype(BF16) for q in qs]
    k16 = [kk.astype(BF16) for kk in kks]
    for lv in range(nlev + 1):
        half = chunk >> (lv + 1)
        for c, (f, q, kk, b) in enumerate(zip(fs, q16, k16, bs)):
            if half >= 2:
                e = jnp.exp(-jnp.abs(b - _block_ref(b, half))).astype(BF16)
                ql, kl = q * e, kk * e
            elif half == 1:
                ql, kl = q * f.astype(BF16), kk
            else:
                ql, kl = q, kk
            p = lax.dot_general(ql, kl, trans_b, preferred_element_type=F32)
            scores[c] = jnp.where(lev == lv, p, scores[c])
    intra, updates, decays, qbs = [], [], [], []
    for a, q, kk, v, b in zip(scores, qs, kks, vs, bs):
        intra.append(jnp.dot(a.astype(BF16), v, preferred_element_type=F32))
        b_last = b[chunk - 1:chunk, :]
        kh = (kk * jnp.exp(b_last - b)).astype(BF16)
        updates.append(lax.dot_general(v, kh, (((0,), (0,)), ((), ())), preferred_element_type=F32))
        decays.append(jnp.exp(b_last))
        qbs.append((q * jnp.exp(b)).astype(BF16))
    state = st_ref[...]
    for rows, qb, o_intra, upd, dec in zip(chunks, qbs, intra, updates, decays):
        o = lax.dot_general(qb, state.astype(BF16), trans_b, preferred_element_type=F32) + o_intra
        state = dec * state + upd
        hg = hg_ref[rows, :]
        y = _rms(o, gn) * (hg * jax.nn.sigmoid(hg))
        o_ref[rows, :] = y.astype(o_ref.dtype)
    st_ref[...] = state


def hgrn2(proj, col0, lbp, gn, batch, seq, layer, tile=1024):
    w3, level, nlev = _hgrn_constants(HG_CHUNK)
    w3 = jnp.asarray(w3, BF16)
    level = jnp.asarray(level)
    nt = seq // tile
    cb = col0 // LANES

    def col(group):
        return lambda b, h, i: (b * nt + i, cb + group * HG_HEADS + h)

    return pl.pallas_call(
        functools.partial(_hgrn_body, layer=layer, tile=tile, chunk=HG_CHUNK, nlev=nlev),
        grid=(batch, HG_HEADS, nt),
        in_specs=[pl.BlockSpec((tile, LANES), col(0)), pl.BlockSpec((tile, LANES), col(1)),
                  pl.BlockSpec((tile, LANES), col(2)), pl.BlockSpec((tile, LANES), col(3)),
                  pl.BlockSpec((lbp.shape[0], LANES), lambda b, h, i: (0, h)),
                  pl.BlockSpec((1, LANES), lambda b, h, i: (0, h)),
                  pl.BlockSpec(w3.shape, lambda b, h, i: (0, 0)),
                  pl.BlockSpec(level.shape, lambda b, h, i: (0, 0))],
        out_specs=pl.BlockSpec((tile, LANES), lambda b, h, i: (b * nt + i, h)),
        out_shape=jax.ShapeDtypeStruct((batch * seq, HG_VDIM), BF16),
        scratch_shapes=[pltpu.VMEM((HG_DV, HG_DK), F32)],
        compiler_params=_params("parallel", "parallel", "arbitrary"),
        name="hgrn2",
    )(proj, proj, proj, proj, lbp, gn.reshape(1, -1), w3, level)


def _swap_halves(w):
    half = w.shape[-1] // 2
    return jnp.concatenate([w[..., half:], w[..., :half]], axis=-1)


def _mla_in_weight_body(wt_ref, o_ref):
    n_a = Q_LORA + KV_LORA
    half = QK_ROPE // 2
    top = wt_ref[...].astype(BF16)
    o_ref[...] = jnp.zeros_like(o_ref)
    o_ref[:n_a + QK_ROPE, :] = top
    o_ref[n_a + LANES:n_a + LANES + half, :] = top[n_a + half:]
    o_ref[n_a + LANES + half:n_a + LANES + QK_ROPE, :] = top[n_a:n_a + half]


def mla_in_weight(w_in_t, tn=512):
    d = w_in_t.shape[1]
    rows = Q_LORA + KV_LORA + QK_ROPE
    return pl.pallas_call(
        _mla_in_weight_body,
        grid=(d // tn,),
        in_specs=[pl.BlockSpec((rows, tn), lambda j: (0, j))],
        out_specs=pl.BlockSpec((A_WIDTH, tn), lambda j: (0, j)),
        out_shape=jax.ShapeDtypeStruct((A_WIDTH, d), BF16),
        compiler_params=_params("parallel"),
        name="mla_in_weight",
    )(w_in_t)


def _layer_weights(w_uq, w_ukv):
    wq = w_uq.reshape(Q_LORA, MLA_HEADS, QK_HEAD)
    rope = wq[:, :, QK_NOPE:]
    pad = jnp.zeros((Q_LORA, MLA_HEADS, LANES - QK_ROPE), w_uq.dtype)
    wq_all = jnp.concatenate([
        wq[:, :, :QK_NOPE].reshape(Q_LORA, -1),
        jnp.concatenate([rope, pad], axis=-1).reshape(Q_LORA, -1),
        jnp.concatenate([_swap_halves(rope), pad], axis=-1).reshape(Q_LORA, -1)], axis=1)
    wkv = w_ukv.reshape(KV_LORA, MLA_HEADS, QK_NOPE + V_HEAD)
    wk = wkv[:, :, :QK_NOPE].reshape(KV_LORA, -1)
    wvt = wkv[:, :, QK_NOPE:].reshape(KV_LORA, -1).T
    return wq_all.astype(BF16), wk.astype(BF16), wvt.astype(BF16)


def _rope_rows():
    inv_freq = ROPE_THETA ** (-jnp.arange(0, QK_ROPE, 2, dtype=F32) / QK_ROPE)
    zero = jnp.zeros((LANES - QK_ROPE,), F32)
    freq = jnp.concatenate([inv_freq, inv_freq, zero]).reshape(1, LANES)
    half = jnp.ones((QK_ROPE // 2,), F32)
    sign = jnp.concatenate([-half, half, zero]).reshape(1, LANES)
    return freq, sign


def kernel(x, p, positions, norm_mix, w_in, q_a_norm, kv_a_norm, w_uq, w_ukv, hg_lower_bound, hg_out_norm, w_o,
           norm_mlp, w_up, w_down, norm_ple, w_ple_gate, w_ple, ple_post_norm, final_norm):
    batch, seq, d_model = x.shape
    n = batch * seq
    depth = w_in.shape[0]
    h = x.reshape(n, d_model)
    pos = positions.reshape(n, 1)
    freq, sign = _rope_rows()
    for i in range(depth):
        w_in_t = w_in[i].T
        wq_all, wk, wvt = _layer_weights(w_uq[i], w_ukv[i])
        proj_a, xb, ss, w_ht = front(h, norm_mix[i], mla_in_weight(w_in_t), w_in_t, Q_LORA + KV_LORA + QK_ROPE)
        proj_h, (w_up_b,) = scaled_matmul(xb, ss, w_ht, F32, "in_proj_h", w_transposed=True, tn=1024,
                                          cast=(w_up[i],))
        q, k, vt = mla_prep(proj_a, pos, q_a_norm[i], kv_a_norm[i], wq_all, wk, wvt, freq, sign)
        o_mla, (w_o_b, w_pg_b) = causal_attention(q, k, vt, batch, seq, cast=(w_o[i], w_ple_gate[i]))
        o_hg = hgrn2(proj_h, 0, hg_lower_bound, hg_out_norm[i], batch, seq, i)
        h, hb, ss = out_proj_residual(o_mla, o_hg, w_o_b, h, norm_mlp[i])
        hidden, (w_down_b,) = scaled_matmul(hb, ss, w_up_b, BF16, "mlp_up", relu2=True, tn=1024, cast=(w_down[i],))
        h, hb, ss = mlp_down_residual(hidden, w_down_b, h, norm_ple[i])
        e = embed_norm(p[i].reshape(n, -1), w_ple[i].astype(BF16), ple_post_norm[i])
        if i + 1 < depth:
            h = gate_residual(hb, ss, w_pg_b, h, e)
        else:
            h = gate_residual_final_norm(hb, ss, w_pg_b, h, e, final_norm)
    return h.reshape(batch, seq, d_model)
```

```python
import functools

import numpy as np
import jax
import jax.numpy as jnp
from jax import lax
from jax.experimental import pallas as pl
from jax.experimental.pallas import tpu as pltpu

EPS = 1e-6
MLA_HEADS = 16
QK_NOPE = 128
QK_ROPE = 64
QK_HEAD = QK_NOPE + QK_ROPE
V_HEAD = 128
Q_LORA = 768
KV_LORA = 512
ROPE_THETA = 10000.0
HG_HEADS = 16
HG_DK = 128
HG_DV = 128
HG_FDIM = HG_HEADS * HG_DK
HG_VDIM = HG_HEADS * HG_DV

LANES = 128
MXU_WIDTH = 256
QK_PAD = 2 * LANES
A_WIDTH = Q_LORA + KV_LORA + 2 * LANES
HG_CHUNK = 128
VMEM_LIMIT_BYTES = 56 * 1024 * 1024
LOG2_E = 1.4426950408889634

F32 = jnp.float32
BF16 = jnp.bfloat16


def _params(*semantics):
    return pltpu.CompilerParams(dimension_semantics=semantics, vmem_limit_bytes=VMEM_LIMIT_BYTES)


def _rms(x, g):
    return x * lax.rsqrt(jnp.mean(x * x, axis=-1, keepdims=True) + EPS) * g


def _emit_norm_inputs(h, g_ref, hb_ref, ss_ref, first):
    hb_ref[...] = (h * g_ref[...]).astype(BF16)

    @pl.when(first)
    def _():
        ss_ref[...] = jnp.zeros_like(ss_ref)

    ss_ref[...] += jnp.sum(h * h, axis=-1, keepdims=True)


def _row_scale(ss_ref, d):
    return lax.rsqrt(ss_ref[:, :1] / d + EPS)


def _cast_specs(ws, n_steps, step_of):
    specs, shapes = [], []
    for w in ws:
        rows = w.shape[0] // n_steps
        assert rows * n_steps == w.shape[0] and rows % 16 == 0
        specs.append(pl.BlockSpec((rows, w.shape[1]), lambda *ids: (step_of(*ids), 0)))
        shapes.append(jax.ShapeDtypeStruct(w.shape, BF16))
    return specs, shapes


def _cast_slabs(src_refs, dst_refs):
    for src, dst in zip(src_refs, dst_refs):
        dst[...] = src[...].astype(dst.dtype)


TRANS_B = (((1,), (1,)), ((), ()))


def _scaled_mm_body(a_ref, w_ref, ss_ref, *refs, relu2, w_transposed):
    n_cast = len(refs) // 2
    o_ref = refs[n_cast]
    if w_transposed:
        acc = lax.dot_general(a_ref[...], w_ref[...], TRANS_B, preferred_element_type=F32)
    else:
        acc = jnp.dot(a_ref[...], w_ref[...], preferred_element_type=F32)
    acc = acc * _row_scale(ss_ref, a_ref.shape[1])
    if relu2:
        acc = jnp.square(jnp.maximum(acc, 0.0))
    o_ref[...] = acc.astype(o_ref.dtype)
    _cast_slabs(refs[:n_cast], refs[n_cast + 1:])


def scaled_matmul(hb, ss, w, out_dtype, name, relu2=False, w_transposed=False, tm=1024, tn=512, cast=()):
    m, k = hb.shape
    n = w.shape[0] if w_transposed else w.shape[1]
    nj = n // tn
    cast_specs, cast_shapes = _cast_specs(cast, (m // tm) * nj, lambda i, j: i * nj + j)
    w_spec = pl.BlockSpec((tn, k), lambda i, j: (j, 0)) if w_transposed else pl.BlockSpec((k, tn), lambda i, j: (0, j))
    outs = pl.pallas_call(
        functools.partial(_scaled_mm_body, relu2=relu2, w_transposed=w_transposed),
        grid=(m // tm, nj),
        in_specs=[pl.BlockSpec((tm, k), lambda i, j: (i, 0)), w_spec,
                  pl.BlockSpec((tm, LANES), lambda i, j: (i, 0))] + cast_specs,
        out_specs=[pl.BlockSpec((tm, tn), lambda i, j: (i, j))] + cast_specs,
        out_shape=[jax.ShapeDtypeStruct((m, n), out_dtype)] + cast_shapes,
        compiler_params=_params("parallel", "parallel"),
        name=name,
    )(hb, w, ss, *cast)
    return outs[0], outs[1:]


def _front_body(x_ref, g_ref, wat_ref, wt_lo_ref, wt_hi_ref, pa_ref, xb_ref, ss_ref, wht_ref):
    x = x_ref[...]
    xb = (x * g_ref[...]).astype(BF16)
    xb_ref[...] = xb
    ss = jnp.sum(x * x, axis=-1, keepdims=True)
    ss_ref[...] = jnp.broadcast_to(ss, ss_ref.shape)
    acc = lax.dot_general(xb, wat_ref[...], TRANS_B, preferred_element_type=F32)
    pa_ref[...] = acc * lax.rsqrt(ss / x.shape[1] + EPS)
    half = wt_lo_ref.shape[0]
    wht_ref[:half, :] = wt_lo_ref[...].astype(BF16)
    wht_ref[half:, :] = wt_hi_ref[...].astype(BF16)


def front(x, gain, w_at, w_t, h_row0, tm=256):
    m, d = x.shape
    n = w_at.shape[0]
    steps = m // tm
    h_rows = w_t.shape[0] - h_row0
    half = h_rows // (2 * steps)
    assert 2 * half * steps == h_rows and half % 16 == 0 and h_row0 % half == 0
    first = h_row0 // half
    row = lambda i: (i, 0)
    return pl.pallas_call(
        _front_body,
        grid=(steps,),
        in_specs=[pl.BlockSpec((tm, d), row), pl.BlockSpec((1, d), lambda i: (0, 0)),
                  pl.BlockSpec((n, d), lambda i: (0, 0), pipeline_mode=pl.Buffered(1)),
                  pl.BlockSpec((half, d), lambda i: (first + 2 * i, 0)),
                  pl.BlockSpec((half, d), lambda i: (first + 2 * i + 1, 0))],
        out_specs=[pl.BlockSpec((tm, n), row), pl.BlockSpec((tm, d), row), pl.BlockSpec((tm, LANES), row),
                   pl.BlockSpec((2 * half, d), row)],
        out_shape=[jax.ShapeDtypeStruct((m, n), F32), jax.ShapeDtypeStruct((m, d), BF16),
                   jax.ShapeDtypeStruct((m, LANES), F32), jax.ShapeDtypeStruct((h_rows, d), BF16)],
        compiler_params=_params("parallel"),
        name="front",
    )(x, gain.reshape(1, d), w_at, w_t, w_t)


def _norm_outputs(m, n, tm, tn, index):
    specs = [pl.BlockSpec((tm, tn), index), pl.BlockSpec((tm, tn), index),
             pl.BlockSpec((tm, LANES), lambda i, *_: (i, 0))]
    shapes = [jax.ShapeDtypeStruct((m, n), F32), jax.ShapeDtypeStruct((m, n), BF16),
              jax.ShapeDtypeStruct((m, LANES), F32)]
    return specs, shapes


def _out_proj_body(a1_ref, a2_ref, w1_ref, w2_ref, r_ref, g_ref, o_ref, hb_ref, ss_ref):
    acc = jnp.dot(a1_ref[...], w1_ref[...], preferred_element_type=F32)
    acc = acc + jnp.dot(a2_ref[...], w2_ref[...], preferred_element_type=F32)
    h = r_ref[...] + acc
    o_ref[...] = h
    _emit_norm_inputs(h, g_ref, hb_ref, ss_ref, pl.program_id(1) == 0)


def out_proj_residual(a1, a2, w, res, gain, tm=1024, tn=512):
    m, k1 = a1.shape
    k2 = a2.shape[1]
    assert k1 == k2 and w.shape[0] == k1 + k2
    n = w.shape[1]
    tile = lambda i, j: (i, j)
    out_specs, out_shape = _norm_outputs(m, n, tm, tn, tile)
    return pl.pallas_call(
        _out_proj_body,
        grid=(m // tm, n // tn),
        in_specs=[pl.BlockSpec((tm, k1), lambda i, j: (i, 0)), pl.BlockSpec((tm, k2), lambda i, j: (i, 0)),
                  pl.BlockSpec((k1, tn), lambda i, j: (0, j)), pl.BlockSpec((k2, tn), lambda i, j: (1, j)),
                  pl.BlockSpec((tm, tn), tile), pl.BlockSpec((1, tn), lambda i, j: (0, j))],
        out_specs=out_specs,
        out_shape=out_shape,
        compiler_params=_params("parallel", "arbitrary"),
        name="out_proj",
    )(a1, a2, w, w, res, gain.reshape(1, n))


def _mlp_down_body(a_ref, w_ref, r_ref, g_ref, o_ref, hb_ref, ss_ref, acc_ref):
    k = pl.program_id(2)
    last = pl.num_programs(2) - 1

    def partial_product():
        return jnp.dot(a_ref[...], w_ref[...], preferred_element_type=F32)

    @pl.when(k == 0)
    def _():
        acc_ref[...] = r_ref[...] + partial_product()

    @pl.when(jnp.logical_and(k > 0, k < last))
    def _():
        acc_ref[...] += partial_product()

    @pl.when(k == last)
    def _():
        h = acc_ref[...] + partial_product()
        o_ref[...] = h
        _emit_norm_inputs(h, g_ref, hb_ref, ss_ref, pl.program_id(1) == 0)


def mlp_down_residual(a, w, res, gain, tm=1024, tn=1024, tk=2048):
    m, k = a.shape
    n = w.shape[1]
    assert k // tk >= 2
    tile = lambda i, j, kk: (i, j)
    out_specs, out_shape = _norm_outputs(m, n, tm, tn, tile)
    return pl.pallas_call(
        _mlp_down_body,
        grid=(m // tm, n // tn, k // tk),
        in_specs=[pl.BlockSpec((tm, tk), lambda i, j, kk: (i, kk)), pl.BlockSpec((tk, tn), lambda i, j, kk: (kk, j)),
                  pl.BlockSpec((tm, tn), tile), pl.BlockSpec((1, tn), lambda i, j, kk: (0, j))],
        out_specs=out_specs,
        out_shape=out_shape,
        scratch_shapes=[pltpu.VMEM((tm, tn), F32)],
        compiler_params=_params("parallel", "arbitrary", "arbitrary"),
        name="mlp_down",
    )(a, w, res, gain.reshape(1, n))


def _gate_body(a_ref, w_ref, ss_ref, h_ref, e_ref, o_ref):
    tm, d = a_ref.shape
    half = tm // 2
    for rows in (slice(0, half), slice(half, tm)):
        scale = lax.rsqrt(ss_ref[rows, :1] / d + EPS)
        acc = jnp.dot(a_ref[rows, :], w_ref[...], preferred_element_type=F32) * scale
        o_ref[rows, :] = h_ref[rows, :] + jax.nn.sigmoid(acc) * e_ref[rows, :]


def gate_residual(hb, ss, w, h, e, tm=1024, tn=512):
    m, k = hb.shape
    n = w.shape[1]
    return pl.pallas_call(
        _gate_body,
        grid=(m // tm, n // tn),
        in_specs=[pl.BlockSpec((tm, k), lambda i, j: (i, 0)), pl.BlockSpec((k, tn), lambda i, j: (0, j)),
                  pl.BlockSpec((tm, LANES), lambda i, j: (i, 0)),
                  pl.BlockSpec((tm, tn), lambda i, j: (i, j)), pl.BlockSpec((tm, tn), lambda i, j: (i, j))],
        out_specs=pl.BlockSpec((tm, tn), lambda i, j: (i, j)),
        out_shape=jax.ShapeDtypeStruct((m, n), F32),
        compiler_params=_params("parallel", "parallel"),
        name="ple_gate",
    )(hb, w, ss, h, e)


def _rmsnorm_body(x_ref, g_ref, o_ref):
    o_ref[...] = _rms(x_ref[...], g_ref[...]).astype(o_ref.dtype)


def rmsnorm(x, g, out_dtype, tm=256):
    n, d = x.shape
    return pl.pallas_call(
        _rmsnorm_body,
        grid=(n // tm,),
        in_specs=[pl.BlockSpec((tm, d), lambda i: (i, 0)), pl.BlockSpec((1, d), lambda i: (0, 0))],
        out_specs=pl.BlockSpec((tm, d), lambda i: (i, 0)),
        out_shape=jax.ShapeDtypeStruct((n, d), out_dtype),
        compiler_params=_params("parallel"),
        name="rmsnorm",
    )(x, g.reshape(1, d))


def _embed_body(p_ref, w_ref, g_ref, o_ref):
    y = jnp.dot(p_ref[...].astype(BF16), w_ref[...], preferred_element_type=F32)
    o_ref[...] = _rms(y, g_ref[...])


def embed_norm(p, w, g, tm=256):
    m, k = p.shape
    n = w.shape[1]
    return pl.pallas_call(
        _embed_body,
        grid=(m // tm,),
        in_specs=[pl.BlockSpec((tm, k), lambda i: (i, 0)), pl.BlockSpec((k, n), lambda i: (0, 0)),
                  pl.BlockSpec((1, n), lambda i: (0, 0))],
        out_specs=pl.BlockSpec((tm, n), lambda i: (i, 0)),
        out_shape=jax.ShapeDtypeStruct((m, n), F32),
        compiler_params=_params("parallel"),
        name="ple_embed",
    )(p, w, g.reshape(1, n))


def _mla_prep_body(pa_ref, pos_ref, gq_ref, gkv_ref, wq_ref, wk_ref, wvt_ref, freq_ref, sign_ref, q_ref, k_ref, vt_ref):
    pa = pa_ref[...]
    cq = _rms(pa[:, :Q_LORA], gq_ref[...]).astype(BF16)
    ckv = _rms(pa[:, Q_LORA:Q_LORA + KV_LORA], gkv_ref[...]).astype(BF16)
    kr = pa[:, Q_LORA + KV_LORA:Q_LORA + KV_LORA + LANES]
    krs = pa[:, Q_LORA + KV_LORA + LANES:]
    ang = pos_ref[...].astype(F32) * freq_ref[...]
    cos = jnp.cos(ang)
    sin = jnp.sin(ang) * sign_ref[...]
    q = jnp.dot(cq, wq_ref[...], preferred_element_type=F32)
    kn = jnp.dot(ckv, wk_ref[...], preferred_element_type=F32)
    vt_ref[...] = lax.dot_general(wvt_ref[...], ckv, (((1,), (1,)), ((), ())),
                                  preferred_element_type=F32).astype(BF16)
    krot = (kr * cos + krs * sin).astype(BF16)
    qscale = (QK_HEAD ** -0.5) * LOG2_E
    hn = MLA_HEADS * LANES
    for h in range(MLA_HEADS):
        lo, hi = h * LANES, (h + 1) * LANES
        qrot = q[:, hn + lo:hn + hi] * cos + q[:, 2 * hn + lo:2 * hn + hi] * sin
        q_ref[:, h * QK_PAD:h * QK_PAD + LANES] = (q[:, lo:hi] * qscale).astype(BF16)
        q_ref[:, h * QK_PAD + LANES:(h + 1) * QK_PAD] = (qrot * qscale).astype(BF16)
        k_ref[:, h * QK_PAD:h * QK_PAD + LANES] = kn[:, lo:hi].astype(BF16)
        k_ref[:, h * QK_PAD + LANES:(h + 1) * QK_PAD] = krot


def mla_prep(proj, pos, gq, gkv, wq, wk, wvt, freq, sign, tm=256):
    n = proj.shape[0]
    const = lambda i: (0, 0)
    return pl.pallas_call(
        _mla_prep_body,
        grid=(n // tm,),
        in_specs=[pl.BlockSpec((tm, A_WIDTH), lambda i: (i, 0)), pl.BlockSpec((tm, 1), lambda i: (i, 0)),
                  pl.BlockSpec((1, Q_LORA), const), pl.BlockSpec((1, KV_LORA), const),
                  pl.BlockSpec(wq.shape, const), pl.BlockSpec(wk.shape, const), pl.BlockSpec(wvt.shape, const),
                  pl.BlockSpec((1, LANES), const), pl.BlockSpec((1, LANES), const)],
        out_specs=[pl.BlockSpec((tm, MLA_HEADS * QK_PAD), lambda i: (i, 0)),
                   pl.BlockSpec((tm, MLA_HEADS * QK_PAD), lambda i: (i, 0)),
                   pl.BlockSpec((MLA_HEADS * V_HEAD, tm), lambda i: (0, i))],
        out_shape=[jax.ShapeDtypeStruct((n, MLA_HEADS * QK_PAD), BF16),
                   jax.ShapeDtypeStruct((n, MLA_HEADS * QK_PAD), BF16),
                   jax.ShapeDtypeStruct((MLA_HEADS * V_HEAD, n), BF16)],
        compiler_params=_params("parallel"),
        name="mla_prep",
    )(proj, pos, gq.reshape(1, -1), gkv.reshape(1, -1), wq, wk, wvt, freq, sign)


def _attn_body(q_ref, k_ref, vt_ref, *refs, tq, tk, n_cast):
    o_ref = refs[n_cast]
    s_ref, mx_ref, m_ref, l_ref, acc_ref = refs[2 * n_cast + 1:]
    _cast_slabs(refs[:n_cast], refs[n_cast + 1:2 * n_cast + 1])
    seq = q_ref.shape[0]
    n_diag = tq // tk
    gw = 2 * LANES
    tiles = [(qi, j) for qi in range(seq // tq) for j in range((qi + 1) * n_diag)]

    def keys_needed(qi, j, c):
        diag = j - qi * n_diag
        return tk if diag < 0 else max(0, min(tk, c + gw - diag * tk))

    def produce(slot, qi, j):
        for c in range(0, tq, gw):
            nk = keys_needed(qi, j, c)
            if nk == 0:
                continue
            q = q_ref[qi * tq + c:qi * tq + c + gw, :]
            s = lax.dot_general(k_ref[j * tk:j * tk + nk, :], q, (((1,), (1,)), ((), ())),
                                preferred_element_type=F32)
            s_ref[slot, :nk, c:c + gw] = s
            mx_ref[slot, :, c:c + gw] = jnp.max(s, axis=0, keepdims=True)

    def consume(slot, qi, j):
        for c in range(0, tq, gw):
            cols = slice(c, c + gw)
            nk = keys_needed(qi, j, c)
            if nk == 0:
                continue
            s = s_ref[slot, :nk, cols]
            first_key = (j - qi * n_diag) * tk
            if first_key + nk - 1 <= c:
                tile_max = mx_ref[slot, :, cols]
            else:
                key = first_key + lax.broadcasted_iota(jnp.int32, (nk, gw), 0)
                qry = c + lax.broadcasted_iota(jnp.int32, (nk, gw), 1)
                s = jnp.where(key <= qry, s, -jnp.inf)
                tile_max = jnp.max(s, axis=0, keepdims=True)
            vt = vt_ref[:, j * tk:j * tk + nk]
            if j == 0:
                m_new = tile_max
                p = jnp.exp2(s - m_new)
                l_ref[:, cols] = jnp.sum(p, axis=0, keepdims=True)
                acc_ref[:, cols] = jnp.dot(vt, p.astype(BF16), preferred_element_type=F32)
            else:
                m = m_ref[:, cols]
                m_new = jnp.maximum(m, tile_max)
                alpha = jnp.exp2(m - m_new)
                p = jnp.exp2(s - m_new)
                l_ref[:, cols] = alpha * l_ref[:, cols] + jnp.sum(p, axis=0, keepdims=True)
                acc_ref[:, cols] = alpha * acc_ref[:, cols] + jnp.dot(vt, p.astype(BF16), preferred_element_type=F32)
            m_ref[:, cols] = m_new

    produce(0, *tiles[0])
    for t, (qi, j) in enumerate(tiles):
        if t + 1 < len(tiles):
            produce((t + 1) % 2, *tiles[t + 1])
        consume(t % 2, qi, j)
        if j == (qi + 1) * n_diag - 1:
            o_ref[qi * tq:(qi + 1) * tq, :] = (acc_ref[...] / l_ref[...]).T.astype(o_ref.dtype)


def causal_attention(q, k, vt, batch, seq, tq=1024, tk=512, cast=()):
    cast_specs, cast_shapes = _cast_specs(cast, batch * MLA_HEADS, lambda b, h: b * MLA_HEADS + h)
    outs = pl.pallas_call(
        functools.partial(_attn_body, tq=tq, tk=tk, n_cast=len(cast)),
        grid=(batch, MLA_HEADS),
        in_specs=[pl.BlockSpec((seq, QK_PAD), lambda b, h: (b, h)), pl.BlockSpec((seq, QK_PAD), lambda b, h: (b, h)),
                  pl.BlockSpec((V_HEAD, seq), lambda b, h: (h, b))] + cast_specs,
        out_specs=[pl.BlockSpec((seq, V_HEAD), lambda b, h: (b, h))] + cast_specs,
        out_shape=[jax.ShapeDtypeStruct((batch * seq, MLA_HEADS * V_HEAD), BF16)] + cast_shapes,
        scratch_shapes=[pltpu.VMEM((2, tk, tq), F32), pltpu.VMEM((2, 1, tq), F32)] + [pltpu.VMEM((1, tq), F32)] * 2
        + [pltpu.VMEM((V_HEAD, tq), F32)],
        compiler_params=_params("parallel", "parallel"),
        name="mla_attention",
    )(q, k, vt, *cast)
    return outs[0], outs[1:]


def _hgrn_constants(c):
    t = np.arange(c)
    ltri = (t[None, :] <= t[:, None]).astype(np.float32)
    nlev = int(np.log2(c))
    x = t[:, None] ^ t[None, :]
    level = np.full((c, c), -1, np.int32)
    lower = t[:, None] > t[None, :]
    level[lower] = (nlev - 1) - np.floor(np.log2(x[lower])).astype(np.int32)
    level[t, t] = nlev
    return np.concatenate([ltri, ltri, ltri], axis=1), level, nlev


def _level_exponent(b, half):
    c, dk = b.shape
    if half >= 8:
        x = b.reshape(c // (2 * half), 2 * half, dk)
        ref = x[:, half - 1:half, :]
        return jnp.concatenate([ref - x[:, :half, :], x[:, half:, :] - ref], axis=1).reshape(c, dk)
    x = b.reshape(c // 8, 8, dk)
    if half == 4:
        ref = jnp.broadcast_to(x[:, 3:4, :], x.shape)
    else:
        assert half == 2
        sub = lax.broadcasted_iota(jnp.int32, x.shape, 1)
        ref = jnp.where(sub < 4, x[:, 1:2, :], x[:, 5:6, :])
    return (-jnp.abs(x - ref)).reshape(c, dk)


def _hgrn_body(hq_ref, hf_ref, hi_ref, hg_ref, lbp_ref, gn_ref, w3_ref, lev_ref, o_ref, st_ref, *, layer, tile, chunk, nlev):
    @pl.when(pl.program_id(2) == 0)
    def _():
        st_ref[...] = jnp.zeros_like(st_ref)

    hb = lbp_ref[...]
    ex = jnp.exp(hb - jnp.max(hb, axis=0, keepdims=True))
    sm = ex / jnp.sum(ex, axis=0, keepdims=True)
    lb = jnp.sum(sm[:layer + 1], axis=0, keepdims=True)
    oml = 1.0 - lb
    lev = lev_ref[...]
    gn = gn_ref[...]
    trans_b = (((1,), (1,)), ((), ()))
    chunks = [slice(c * chunk, (c + 1) * chunk) for c in range(tile // chunk)]
    fs, qs, kks, vs, bs = [], [], [], [], []
    for rows in chunks:
        z = hf_ref[rows, :]
        f = lb + oml * jax.nn.sigmoid(z)
        g = jnp.log(f)
        hq = hq_ref[rows, :]
        fs.append(f)
        kks.append(oml * jax.nn.sigmoid(-z))
        qs.append(hq * jax.nn.sigmoid(hq))
        vs.append(hi_ref[rows, :].astype(BF16))
        g_hi = g.astype(BF16)
        r1 = g - g_hi.astype(F32)
        g_mid = r1.astype(BF16)
        g_lo = (r1 - g_mid.astype(F32)).astype(BF16)
        g3 = jnp.concatenate([g_hi, g_mid, g_lo], axis=0)
        bs.append(jnp.dot(w3_ref[...], g3, preferred_element_type=F32))
    scores = [jnp.zeros((chunk, chunk), F32) for _ in chunks]
    q16 = [q.astype(BF16) for q in qs]
    k16 = [kk.astype(BF16) for kk in kks]
    for lv in range(nlev + 1):
        half = chunk >> (lv + 1)
        for c, (f, q, kk, b) in enumerate(zip(fs, q16, k16, bs)):
            if half >= 2:
                e = jnp.exp(_level_exponent(b, half)).astype(BF16)
                ql, kl = q * e, kk * e
            elif half == 1:
                ql, kl = q * f.astype(BF16), kk
            else:
                ql, kl = q, kk
            p = lax.dot_general(ql, kl, trans_b, preferred_element_type=F32)
            scores[c] = jnp.where(lev == lv, p, scores[c])
    intra, updates, decays, qbs = [], [], [], []
    for a, q, kk, v, b in zip(scores, qs, kks, vs, bs):
        intra.append(jnp.dot(a.astype(BF16), v, preferred_element_type=F32))
        b_last = b[chunk - 1:chunk, :]
        kh = (kk * jnp.exp(b_last - b)).astype(BF16)
        updates.append(lax.dot_general(v, kh, (((0,), (0,)), ((), ())), preferred_element_type=F32))
        decays.append(jnp.exp(b_last))
        qbs.append((q * jnp.exp(b)).astype(BF16))
    state = st_ref[...]
    for rows, qb, o_intra, upd, dec in zip(chunks, qbs, intra, updates, decays):
        o = lax.dot_general(qb, state.astype(BF16), trans_b, preferred_element_type=F32) + o_intra
        state = dec * state + upd
        hg = hg_ref[rows, :]
        y = _rms(o, gn) * (hg * jax.nn.sigmoid(hg))
        o_ref[rows, :] = y.astype(o_ref.dtype)
    st_ref[...] = state


def hgrn2(proj, col0, lbp, gn, batch, seq, layer, tile=1024):
    w3, level, nlev = _hgrn_constants(HG_CHUNK)
    w3 = jnp.asarray(w3, BF16)
    level = jnp.asarray(level)
    nt = seq // tile
    cb = col0 // LANES

    def col(group):
        return lambda b, h, i: (b * nt + i, cb + group * HG_HEADS + h)

    return pl.pallas_call(
        functools.partial(_hgrn_body, layer=layer, tile=tile, chunk=HG_CHUNK, nlev=nlev),
        grid=(batch, HG_HEADS, nt),
        in_specs=[pl.BlockSpec((tile, LANES), col(0)), pl.BlockSpec((tile, LANES), col(1)),
                  pl.BlockSpec((tile, LANES), col(2)), pl.BlockSpec((tile, LANES), col(3)),
                  pl.BlockSpec((lbp.shape[0], LANES), lambda b, h, i: (0, h)),
                  pl.BlockSpec((1, LANES), lambda b, h, i: (0, h)),
                  pl.BlockSpec(w3.shape, lambda b, h, i: (0, 0)),
                  pl.BlockSpec(level.shape, lambda b, h, i: (0, 0))],
        out_specs=pl.BlockSpec((tile, LANES), lambda b, h, i: (b * nt + i, h)),
        out_shape=jax.ShapeDtypeStruct((batch * seq, HG_VDIM), BF16),
        scratch_shapes=[pltpu.VMEM((HG_DV, HG_DK), F32)],
        compiler_params=_params("parallel", "parallel", "arbitrary"),
        name="hgrn2",
    )(proj, proj, proj, proj, lbp, gn.reshape(1, -1), w3, level)


def _swap_halves(w):
    half = w.shape[-1] // 2
    return jnp.concatenate([w[..., half:], w[..., :half]], axis=-1)


def _mla_in_weight_body(wt_ref, o_ref):
    n_a = Q_LORA + KV_LORA
    half = QK_ROPE // 2
    top = wt_ref[...].astype(BF16)
    o_ref[...] = jnp.zeros_like(o_ref)
    o_ref[:n_a + QK_ROPE, :] = top
    o_ref[n_a + LANES:n_a + LANES + half, :] = top[n_a + half:]
    o_ref[n_a + LANES + half:n_a + LANES + QK_ROPE, :] = top[n_a:n_a + half]


def mla_in_weight(w_in_t, tn=512):
    d = w_in_t.shape[1]
    rows = Q_LORA + KV_LORA + QK_ROPE
    return pl.pallas_call(
        _mla_in_weight_body,
        grid=(d // tn,),
        in_specs=[pl.BlockSpec((rows, tn), lambda j: (0, j))],
        out_specs=pl.BlockSpec((A_WIDTH, tn), lambda j: (0, j)),
        out_shape=jax.ShapeDtypeStruct((A_WIDTH, d), BF16),
        compiler_params=_params("parallel"),
        name="mla_in_weight",
    )(w_in_t)


def _layer_weights(w_uq, w_ukv):
    wq = w_uq.reshape(Q_LORA, MLA_HEADS, QK_HEAD)
    rope = wq[:, :, QK_NOPE:]
    pad = jnp.zeros((Q_LORA, MLA_HEADS, LANES - QK_ROPE), w_uq.dtype)
    wq_all = jnp.concatenate([
        wq[:, :, :QK_NOPE].reshape(Q_LORA, -1),
        jnp.concatenate([rope, pad], axis=-1).reshape(Q_LORA, -1),
        jnp.concatenate([_swap_halves(rope), pad], axis=-1).reshape(Q_LORA, -1)], axis=1)
    wkv = w_ukv.reshape(KV_LORA, MLA_HEADS, QK_NOPE + V_HEAD)
    wk = wkv[:, :, :QK_NOPE].reshape(KV_LORA, -1)
    wvt = wkv[:, :, QK_NOPE:].reshape(KV_LORA, -1).T
    return wq_all.astype(BF16), wk.astype(BF16), wvt.astype(BF16)


def _rope_rows():
    inv_freq = ROPE_THETA ** (-jnp.arange(0, QK_ROPE, 2, dtype=F32) / QK_ROPE)
    zero = jnp.zeros((LANES - QK_ROPE,), F32)
    freq = jnp.concatenate([inv_freq, inv_freq, zero]).reshape(1, LANES)
    half = jnp.ones((QK_ROPE // 2,), F32)
    sign = jnp.concatenate([-half, half, zero]).reshape(1, LANES)
    return freq, sign


def kernel(x, p, positions, norm_mix, w_in, q_a_norm, kv_a_norm, w_uq, w_ukv, hg_lower_bound, hg_out_norm, w_o,
           norm_mlp, w_up, w_down, norm_ple, w_ple_gate, w_ple, ple_post_norm, final_norm):
    batch, seq, d_model = x.shape
    n = batch * seq
    depth = w_in.shape[0]
    h = x.reshape(n, d_model)
    pos = positions.reshape(n, 1)
    freq, sign = _rope_rows()
    for i in range(depth):
        w_in_t = w_in[i].T
        wq_all, wk, wvt = _layer_weights(w_uq[i], w_ukv[i])
        proj_a, xb, ss, w_ht = front(h, norm_mix[i], mla_in_weight(w_in_t), w_in_t, Q_LORA + KV_LORA + QK_ROPE)
        proj_h, (w_up_b,) = scaled_matmul(xb, ss, w_ht, F32, "in_proj_h", w_transposed=True, tn=1024,
                                          cast=(w_up[i],))
        q, k, vt = mla_prep(proj_a, pos, q_a_norm[i], kv_a_norm[i], wq_all, wk, wvt, freq, sign)
        o_mla, (w_o_b, w_pg_b) = causal_attention(q, k, vt, batch, seq, cast=(w_o[i], w_ple_gate[i]))
        o_hg = hgrn2(proj_h, 0, hg_lower_bound, hg_out_norm[i], batch, seq, i)
        h, hb, ss = out_proj_residual(o_mla, o_hg, w_o_b, h, norm_mlp[i])
        hidden, (w_down_b,) = scaled_matmul(hb, ss, w_up_b, BF16, "mlp_up", relu2=True, tn=1024, cast=(w_down[i],))
        h, hb, ss = mlp_down_residual(hidden, w_down_b, h, norm_ple[i])
        e = embed_norm(p[i].reshape(n, -1), w_ple[i].astype(BF16), ple_post_norm[i])
        h = gate_residual(hb, ss, w_pg_b, h, e)
    return rmsnorm(h, final_norm, x.dtype).reshape(batch, seq, d_model)
```

```python
import functools

import numpy as np
import jax
import jax.numpy as jnp
from jax import lax
from jax.experimental import pallas as pl
from jax.experimental.pallas import tpu as pltpu

EPS = 1e-6
MLA_HEADS = 16
QK_NOPE = 128
QK_ROPE = 64
QK_HEAD = QK_NOPE + QK_ROPE
V_HEAD = 128
Q_LORA = 768
KV_LORA = 512
ROPE_THETA = 10000.0
HG_HEADS = 16
HG_DK = 128
HG_DV = 128
HG_FDIM = HG_HEADS * HG_DK
HG_VDIM = HG_HEADS * HG_DV

LANES = 128
QK_PAD = 2 * LANES
A_WIDTH = Q_LORA + KV_LORA + 2 * LANES
HG_CHUNK = 128
VMEM_LIMIT_BYTES = 56 * 1024 * 1024
LOG2_E = 1.4426950408889634
VT_ROWS = V_HEAD + 16

F32 = jnp.float32
BF16 = jnp.bfloat16


def _params(*semantics):
    return pltpu.CompilerParams(dimension_semantics=semantics, vmem_limit_bytes=VMEM_LIMIT_BYTES)


def _rms(x, g):
    return x * lax.rsqrt(jnp.mean(x * x, axis=-1, keepdims=True) + EPS) * g


def _emit_norm_inputs(h, g_ref, hb_ref, ss_ref, first):
    hb_ref[...] = (h * g_ref[...]).astype(BF16)

    @pl.when(first)
    def _():
        ss_ref[...] = jnp.zeros_like(ss_ref)

    ss_ref[...] += jnp.sum(h * h, axis=-1, keepdims=True)


def _row_scale(ss_ref, d):
    return lax.rsqrt(ss_ref[:, :1] / d + EPS)


def _cast_specs(ws, n_steps, step_of):
    specs, shapes = [], []
    for w in ws:
        rows = w.shape[0] // n_steps
        assert rows * n_steps == w.shape[0] and rows % 16 == 0
        specs.append(pl.BlockSpec((rows, w.shape[1]), lambda *ids: (step_of(*ids), 0)))
        shapes.append(jax.ShapeDtypeStruct(w.shape, BF16))
    return specs, shapes


def _cast_slabs(src_refs, dst_refs):
    for src, dst in zip(src_refs, dst_refs):
        dst[...] = src[...].astype(dst.dtype)


TRANS_B = (((1,), (1,)), ((), ()))


def _scaled_mm_body(a_ref, w_ref, ss_ref, *refs, relu2, w_transposed):
    n_cast = len(refs) // 2
    o_ref = refs[n_cast]
    if w_transposed:
        acc = lax.dot_general(a_ref[...], w_ref[...], TRANS_B, preferred_element_type=F32)
    else:
        acc = jnp.dot(a_ref[...], w_ref[...], preferred_element_type=F32)
    acc = acc * _row_scale(ss_ref, a_ref.shape[1])
    if relu2:
        acc = jnp.square(jnp.maximum(acc, 0.0))
    o_ref[...] = acc.astype(o_ref.dtype)
    _cast_slabs(refs[:n_cast], refs[n_cast + 1:])


def scaled_matmul(hb, ss, w, out_dtype, name, relu2=False, w_transposed=False, tm=1024, tn=512, cast=()):
    m, k = hb.shape
    n = w.shape[0] if w_transposed else w.shape[1]
    nj = n // tn
    cast_specs, cast_shapes = _cast_specs(cast, (m // tm) * nj, lambda i, j: i * nj + j)
    w_spec = pl.BlockSpec((tn, k), lambda i, j: (j, 0)) if w_transposed else pl.BlockSpec((k, tn), lambda i, j: (0, j))
    outs = pl.pallas_call(
        functools.partial(_scaled_mm_body, relu2=relu2, w_transposed=w_transposed),
        grid=(m // tm, nj),
        in_specs=[pl.BlockSpec((tm, k), lambda i, j: (i, 0)), w_spec,
                  pl.BlockSpec((tm, LANES), lambda i, j: (i, 0))] + cast_specs,
        out_specs=[pl.BlockSpec((tm, tn), lambda i, j: (i, j))] + cast_specs,
        out_shape=[jax.ShapeDtypeStruct((m, n), out_dtype)] + cast_shapes,
        compiler_params=_params("parallel", "parallel"),
        name=name,
    )(hb, w, ss, *cast)
    return outs[0], outs[1:]


def _front_body(x_ref, g_ref, wat_ref, wt_lo_ref, wt_hi_ref, pa_ref, xb_ref, ss_ref, wht_ref):
    x = x_ref[...]
    xb = (x * g_ref[...]).astype(BF16)
    xb_ref[...] = xb
    ss = jnp.sum(x * x, axis=-1, keepdims=True)
    ss_ref[...] = jnp.broadcast_to(ss, ss_ref.shape)
    acc = lax.dot_general(xb, wat_ref[...], TRANS_B, preferred_element_type=F32)
    pa_ref[...] = acc * lax.rsqrt(ss / x.shape[1] + EPS)
    half = wt_lo_ref.shape[0]
    wht_ref[:half, :] = wt_lo_ref[...].astype(BF16)
    wht_ref[half:, :] = wt_hi_ref[...].astype(BF16)


def front(x, gain, w_at, w_t, h_row0, tm=256):
    m, d = x.shape
    n = w_at.shape[0]
    steps = m // tm
    h_rows = w_t.shape[0] - h_row0
    half = h_rows // (2 * steps)
    assert 2 * half * steps == h_rows and half % 16 == 0 and h_row0 % half == 0
    first = h_row0 // half
    row = lambda i: (i, 0)
    return pl.pallas_call(
        _front_body,
        grid=(steps,),
        in_specs=[pl.BlockSpec((tm, d), row), pl.BlockSpec((1, d), lambda i: (0, 0)),
                  pl.BlockSpec((n, d), lambda i: (0, 0), pipeline_mode=pl.Buffered(1)),
                  pl.BlockSpec((half, d), lambda i: (first + 2 * i, 0)),
                  pl.BlockSpec((half, d), lambda i: (first + 2 * i + 1, 0))],
        out_specs=[pl.BlockSpec((tm, n), row), pl.BlockSpec((tm, d), row), pl.BlockSpec((tm, LANES), row),
                   pl.BlockSpec((2 * half, d), row)],
        out_shape=[jax.ShapeDtypeStruct((m, n), F32), jax.ShapeDtypeStruct((m, d), BF16),
                   jax.ShapeDtypeStruct((m, LANES), F32), jax.ShapeDtypeStruct((h_rows, d), BF16)],
        compiler_params=_params("parallel"),
        name="front",
    )(x, gain.reshape(1, d), w_at, w_t, w_t)


def _norm_outputs(m, n, tm, tn, index):
    specs = [pl.BlockSpec((tm, tn), index), pl.BlockSpec((tm, tn), index),
             pl.BlockSpec((tm, LANES), lambda i, *_: (i, 0))]
    shapes = [jax.ShapeDtypeStruct((m, n), F32), jax.ShapeDtypeStruct((m, n), BF16),
              jax.ShapeDtypeStruct((m, LANES), F32)]
    return specs, shapes


def _out_proj_body(a1_ref, a2_ref, w1_ref, w2_ref, r_ref, g_ref, o_ref, hb_ref, ss_ref):
    acc = jnp.dot(a1_ref[...], w1_ref[...], preferred_element_type=F32)
    acc = acc + jnp.dot(a2_ref[...], w2_ref[...], preferred_element_type=F32)
    h = r_ref[...] + acc
    o_ref[...] = h
    _emit_norm_inputs(h, g_ref, hb_ref, ss_ref, pl.program_id(1) == 0)


def out_proj_residual(a1, a2, w, res, gain, tm=1024, tn=512):
    m, k1 = a1.shape
    k2 = a2.shape[1]
    assert k1 == k2 and w.shape[0] == k1 + k2
    n = w.shape[1]
    tile = lambda i, j: (i, j)
    out_specs, out_shape = _norm_outputs(m, n, tm, tn, tile)
    return pl.pallas_call(
        _out_proj_body,
        grid=(m // tm, n // tn),
        in_specs=[pl.BlockSpec((tm, k1), lambda i, j: (i, 0)), pl.BlockSpec((tm, k2), lambda i, j: (i, 0)),
                  pl.BlockSpec((k1, tn), lambda i, j: (0, j)), pl.BlockSpec((k2, tn), lambda i, j: (1, j)),
                  pl.BlockSpec((tm, tn), tile), pl.BlockSpec((1, tn), lambda i, j: (0, j))],
        out_specs=out_specs,
        out_shape=out_shape,
        compiler_params=_params("parallel", "arbitrary"),
        name="out_proj",
    )(a1, a2, w, w, res, gain.reshape(1, n))


def _mlp_down_body(a_ref, w_ref, r_ref, g_ref, o_ref, hb_ref, ss_ref, acc_ref):
    k = pl.program_id(2)
    last = pl.num_programs(2) - 1

    def partial_product():
        return jnp.dot(a_ref[...], w_ref[...], preferred_element_type=F32)

    @pl.when(k == 0)
    def _():
        acc_ref[...] = r_ref[...] + partial_product()

    @pl.when(jnp.logical_and(k > 0, k < last))
    def _():
        acc_ref[...] += partial_product()

    @pl.when(k == last)
    def _():
        h = acc_ref[...] + partial_product()
        o_ref[...] = h
        _emit_norm_inputs(h, g_ref, hb_ref, ss_ref, pl.program_id(1) == 0)


def mlp_down_residual(a, w, res, gain, tm=1024, tn=1024, tk=2048):
    m, k = a.shape
    n = w.shape[1]
    assert k // tk >= 2
    tile = lambda i, j, kk: (i, j)
    out_specs, out_shape = _norm_outputs(m, n, tm, tn, tile)
    return pl.pallas_call(
        _mlp_down_body,
        grid=(m // tm, n // tn, k // tk),
        in_specs=[pl.BlockSpec((tm, tk), lambda i, j, kk: (i, kk)), pl.BlockSpec((tk, tn), lambda i, j, kk: (kk, j)),
                  pl.BlockSpec((tm, tn), tile), pl.BlockSpec((1, tn), lambda i, j, kk: (0, j))],
        out_specs=out_specs,
        out_shape=out_shape,
        scratch_shapes=[pltpu.VMEM((tm, tn), F32)],
        compiler_params=_params("parallel", "arbitrary", "arbitrary"),
        name="mlp_down",
    )(a, w, res, gain.reshape(1, n))


def _gate_body(a_ref, w_ref, ss_ref, h_ref, e_ref, o_ref):
    tm, d = a_ref.shape
    half = tm // 2
    for rows in (slice(0, half), slice(half, tm)):
        scale = lax.rsqrt(ss_ref[rows, :1] / d + EPS)
        acc = jnp.dot(a_ref[rows, :], w_ref[...], preferred_element_type=F32) * scale
        o_ref[rows, :] = h_ref[rows, :] + jax.nn.sigmoid(acc) * e_ref[rows, :]


def gate_residual(hb, ss, w, h, e, tm=1024, tn=512):
    m, k = hb.shape
    n = w.shape[1]
    return pl.pallas_call(
        _gate_body,
        grid=(m // tm, n // tn),
        in_specs=[pl.BlockSpec((tm, k), lambda i, j: (i, 0)), pl.BlockSpec((k, tn), lambda i, j: (0, j)),
                  pl.BlockSpec((tm, LANES), lambda i, j: (i, 0)),
                  pl.BlockSpec((tm, tn), lambda i, j: (i, j)), pl.BlockSpec((tm, tn), lambda i, j: (i, j))],
        out_specs=pl.BlockSpec((tm, tn), lambda i, j: (i, j)),
        out_shape=jax.ShapeDtypeStruct((m, n), F32),
        compiler_params=_params("parallel", "parallel"),
        name="ple_gate",
    )(hb, w, ss, h, e)


def _rmsnorm_body(x_ref, g_ref, o_ref):
    o_ref[...] = _rms(x_ref[...], g_ref[...]).astype(o_ref.dtype)


def rmsnorm(x, g, out_dtype, tm=256):
    n, d = x.shape
    return pl.pallas_call(
        _rmsnorm_body,
        grid=(n // tm,),
        in_specs=[pl.BlockSpec((tm, d), lambda i: (i, 0)), pl.BlockSpec((1, d), lambda i: (0, 0))],
        out_specs=pl.BlockSpec((tm, d), lambda i: (i, 0)),
        out_shape=jax.ShapeDtypeStruct((n, d), out_dtype),
        compiler_params=_params("parallel"),
        name="rmsnorm",
    )(x, g.reshape(1, d))


def _embed_body(p_ref, w_ref, g_ref, o_ref):
    y = jnp.dot(p_ref[...].astype(BF16), w_ref[...], preferred_element_type=F32)
    o_ref[...] = _rms(y, g_ref[...])


def embed_norm(p, w, g, tm=256):
    m, k = p.shape
    n = w.shape[1]
    return pl.pallas_call(
        _embed_body,
        grid=(m // tm,),
        in_specs=[pl.BlockSpec((tm, k), lambda i: (i, 0)), pl.BlockSpec((k, n), lambda i: (0, 0)),
                  pl.BlockSpec((1, n), lambda i: (0, 0))],
        out_specs=pl.BlockSpec((tm, n), lambda i: (i, 0)),
        out_shape=jax.ShapeDtypeStruct((m, n), F32),
        compiler_params=_params("parallel"),
        name="ple_embed",
    )(p, w, g.reshape(1, n))


def _mla_prep_body(pa_ref, pos_ref, gq_ref, gkv_ref, wq_ref, wk_ref, wvt_ref, ones_ref, freq_ref, sign_ref, q_ref, k_ref,
                   vt_ref):
    pa = pa_ref[...]
    cq = _rms(pa[:, :Q_LORA], gq_ref[...]).astype(BF16)
    ckv = _rms(pa[:, Q_LORA:Q_LORA + KV_LORA], gkv_ref[...]).astype(BF16)
    kr = pa[:, Q_LORA + KV_LORA:Q_LORA + KV_LORA + LANES]
    krs = pa[:, Q_LORA + KV_LORA + LANES:]
    ang = pos_ref[...].astype(F32) * freq_ref[...]
    cos = jnp.cos(ang)
    sin = jnp.sin(ang) * sign_ref[...]
    q = jnp.dot(cq, wq_ref[...], preferred_element_type=F32)
    kn = jnp.dot(ckv, wk_ref[...], preferred_element_type=F32)
    vt_ref[...] = (lax.dot_general(wvt_ref[...], ckv, TRANS_B, preferred_element_type=F32) + ones_ref[...]).astype(BF16)
    krot = (kr * cos + krs * sin).astype(BF16)
    qscale = (QK_HEAD ** -0.5) * LOG2_E
    hn = MLA_HEADS * LANES
    for h in range(MLA_HEADS):
        lo, hi = h * LANES, (h + 1) * LANES
        qrot = q[:, hn + lo:hn + hi] * cos + q[:, 2 * hn + lo:2 * hn + hi] * sin
        q_ref[:, h * QK_PAD:h * QK_PAD + LANES] = (q[:, lo:hi] * qscale).astype(BF16)
        q_ref[:, h * QK_PAD + LANES:(h + 1) * QK_PAD] = (qrot * qscale).astype(BF16)
        k_ref[:, h * QK_PAD:h * QK_PAD + LANES] = kn[:, lo:hi].astype(BF16)
        k_ref[:, h * QK_PAD + LANES:(h + 1) * QK_PAD] = krot


def mla_prep(proj, pos, gq, gkv, wq, wk, wvt, freq, sign, tm=256):
    n = proj.shape[0]
    const = lambda i: (0, 0)
    ones_rows = np.zeros((MLA_HEADS, VT_ROWS, 1), np.float32)
    ones_rows[:, V_HEAD, 0] = 1.0
    ones_rows = jnp.asarray(ones_rows.reshape(MLA_HEADS * VT_ROWS, 1))
    return pl.pallas_call(
        _mla_prep_body,
        grid=(n // tm,),
        in_specs=[pl.BlockSpec((tm, A_WIDTH), lambda i: (i, 0)), pl.BlockSpec((tm, 1), lambda i: (i, 0)),
                  pl.BlockSpec((1, Q_LORA), const), pl.BlockSpec((1, KV_LORA), const),
                  pl.BlockSpec(wq.shape, const), pl.BlockSpec(wk.shape, const), pl.BlockSpec(wvt.shape, const),
                  pl.BlockSpec(ones_rows.shape, const), pl.BlockSpec((1, LANES), const), pl.BlockSpec((1, LANES), const)],
        out_specs=[pl.BlockSpec((tm, MLA_HEADS * QK_PAD), lambda i: (i, 0)),
                   pl.BlockSpec((tm, MLA_HEADS * QK_PAD), lambda i: (i, 0)),
                   pl.BlockSpec((MLA_HEADS * VT_ROWS, tm), lambda i: (0, i))],
        out_shape=[jax.ShapeDtypeStruct((n, MLA_HEADS * QK_PAD), BF16),
                   jax.ShapeDtypeStruct((n, MLA_HEADS * QK_PAD), BF16),
                   jax.ShapeDtypeStruct((MLA_HEADS * VT_ROWS, n), BF16)],
        compiler_params=_params("parallel"),
        name="mla_prep",
    )(proj, pos, gq.reshape(1, -1), gkv.reshape(1, -1), wq, wk, wvt, ones_rows, freq, sign)


def _attn_body(q_ref, k_ref, vt_ref, *refs, tq, tk, n_cast):
    o_ref = refs[n_cast]
    s_ref, mx_ref, m_ref, acc_ref = refs[2 * n_cast + 1:]
    _cast_slabs(refs[:n_cast], refs[n_cast + 1:2 * n_cast + 1])
    seq = q_ref.shape[0]
    n_diag = tq // tk
    gw = 2 * LANES
    tiles = [(qi, j) for qi in range(seq // tq) for j in range((qi + 1) * n_diag)]

    def keys_needed(qi, j, c):
        diag = j - qi * n_diag
        return tk if diag < 0 else max(0, min(tk, c + gw - diag * tk))

    def produce(slot, qi, j):
        for c in range(0, tq, gw):
            nk = keys_needed(qi, j, c)
            if nk == 0:
                continue
            q = q_ref[qi * tq + c:qi * tq + c + gw, :]
            s = lax.dot_general(k_ref[j * tk:j * tk + nk, :], q, (((1,), (1,)), ((), ())),
                                preferred_element_type=F32)
            s_ref[slot, :nk, c:c + gw] = s
            mx_ref[slot, :, c:c + gw] = jnp.max(s, axis=0, keepdims=True)

    def consume(slot, qi, j):
        for c in range(0, tq, gw):
            cols = slice(c, c + gw)
            nk = keys_needed(qi, j, c)
            if nk == 0:
                continue
            s = s_ref[slot, :nk, cols]
            first_key = (j - qi * n_diag) * tk
            if first_key + nk - 1 <= c:
                tile_max = mx_ref[slot, :, cols]
            else:
                key = first_key + lax.broadcasted_iota(jnp.int32, (nk, gw), 0)
                qry = c + lax.broadcasted_iota(jnp.int32, (nk, gw), 1)
                s = jnp.where(key <= qry, s, -jnp.inf)
                tile_max = jnp.max(s, axis=0, keepdims=True)
            vt = vt_ref[:, j * tk:j * tk + nk]
            if j == 0:
                m_new = tile_max
                p = jnp.exp2(s - m_new)
                acc_ref[:, cols] = jnp.dot(vt, p.astype(BF16), preferred_element_type=F32)
            else:
                m = m_ref[:, cols]
                m_new = jnp.maximum(m, tile_max)
                alpha = jnp.exp2(m - m_new)
                p = jnp.exp2(s - m_new)
                acc_ref[:, cols] = alpha * acc_ref[:, cols] + jnp.dot(vt, p.astype(BF16), preferred_element_type=F32)
            m_ref[:, cols] = m_new

    produce(0, *tiles[0])
    for t, (qi, j) in enumerate(tiles):
        if t + 1 < len(tiles):
            produce((t + 1) % 2, *tiles[t + 1])
        consume(t % 2, qi, j)
        if j == (qi + 1) * n_diag - 1:
            acc = acc_ref[...]
            o_ref[qi * tq:(qi + 1) * tq, :] = (acc[:V_HEAD] / acc[V_HEAD:V_HEAD + 1]).T.astype(o_ref.dtype)


def causal_attention(q, k, vt, batch, seq, tq=1024, tk=512, cast=()):
    cast_specs, cast_shapes = _cast_specs(cast, batch * MLA_HEADS, lambda b, h: b * MLA_HEADS + h)
    outs = pl.pallas_call(
        functools.partial(_attn_body, tq=tq, tk=tk, n_cast=len(cast)),
        grid=(batch, MLA_HEADS),
        in_specs=[pl.BlockSpec((seq, QK_PAD), lambda b, h: (b, h)), pl.BlockSpec((seq, QK_PAD), lambda b, h: (b, h)),
                  pl.BlockSpec((VT_ROWS, seq), lambda b, h: (h, b))] + cast_specs,
        out_specs=[pl.BlockSpec((seq, V_HEAD), lambda b, h: (b, h))] + cast_specs,
        out_shape=[jax.ShapeDtypeStruct((batch * seq, MLA_HEADS * V_HEAD), BF16)] + cast_shapes,
        scratch_shapes=[pltpu.VMEM((2, tk, tq), F32), pltpu.VMEM((2, 1, tq), F32), pltpu.VMEM((1, tq), F32),
                        pltpu.VMEM((VT_ROWS, tq), F32)],
        compiler_params=_params("parallel", "parallel"),
        name="mla_attention",
    )(q, k, vt, *cast)
    return outs[0], outs[1:]


def _hgrn_constants(c):
    t = np.arange(c)
    ltri = (t[None, :] <= t[:, None]).astype(np.float32)
    nlev = int(np.log2(c))
    x = t[:, None] ^ t[None, :]
    level = np.full((c, c), -1, np.int32)
    lower = t[:, None] > t[None, :]
    level[lower] = (nlev - 1) - np.floor(np.log2(x[lower])).astype(np.int32)
    level[t, t] = nlev
    return np.concatenate([ltri, ltri, ltri], axis=1), level, nlev


def _level_exponent(b, half):
    c, dk = b.shape
    if half >= 8:
        x = b.reshape(c // (2 * half), 2 * half, dk)
        ref = x[:, half - 1:half, :]
        return jnp.concatenate([ref - x[:, :half, :], x[:, half:, :] - ref], axis=1).reshape(c, dk)
    x = b.reshape(c // 8, 8, dk)
    if half == 4:
        ref = jnp.broadcast_to(x[:, 3:4, :], x.shape)
    else:
        assert half == 2
        sub = lax.broadcasted_iota(jnp.int32, x.shape, 1)
        ref = jnp.where(sub < 4, x[:, 1:2, :], x[:, 5:6, :])
    return (-jnp.abs(x - ref)).reshape(c, dk)


def _hgrn_body(hq_ref, hf_ref, hi_ref, hg_ref, lbp_ref, gn_ref, w3_ref, lev_ref, o_ref, st_ref, *, layer, tile, chunk, nlev):
    @pl.when(pl.program_id(2) == 0)
    def _():
        st_ref[...] = jnp.zeros_like(st_ref)

    hb = lbp_ref[...]
    ex = jnp.exp(hb - jnp.max(hb, axis=0, keepdims=True))
    sm = ex / jnp.sum(ex, axis=0, keepdims=True)
    lb = jnp.sum(sm[:layer + 1], axis=0, keepdims=True)
    oml = 1.0 - lb
    lev = lev_ref[...]
    gn = gn_ref[...]
    trans_b = (((1,), (1,)), ((), ()))
    chunks = [slice(c * chunk, (c + 1) * chunk) for c in range(tile // chunk)]
    fs, qs, kks, vs, bs = [], [], [], [], []
    for rows in chunks:
        z = hf_ref[rows, :]
        f = lb + oml * jax.nn.sigmoid(z)
        g = jnp.log(f)
        hq = hq_ref[rows, :]
        fs.append(f)
        kks.append(oml * jax.nn.sigmoid(-z))
        qs.append(hq * jax.nn.sigmoid(hq))
        vs.append(hi_ref[rows, :].astype(BF16))
        g_hi = g.astype(BF16)
        r1 = g - g_hi.astype(F32)
        g_mid = r1.astype(BF16)
        g_lo = (r1 - g_mid.astype(F32)).astype(BF16)
        g3 = jnp.concatenate([g_hi, g_mid, g_lo], axis=0)
        bs.append(jnp.dot(w3_ref[...], g3, preferred_element_type=F32))
    scores = [jnp.zeros((chunk, chunk), F32) for _ in chunks]
    q16 = [q.astype(BF16) for q in qs]
    k16 = [kk.astype(BF16) for kk in kks]
    for lv in range(nlev + 1):
        half = chunk >> (lv + 1)
        for c, (f, q, kk, b) in enumerate(zip(fs, q16, k16, bs)):
            if half >= 2:
                e = jnp.exp(_level_exponent(b, half)).astype(BF16)
                ql, kl = q * e, kk * e
            elif half == 1:
                ql, kl = q * f.astype(BF16), kk
            else:
                ql, kl = q, kk
            p = lax.dot_general(ql, kl, trans_b, preferred_element_type=F32)
            scores[c] = jnp.where(lev == lv, p, scores[c])
    intra, updates, decays, qbs = [], [], [], []
    for a, q, kk, v, b in zip(scores, qs, kks, vs, bs):
        intra.append(jnp.dot(a.astype(BF16), v, preferred_element_type=F32))
        b_last = b[chunk - 1:chunk, :]
        kh = (kk * jnp.exp(b_last - b)).astype(BF16)
        updates.append(lax.dot_general(v, kh, (((0,), (0,)), ((), ())), preferred_element_type=F32))
        decays.append(jnp.exp(b_last))
        qbs.append((q * jnp.exp(b)).astype(BF16))
    state = st_ref[...]
    for rows, qb, o_intra, upd, dec in zip(chunks, qbs, intra, updates, decays):
        o = lax.dot_general(qb, state.astype(BF16), trans_b, preferred_element_type=F32) + o_intra
        state = dec * state + upd
        hg = hg_ref[rows, :]
        y = _rms(o, gn) * (hg * jax.nn.sigmoid(hg))
        o_ref[rows, :] = y.astype(o_ref.dtype)
    st_ref[...] = state


def hgrn2(proj, col0, lbp, gn, batch, seq, layer, tile=2048):
    w3, level, nlev = _hgrn_constants(HG_CHUNK)
    w3 = jnp.asarray(w3, BF16)
    level = jnp.asarray(level)
    nt = seq // tile
    cb = col0 // LANES

    def col(group):
        return lambda b, h, i: (b * nt + i, cb + group * HG_HEADS + h)

    return pl.pallas_call(
        functools.partial(_hgrn_body, layer=layer, tile=tile, chunk=HG_CHUNK, nlev=nlev),
        grid=(batch, HG_HEADS, nt),
        in_specs=[pl.BlockSpec((tile, LANES), col(0)), pl.BlockSpec((tile, LANES), col(1)),
                  pl.BlockSpec((tile, LANES), col(2)), pl.BlockSpec((tile, LANES), col(3)),
                  pl.BlockSpec((lbp.shape[0], LANES), lambda b, h, i: (0, h)),
                  pl.BlockSpec((1, LANES), lambda b, h, i: (0, h)),
                  pl.BlockSpec(w3.shape, lambda b, h, i: (0, 0)),
                  pl.BlockSpec(level.shape, lambda b, h, i: (0, 0))],
        out_specs=pl.BlockSpec((tile, LANES), lambda b, h, i: (b * nt + i, h)),
        out_shape=jax.ShapeDtypeStruct((batch * seq, HG_VDIM), BF16),
        scratch_shapes=[pltpu.VMEM((HG_DV, HG_DK), F32)],
        compiler_params=_params("parallel", "parallel", "arbitrary"),
        name="hgrn2",
    )(proj, proj, proj, proj, lbp, gn.reshape(1, -1), w3, level)


def _swap_halves(w):
    half = w.shape[-1] // 2
    return jnp.concatenate([w[..., half:], w[..., :half]], axis=-1)


def _mla_in_weight_body(wt_ref, o_ref):
    n_a = Q_LORA + KV_LORA
    half = QK_ROPE // 2
    top = wt_ref[...].astype(BF16)
    o_ref[...] = jnp.zeros_like(o_ref)
    o_ref[:n_a + QK_ROPE, :] = top
    o_ref[n_a + LANES:n_a + LANES + half, :] = top[n_a + half:]
    o_ref[n_a + LANES + half:n_a + LANES + QK_ROPE, :] = top[n_a:n_a + half]


def mla_in_weight(w_in_t, tn=512):
    d = w_in_t.shape[1]
    rows = Q_LORA + KV_LORA + QK_ROPE
    return pl.pallas_call(
        _mla_in_weight_body,
        grid=(d // tn,),
        in_specs=[pl.BlockSpec((rows, tn), lambda j: (0, j))],
        out_specs=pl.BlockSpec((A_WIDTH, tn), lambda j: (0, j)),
        out_shape=jax.ShapeDtypeStruct((A_WIDTH, d), BF16),
        compiler_params=_params("parallel"),
        name="mla_in_weight",
    )(w_in_t)


def _layer_weights(w_uq, w_ukv):
    wq = w_uq.reshape(Q_LORA, MLA_HEADS, QK_HEAD)
    rope = wq[:, :, QK_NOPE:]
    pad = jnp.zeros((Q_LORA, MLA_HEADS, LANES - QK_ROPE), w_uq.dtype)
    wq_all = jnp.concatenate([
        wq[:, :, :QK_NOPE].reshape(Q_LORA, -1),
        jnp.concatenate([rope, pad], axis=-1).reshape(Q_LORA, -1),
        jnp.concatenate([_swap_halves(rope), pad], axis=-1).reshape(Q_LORA, -1)], axis=1)
    wkv = w_ukv.reshape(KV_LORA, MLA_HEADS, QK_NOPE + V_HEAD)
    wk = wkv[:, :, :QK_NOPE].reshape(KV_LORA, -1)
    wv = jnp.pad(wkv[:, :, QK_NOPE:], ((0, 0), (0, 0), (0, VT_ROWS - V_HEAD)))
    wvt = wv.reshape(KV_LORA, -1).T
    return wq_all.astype(BF16), wk.astype(BF16), wvt.astype(BF16)


def _rope_rows():
    inv_freq = ROPE_THETA ** (-jnp.arange(0, QK_ROPE, 2, dtype=F32) / QK_ROPE)
    zero = jnp.zeros((LANES - QK_ROPE,), F32)
    freq = jnp.concatenate([inv_freq, inv_freq, zero]).reshape(1, LANES)
    half = jnp.ones((QK_ROPE // 2,), F32)
    sign = jnp.concatenate([-half, half, zero]).reshape(1, LANES)
    return freq, sign


def kernel(x, p, positions, norm_mix, w_in, q_a_norm, kv_a_norm, w_uq, w_ukv, hg_lower_bound, hg_out_norm, w_o,
           norm_mlp, w_up, w_down, norm_ple, w_ple_gate, w_ple, ple_post_norm, final_norm):
    batch, seq, d_model = x.shape
    n = batch * seq
    depth = w_in.shape[0]
    h = x.reshape(n, d_model)
    pos = positions.reshape(n, 1)
    freq, sign = _rope_rows()
    for i in range(depth):
        w_in_t = w_in[i].T
        wq_all, wk, wvt = _layer_weights(w_uq[i], w_ukv[i])
        proj_a, xb, ss, w_ht = front(h, norm_mix[i], mla_in_weight(w_in_t), w_in_t, Q_LORA + KV_LORA + QK_ROPE)
        proj_h, (w_up_b,) = scaled_matmul(xb, ss, w_ht, F32, "in_proj_h", w_transposed=True, tn=1024,
                                          cast=(w_up[i],))
        q, k, vt = mla_prep(proj_a, pos, q_a_norm[i], kv_a_norm[i], wq_all, wk, wvt, freq, sign)
        o_mla, (w_o_b, w_pg_b) = causal_attention(q, k, vt, batch, seq, cast=(w_o[i], w_ple_gate[i]))
        o_hg = hgrn2(proj_h, 0, hg_lower_bound, hg_out_norm[i], batch, seq, i)
        h, hb, ss = out_proj_residual(o_mla, o_hg, w_o_b, h, norm_mlp[i])
        hidden, (w_down_b,) = scaled_matmul(hb, ss, w_up_b, BF16, "mlp_up", relu2=True, tn=1024, cast=(w_down[i],))
        h, hb, ss = mlp_down_residual(hidden, w_down_b, h, norm_ple[i])
        e = embed_norm(p[i].reshape(n, -1), w_ple[i].astype(BF16), ple_post_norm[i])
        h = gate_residual(hb, ss, w_pg_b, h, e)
    return rmsnorm(h, final_norm, x.dtype).reshape(batch, seq, d_model)
```

```python
import functools

import numpy as np
import jax
import jax.numpy as jnp
from jax import lax
from jax.experimental import pallas as pl
from jax.experimental.pallas import tpu as pltpu

EPS = 1e-6
MLA_HEADS = 16
QK_NOPE = 128
QK_ROPE = 64
QK_HEAD = QK_NOPE + QK_ROPE
V_HEAD = 128
Q_LORA = 768
KV_LORA = 512
ROPE_THETA = 10000.0
HG_HEADS = 16
HG_DK = 128
HG_DV = 128
HG_FDIM = HG_HEADS * HG_DK
HG_VDIM = HG_HEADS * HG_DV

LANES = 128
QK_PAD = 2 * LANES
A_WIDTH = Q_LORA + KV_LORA + 2 * LANES
HG_CHUNK = 128
VMEM_LIMIT_BYTES = 56 * 1024 * 1024
LOG2_E = 1.4426950408889634
VT_ROWS = V_HEAD + 16

F32 = jnp.float32
BF16 = jnp.bfloat16


def _params(*semantics):
    return pltpu.CompilerParams(dimension_semantics=semantics, vmem_limit_bytes=VMEM_LIMIT_BYTES)


def _rms(x, g):
    return x * lax.rsqrt(jnp.mean(x * x, axis=-1, keepdims=True) + EPS) * g


def _emit_norm_inputs(h, g_ref, hb_ref, ss_ref, first):
    hb_ref[...] = (h * g_ref[...]).astype(BF16)

    @pl.when(first)
    def _():
        ss_ref[...] = jnp.zeros_like(ss_ref)

    ss_ref[...] += jnp.sum(h * h, axis=-1, keepdims=True)


def _row_scale(ss_ref, d):
    return lax.rsqrt(ss_ref[:, :1] / d + EPS)


def _cast_specs(ws, n_steps, step_of):
    specs, shapes = [], []
    for w in ws:
        rows = w.shape[0] // n_steps
        assert rows * n_steps == w.shape[0] and rows % 16 == 0
        specs.append(pl.BlockSpec((rows, w.shape[1]), lambda *ids: (step_of(*ids), 0)))
        shapes.append(jax.ShapeDtypeStruct(w.shape, BF16))
    return specs, shapes


def _cast_slabs(src_refs, dst_refs):
    for src, dst in zip(src_refs, dst_refs):
        dst[...] = src[...].astype(dst.dtype)


TRANS_B = (((1,), (1,)), ((), ()))


def _scaled_mm_body(a_ref, w_ref, ss_ref, *refs, relu2, w_transposed):
    n_cast = len(refs) // 2
    o_ref = refs[n_cast]
    if w_transposed:
        acc = lax.dot_general(a_ref[...], w_ref[...], TRANS_B, preferred_element_type=F32)
    else:
        acc = jnp.dot(a_ref[...], w_ref[...], preferred_element_type=F32)
    acc = acc * _row_scale(ss_ref, a_ref.shape[1])
    if relu2:
        acc = jnp.square(jnp.maximum(acc, 0.0))
    o_ref[...] = acc.astype(o_ref.dtype)
    _cast_slabs(refs[:n_cast], refs[n_cast + 1:])


def scaled_matmul(hb, ss, w, out_dtype, name, relu2=False, w_transposed=False, tm=1024, tn=512, cast=()):
    m, k = hb.shape
    n = w.shape[0] if w_transposed else w.shape[1]
    nj = n // tn
    cast_specs, cast_shapes = _cast_specs(cast, (m // tm) * nj, lambda i, j: i * nj + j)
    w_spec = pl.BlockSpec((tn, k), lambda i, j: (j, 0)) if w_transposed else pl.BlockSpec((k, tn), lambda i, j: (0, j))
    outs = pl.pallas_call(
        functools.partial(_scaled_mm_body, relu2=relu2, w_transposed=w_transposed),
        grid=(m // tm, nj),
        in_specs=[pl.BlockSpec((tm, k), lambda i, j: (i, 0)), w_spec,
                  pl.BlockSpec((tm, LANES), lambda i, j: (i, 0))] + cast_specs,
        out_specs=[pl.BlockSpec((tm, tn), lambda i, j: (i, j))] + cast_specs,
        out_shape=[jax.ShapeDtypeStruct((m, n), out_dtype)] + cast_shapes,
        compiler_params=_params("parallel", "parallel"),
        name=name,
    )(hb, w, ss, *cast)
    return outs[0], outs[1:]


def _front_body(x_ref, g_ref, wat_ref, wt_lo_ref, wt_hi_ref, pa_ref, xb_ref, ss_ref, wht_ref):
    x = x_ref[...]
    xb = (x * g_ref[...]).astype(BF16)
    xb_ref[...] = xb
    ss = jnp.sum(x * x, axis=-1, keepdims=True)
    ss_ref[...] = jnp.broadcast_to(ss, ss_ref.shape)
    acc = lax.dot_general(xb, wat_ref[...], TRANS_B, preferred_element_type=F32)
    pa_ref[...] = acc * lax.rsqrt(ss / x.shape[1] + EPS)
    half = wt_lo_ref.shape[0]
    wht_ref[:half, :] = wt_lo_ref[...].astype(BF16)
    wht_ref[half:, :] = wt_hi_ref[...].astype(BF16)


def front(x, gain, w_at, w_t, h_row0, tm=256):
    m, d = x.shape
    n = w_at.shape[0]
    steps = m // tm
    h_rows = w_t.shape[0] - h_row0
    half = h_rows // (2 * steps)
    assert 2 * half * steps == h_rows and half % 16 == 0 and h_row0 % half == 0
    first = h_row0 // half
    row = lambda i: (i, 0)
    return pl.pallas_call(
        _front_body,
        grid=(steps,),
        in_specs=[pl.BlockSpec((tm, d), row), pl.BlockSpec((1, d), lambda i: (0, 0)),
                  pl.BlockSpec((n, d), lambda i: (0, 0), pipeline_mode=pl.Buffered(1)),
                  pl.BlockSpec((half, d), lambda i: (first + 2 * i, 0)),
                  pl.BlockSpec((half, d), lambda i: (first + 2 * i + 1, 0))],
        out_specs=[pl.BlockSpec((tm, n), row), pl.BlockSpec((tm, d), row), pl.BlockSpec((tm, LANES), row),
                   pl.BlockSpec((2 * half, d), row)],
        out_shape=[jax.ShapeDtypeStruct((m, n), F32), jax.ShapeDtypeStruct((m, d), BF16),
                   jax.ShapeDtypeStruct((m, LANES), F32), jax.ShapeDtypeStruct((h_rows, d), BF16)],
        compiler_params=_params("parallel"),
        name="front",
    )(x, gain.reshape(1, d), w_at, w_t, w_t)


def _norm_outputs(m, n, tm, tn, index):
    specs = [pl.BlockSpec((tm, tn), index), pl.BlockSpec((tm, tn), index),
             pl.BlockSpec((tm, LANES), lambda i, *_: (i, 0))]
    shapes = [jax.ShapeDtypeStruct((m, n), F32), jax.ShapeDtypeStruct((m, n), BF16),
              jax.ShapeDtypeStruct((m, LANES), F32)]
    return specs, shapes


def _out_proj_body(a1_ref, a2_ref, w1_ref, w2_ref, r_ref, g_ref, o_ref, hb_ref, ss_ref):
    @pl.when(pl.program_id(1) == 0)
    def _():
        ss_ref[...] = jnp.zeros_like(ss_ref)

    tm = o_ref.shape[0]
    half = tm // 2
    for rows in (slice(0, half), slice(half, tm)):
        acc = jnp.dot(a1_ref[rows, :], w1_ref[...], preferred_element_type=F32)
        acc = acc + jnp.dot(a2_ref[rows, :], w2_ref[...], preferred_element_type=F32)
        h = r_ref[rows, :] + acc
        o_ref[rows, :] = h
        hb_ref[rows, :] = (h * g_ref[...]).astype(BF16)
        ss_ref[rows, :] += jnp.sum(h * h, axis=-1, keepdims=True)


def out_proj_residual(a1, a2, w, res, gain, tm=1024, tn=512):
    m, k1 = a1.shape
    k2 = a2.shape[1]
    assert k1 == k2 and w.shape[0] == k1 + k2
    n = w.shape[1]
    tile = lambda i, j: (i, j)
    out_specs, out_shape = _norm_outputs(m, n, tm, tn, tile)
    return pl.pallas_call(
        _out_proj_body,
        grid=(m // tm, n // tn),
        in_specs=[pl.BlockSpec((tm, k1), lambda i, j: (i, 0)), pl.BlockSpec((tm, k2), lambda i, j: (i, 0)),
                  pl.BlockSpec((k1, tn), lambda i, j: (0, j)), pl.BlockSpec((k2, tn), lambda i, j: (1, j)),
                  pl.BlockSpec((tm, tn), tile), pl.BlockSpec((1, tn), lambda i, j: (0, j))],
        out_specs=out_specs,
        out_shape=out_shape,
        compiler_params=_params("parallel", "arbitrary"),
        name="out_proj",
    )(a1, a2, w, w, res, gain.reshape(1, n))


def _mlp_down_body(a_ref, w_ref, r_ref, g_ref, o_ref, hb_ref, ss_ref, acc_ref):
    k = pl.program_id(2)
    last = pl.num_programs(2) - 1

    def partial_product():
        return jnp.dot(a_ref[...], w_ref[...], preferred_element_type=F32)

    @pl.when(k == 0)
    def _():
        acc_ref[...] = r_ref[...] + partial_product()

    @pl.when(jnp.logical_and(k > 0, k < last))
    def _():
        acc_ref[...] += partial_product()

    @pl.when(k == last)
    def _():
        h = acc_ref[...] + partial_product()
        o_ref[...] = h
        _emit_norm_inputs(h, g_ref, hb_ref, ss_ref, pl.program_id(1) == 0)


def mlp_down_residual(a, w, res, gain, tm=1024, tn=1024, tk=2048):
    m, k = a.shape
    n = w.shape[1]
    assert k // tk >= 2
    tile = lambda i, j, kk: (i, j)
    out_specs, out_shape = _norm_outputs(m, n, tm, tn, tile)
    return pl.pallas_call(
        _mlp_down_body,
        grid=(m // tm, n // tn, k // tk),
        in_specs=[pl.BlockSpec((tm, tk), lambda i, j, kk: (i, kk)), pl.BlockSpec((tk, tn), lambda i, j, kk: (kk, j)),
                  pl.BlockSpec((tm, tn), tile), pl.BlockSpec((1, tn), lambda i, j, kk: (0, j))],
        out_specs=out_specs,
        out_shape=out_shape,
        scratch_shapes=[pltpu.VMEM((tm, tn), F32)],
        compiler_params=_params("parallel", "arbitrary", "arbitrary"),
        name="mlp_down",
    )(a, w, res, gain.reshape(1, n))


def _gate_body(a_ref, w_ref, ss_ref, h_ref, e_ref, o_ref):
    tm, d = a_ref.shape
    half = tm // 2
    for rows in (slice(0, half), slice(half, tm)):
        scale = lax.rsqrt(ss_ref[rows, :1] / d + EPS)
        acc = jnp.dot(a_ref[rows, :], w_ref[...], preferred_element_type=F32) * scale
        gate = 0.5 + 0.5 * jnp.tanh(0.5 * acc)
        o_ref[rows, :] = h_ref[rows, :] + gate * e_ref[rows, :]


def gate_residual(hb, ss, w, h, e, tm=1024, tn=512):
    m, k = hb.shape
    n = w.shape[1]
    return pl.pallas_call(
        _gate_body,
        grid=(m // tm, n // tn),
        in_specs=[pl.BlockSpec((tm, k), lambda i, j: (i, 0)), pl.BlockSpec((k, tn), lambda i, j: (0, j)),
                  pl.BlockSpec((tm, LANES), lambda i, j: (i, 0)),
                  pl.BlockSpec((tm, tn), lambda i, j: (i, j)), pl.BlockSpec((tm, tn), lambda i, j: (i, j))],
        out_specs=pl.BlockSpec((tm, tn), lambda i, j: (i, j)),
        out_shape=jax.ShapeDtypeStruct((m, n), F32),
        compiler_params=_params("parallel", "parallel"),
        name="ple_gate",
    )(hb, w, ss, h, e)


def _rmsnorm_body(x_ref, g_ref, o_ref):
    o_ref[...] = _rms(x_ref[...], g_ref[...]).astype(o_ref.dtype)


def rmsnorm(x, g, out_dtype, tm=256):
    n, d = x.shape
    return pl.pallas_call(
        _rmsnorm_body,
        grid=(n // tm,),
        in_specs=[pl.BlockSpec((tm, d), lambda i: (i, 0)), pl.BlockSpec((1, d), lambda i: (0, 0))],
        out_specs=pl.BlockSpec((tm, d), lambda i: (i, 0)),
        out_shape=jax.ShapeDtypeStruct((n, d), out_dtype),
        compiler_params=_params("parallel"),
        name="rmsnorm",
    )(x, g.reshape(1, d))


def _embed_body(p_ref, w_ref, g_ref, o_ref):
    y = jnp.dot(p_ref[...].astype(BF16), w_ref[...], preferred_element_type=F32)
    o_ref[...] = _rms(y, g_ref[...])


def embed_norm(p, w, g, tm=256):
    m, k = p.shape
    n = w.shape[1]
    return pl.pallas_call(
        _embed_body,
        grid=(m // tm,),
        in_specs=[pl.BlockSpec((tm, k), lambda i: (i, 0)), pl.BlockSpec((k, n), lambda i: (0, 0)),
                  pl.BlockSpec((1, n), lambda i: (0, 0))],
        out_specs=pl.BlockSpec((tm, n), lambda i: (i, 0)),
        out_shape=jax.ShapeDtypeStruct((m, n), F32),
        compiler_params=_params("parallel"),
        name="ple_embed",
    )(p, w, g.reshape(1, n))


def _mla_prep_body(pa_ref, pos_ref, gq_ref, gkv_ref, wq_ref, wk_ref, wvt_ref, ones_ref, freq_ref, sign_ref, q_ref, k_ref,
                   vt_ref):
    pa = pa_ref[...]
    cq = _rms(pa[:, :Q_LORA], gq_ref[...]).astype(BF16)
    ckv = _rms(pa[:, Q_LORA:Q_LORA + KV_LORA], gkv_ref[...]).astype(BF16)
    kr = pa[:, Q_LORA + KV_LORA:Q_LORA + KV_LORA + LANES]
    krs = pa[:, Q_LORA + KV_LORA + LANES:]
    ang = pos_ref[...].astype(F32) * freq_ref[...]
    cos = jnp.cos(ang)
    sin = jnp.sin(ang) * sign_ref[...]
    q = jnp.dot(cq, wq_ref[...], preferred_element_type=F32)
    kn = jnp.dot(ckv, wk_ref[...], preferred_element_type=F32)
    vt_ref[...] = (lax.dot_general(wvt_ref[...], ckv, TRANS_B, preferred_element_type=F32) + ones_ref[...]).astype(BF16)
    krot = (kr * cos + krs * sin).astype(BF16)
    qscale = (QK_HEAD ** -0.5) * LOG2_E
    hn = MLA_HEADS * LANES
    hr = MLA_HEADS * QK_ROPE
    low = lax.broadcasted_iota(jnp.int32, cos.shape, 1) < QK_ROPE
    for h in range(MLA_HEADS):
        lo, hi = h * LANES, (h + 1) * LANES
        if h % 2 == 0:
            g0 = hn + (h // 2) * LANES
            pair = (q[:, g0:g0 + LANES] * cos + q[:, hr + g0:hr + g0 + LANES] * sin) * qscale
            qrot = pair
        else:
            qrot = pltpu.roll(pair, QK_ROPE, axis=1)
        q_ref[:, h * QK_PAD:h * QK_PAD + LANES] = (q[:, lo:hi] * qscale).astype(BF16)
        q_ref[:, h * QK_PAD + LANES:(h + 1) * QK_PAD] = jnp.where(low, qrot, 0.0).astype(BF16)
        k_ref[:, h * QK_PAD:h * QK_PAD + LANES] = kn[:, lo:hi].astype(BF16)
        k_ref[:, h * QK_PAD + LANES:(h + 1) * QK_PAD] = krot


def mla_prep(proj, pos, gq, gkv, wq, wk, wvt, freq, sign, tm=256):
    n = proj.shape[0]
    const = lambda i: (0, 0)
    ones_rows = np.zeros((MLA_HEADS, VT_ROWS, 1), np.float32)
    ones_rows[:, V_HEAD, 0] = 1.0
    ones_rows = jnp.asarray(ones_rows.reshape(MLA_HEADS * VT_ROWS, 1))
    return pl.pallas_call(
        _mla_prep_body,
        grid=(n // tm,),
        in_specs=[pl.BlockSpec((tm, A_WIDTH), lambda i: (i, 0)), pl.BlockSpec((tm, 1), lambda i: (i, 0)),
                  pl.BlockSpec((1, Q_LORA), const), pl.BlockSpec((1, KV_LORA), const),
                  pl.BlockSpec(wq.shape, const), pl.BlockSpec(wk.shape, const), pl.BlockSpec(wvt.shape, const),
                  pl.BlockSpec(ones_rows.shape, const), pl.BlockSpec((1, LANES), const), pl.BlockSpec((1, LANES), const)],
        out_specs=[pl.BlockSpec((tm, MLA_HEADS * QK_PAD), lambda i: (i, 0)),
                   pl.BlockSpec((tm, MLA_HEADS * QK_PAD), lambda i: (i, 0)),
                   pl.BlockSpec((MLA_HEADS * VT_ROWS, tm), lambda i: (0, i))],
        out_shape=[jax.ShapeDtypeStruct((n, MLA_HEADS * QK_PAD), BF16),
                   jax.ShapeDtypeStruct((n, MLA_HEADS * QK_PAD), BF16),
                   jax.ShapeDtypeStruct((MLA_HEADS * VT_ROWS, n), BF16)],
        compiler_params=_params("parallel"),
        name="mla_prep",
    )(proj, pos, gq.reshape(1, -1), gkv.reshape(1, -1), wq, wk, wvt, ones_rows, freq, sign)


def _attn_body(q_ref, k_ref, vt_ref, *refs, tq, tk, n_cast):
    o_ref = refs[n_cast]
    s_ref, mx_ref, m_ref, acc_ref = refs[2 * n_cast + 1:]
    _cast_slabs(refs[:n_cast], refs[n_cast + 1:2 * n_cast + 1])
    seq = q_ref.shape[0]
    n_diag = tq // tk
    gw = 2 * LANES
    tiles = [(qi, j) for qi in range(seq // tq) for j in range((qi + 1) * n_diag)]

    def keys_needed(qi, j, c):
        diag = j - qi * n_diag
        return tk if diag < 0 else max(0, min(tk, c + gw - diag * tk))

    def produce(slot, qi, j):
        for c in range(0, tq, gw):
            nk = keys_needed(qi, j, c)
            if nk == 0:
                continue
            q = q_ref[qi * tq + c:qi * tq + c + gw, :]
            s = lax.dot_general(k_ref[j * tk:j * tk + nk, :], q, (((1,), (1,)), ((), ())),
                                preferred_element_type=F32)
            s_ref[slot, :nk, c:c + gw] = s
            mx_ref[slot, :, c:c + gw] = jnp.max(s, axis=0, keepdims=True)

    def consume(slot, qi, j):
        for c in range(0, tq, gw):
            cols = slice(c, c + gw)
            nk = keys_needed(qi, j, c)
            if nk == 0:
                continue
            s = s_ref[slot, :nk, cols]
            first_key = (j - qi * n_diag) * tk
            if first_key + nk - 1 <= c:
                tile_max = mx_ref[slot, :, cols]
            else:
                key = first_key + lax.broadcasted_iota(jnp.int32, (nk, gw), 0)
                qry = c + lax.broadcasted_iota(jnp.int32, (nk, gw), 1)
                s = jnp.where(key <= qry, s, -jnp.inf)
                tile_max = jnp.max(s, axis=0, keepdims=True)
            vt = vt_ref[:, j * tk:j * tk + nk]
            if j == 0:
                m_new = tile_max
                p = jnp.exp2(s - m_new)
                acc_ref[:, cols] = jnp.dot(vt, p.astype(BF16), preferred_element_type=F32)
            else:
                m = m_ref[:, cols]
                m_new = jnp.maximum(m, tile_max)
                alpha = jnp.exp2(m - m_new)
                p = jnp.exp2(s - m_new)
                acc_ref[:, cols] = alpha * acc_ref[:, cols] + jnp.dot(vt, p.astype(BF16), preferred_element_type=F32)
            m_ref[:, cols] = m_new

    produce(0, *tiles[0])
    for t, (qi, j) in enumerate(tiles):
        if t + 1 < len(tiles):
            produce((t + 1) % 2, *tiles[t + 1])
        consume(t % 2, qi, j)
        if j == (qi + 1) * n_diag - 1:
            acc = acc_ref[...]
            o_ref[qi * tq:(qi + 1) * tq, :] = (acc[:V_HEAD] / acc[V_HEAD:V_HEAD + 1]).T.astype(o_ref.dtype)


def causal_attention(q, k, vt, batch, seq, tq=1024, tk=512, cast=()):
    cast_specs, cast_shapes = _cast_specs(cast, batch * MLA_HEADS, lambda b, h: b * MLA_HEADS + h)
    outs = pl.pallas_call(
        functools.partial(_attn_body, tq=tq, tk=tk, n_cast=len(cast)),
        grid=(batch, MLA_HEADS),
        in_specs=[pl.BlockSpec((seq, QK_PAD), lambda b, h: (b, h)), pl.BlockSpec((seq, QK_PAD), lambda b, h: (b, h)),
                  pl.BlockSpec((VT_ROWS, seq), lambda b, h: (h, b))] + cast_specs,
        out_specs=[pl.BlockSpec((seq, V_HEAD), lambda b, h: (b, h))] + cast_specs,
        out_shape=[jax.ShapeDtypeStruct((batch * seq, MLA_HEADS * V_HEAD), BF16)] + cast_shapes,
        scratch_shapes=[pltpu.VMEM((2, tk, tq), F32), pltpu.VMEM((2, 1, tq), F32), pltpu.VMEM((1, tq), F32),
                        pltpu.VMEM((VT_ROWS, tq), F32)],
        compiler_params=_params("parallel", "parallel"),
        name="mla_attention",
    )(q, k, vt, *cast)
    return outs[0], outs[1:]


def _hgrn_constants(c):
    t = np.arange(c)
    ltri = (t[None, :] <= t[:, None]).astype(np.float32)
    nlev = int(np.log2(c))
    x = t[:, None] ^ t[None, :]
    level = np.full((c, c), -1, np.int32)
    lower = t[:, None] > t[None, :]
    level[lower] = (nlev - 1) - np.floor(np.log2(x[lower])).astype(np.int32)
    level[t, t] = nlev
    return np.concatenate([ltri, ltri, ltri], axis=1), level, nlev


def _level_exponent(b, half):
    c, dk = b.shape
    if half >= 8:
        x = b.reshape(c // (2 * half), 2 * half, dk)
        ref = x[:, half - 1:half, :]
        return jnp.concatenate([ref - x[:, :half, :], x[:, half:, :] - ref], axis=1).reshape(c, dk)
    x = b.reshape(c // 8, 8, dk)
    if half == 4:
        ref = jnp.broadcast_to(x[:, 3:4, :], x.shape)
    else:
        assert half == 2
        sub = lax.broadcasted_iota(jnp.int32, x.shape, 1)
        ref = jnp.where(sub < 4, x[:, 1:2, :], x[:, 5:6, :])
    return (-jnp.abs(x - ref)).reshape(c, dk)


def _hgrn_body(hq_ref, hf_ref, hi_ref, hg_ref, lbp_ref, gn_ref, w3_ref, lev_ref, o_ref, st_ref, *, layer, tile, chunk, nlev):
    @pl.when(pl.program_id(2) == 0)
    def _():
        st_ref[...] = jnp.zeros_like(st_ref)

    hb = lbp_ref[...]
    ex = jnp.exp(hb - jnp.max(hb, axis=0, keepdims=True))
    sm = ex / jnp.sum(ex, axis=0, keepdims=True)
    lb = jnp.sum(sm[:layer + 1], axis=0, keepdims=True)
    oml = 1.0 - lb
    lev = lev_ref[...]
    gn = gn_ref[...]
    trans_b = (((1,), (1,)), ((), ()))
    chunks = [slice(c * chunk, (c + 1) * chunk) for c in range(tile // chunk)]
    fs, qs, kks, vs, bs = [], [], [], [], []
    for rows in chunks:
        z = hf_ref[rows, :]
        f = lb + oml * jax.nn.sigmoid(z)
        g = jnp.log(f)
        hq = hq_ref[rows, :]
        fs.append(f)
        kks.append(oml * jax.nn.sigmoid(-z))
        qs.append(hq * jax.nn.sigmoid(hq))
        vs.append(hi_ref[rows, :].astype(BF16))
        g_hi = g.astype(BF16)
        r1 = g - g_hi.astype(F32)
        g_mid = r1.astype(BF16)
        g_lo = (r1 - g_mid.astype(F32)).astype(BF16)
        g3 = jnp.concatenate([g_hi, g_mid, g_lo], axis=0)
        bs.append(jnp.dot(w3_ref[...], g3, preferred_element_type=F32))
    scores = [jnp.zeros((chunk, chunk), F32) for _ in chunks]
    q16 = [q.astype(BF16) for q in qs]
    k16 = [kk.astype(BF16) for kk in kks]
    for lv in range(nlev + 1):
        half = chunk >> (lv + 1)
        for c, (f, q, kk, b) in enumerate(zip(fs, q16, k16, bs)):
            if half >= 2:
                e = jnp.exp(_level_exponent(b, half)).astype(BF16)
                ql, kl = q * e, kk * e
            elif half == 1:
                ql, kl = q * f.astype(BF16), kk
            else:
                ql, kl = q, kk
            p = lax.dot_general(ql, kl, trans_b, preferred_element_type=F32)
            scores[c] = jnp.where(lev == lv, p, scores[c])
    intra, updates, decays, qbs = [], [], [], []
    for a, q, kk, v, b in zip(scores, qs, kks, vs, bs):
        intra.append(jnp.dot(a.astype(BF16), v, preferred_element_type=F32))
        b_last = b[chunk - 1:chunk, :]
        kh = (kk * jnp.exp(b_last - b)).astype(BF16)
        updates.append(lax.dot_general(v, kh, (((0,), (0,)), ((), ())), preferred_element_type=F32))
        decays.append(jnp.exp(b_last))
        qbs.append((q * jnp.exp(b)).astype(BF16))
    state = st_ref[...]
    for rows, qb, o_intra, upd, dec in zip(chunks, qbs, intra, updates, decays):
        o = lax.dot_general(qb, state.astype(BF16), trans_b, preferred_element_type=F32) + o_intra
        state = dec * state + upd
        hg = hg_ref[rows, :]
        y = _rms(o, gn) * (hg * jax.nn.sigmoid(hg))
        o_ref[rows, :] = y.astype(o_ref.dtype)
    st_ref[...] = state


def hgrn2(proj, col0, lbp, gn, batch, seq, layer, tile=2048):
    w3, level, nlev = _hgrn_constants(HG_CHUNK)
    w3 = jnp.asarray(w3, BF16)
    level = jnp.asarray(level)
    nt = seq // tile
    cb = col0 // LANES

    def col(group):
        return lambda b, h, i: (b * nt + i, cb + group * HG_HEADS + h)

    return pl.pallas_call(
        functools.partial(_hgrn_body, layer=layer, tile=tile, chunk=HG_CHUNK, nlev=nlev),
        grid=(batch, HG_HEADS, nt),
        in_specs=[pl.BlockSpec((tile, LANES), col(0)), pl.BlockSpec((tile, LANES), col(1)),
                  pl.BlockSpec((tile, LANES), col(2)), pl.BlockSpec((tile, LANES), col(3)),
                  pl.BlockSpec((lbp.shape[0], LANES), lambda b, h, i: (0, h)),
                  pl.BlockSpec((1, LANES), lambda b, h, i: (0, h)),
                  pl.BlockSpec(w3.shape, lambda b, h, i: (0, 0)),
                  pl.BlockSpec(level.shape, lambda b, h, i: (0, 0))],
        out_specs=pl.BlockSpec((tile, LANES), lambda b, h, i: (b * nt + i, h)),
        out_shape=jax.ShapeDtypeStruct((batch * seq, HG_VDIM), BF16),
        scratch_shapes=[pltpu.VMEM((HG_DV, HG_DK), F32)],
        compiler_params=_params("parallel", "parallel", "arbitrary"),
        name="hgrn2",
    )(proj, proj, proj, proj, lbp, gn.reshape(1, -1), w3, level)


def _swap_halves(w):
    half = w.shape[-1] // 2
    return jnp.concatenate([w[..., half:], w[..., :half]], axis=-1)


def _mla_in_weight_body(wt_ref, o_ref):
    n_a = Q_LORA + KV_LORA
    half = QK_ROPE // 2
    top = wt_ref[...].astype(BF16)
    o_ref[...] = jnp.zeros_like(o_ref)
    o_ref[:n_a + QK_ROPE, :] = top
    o_ref[n_a + LANES:n_a + LANES + half, :] = top[n_a + half:]
    o_ref[n_a + LANES + half:n_a + LANES + QK_ROPE, :] = top[n_a:n_a + half]


def mla_in_weight(w_in_t, tn=512):
    d = w_in_t.shape[1]
    rows = Q_LORA + KV_LORA + QK_ROPE
    return pl.pallas_call(
        _mla_in_weight_body,
        grid=(d // tn,),
        in_specs=[pl.BlockSpec((rows, tn), lambda j: (0, j))],
        out_specs=pl.BlockSpec((A_WIDTH, tn), lambda j: (0, j)),
        out_shape=jax.ShapeDtypeStruct((A_WIDTH, d), BF16),
        compiler_params=_params("parallel"),
        name="mla_in_weight",
    )(w_in_t)


def _layer_weights(w_uq, w_ukv):
    wq = w_uq.reshape(Q_LORA, MLA_HEADS, QK_HEAD)
    rope = wq[:, :, QK_NOPE:]
    wq_all = jnp.concatenate([wq[:, :, :QK_NOPE].reshape(Q_LORA, -1), rope.reshape(Q_LORA, -1),
                              _swap_halves(rope).reshape(Q_LORA, -1)], axis=1)
    wkv = w_ukv.reshape(KV_LORA, MLA_HEADS, QK_NOPE + V_HEAD)
    wk = wkv[:, :, :QK_NOPE].reshape(KV_LORA, -1)
    wv = jnp.pad(wkv[:, :, QK_NOPE:], ((0, 0), (0, 0), (0, VT_ROWS - V_HEAD)))
    wvt = wv.reshape(KV_LORA, -1).T
    return wq_all.astype(BF16), wk.astype(BF16), wvt.astype(BF16)


def _rope_rows():
    inv_freq = ROPE_THETA ** (-jnp.arange(0, QK_ROPE, 2, dtype=F32) / QK_ROPE)
    half = jnp.ones((QK_ROPE // 2,), F32)
    reps = LANES // QK_ROPE
    freq = jnp.concatenate([inv_freq, inv_freq] * reps).reshape(1, LANES)
    sign = jnp.concatenate([-half, half] * reps).reshape(1, LANES)
    return freq, sign


def kernel(x, p, positions, norm_mix, w_in, q_a_norm, kv_a_norm, w_uq, w_ukv, hg_lower_bound, hg_out_norm, w_o,
           norm_mlp, w_up, w_down, norm_ple, w_ple_gate, w_ple, ple_post_norm, final_norm):
    batch, seq, d_model = x.shape
    n = batch * seq
    depth = w_in.shape[0]
    h = x.reshape(n, d_model)
    pos = positions.reshape(n, 1)
    freq, sign = _rope_rows()
    for i in range(depth):
        w_in_t = w_in[i].T
        wq_all, wk, wvt = _layer_weights(w_uq[i], w_ukv[i])
        proj_a, xb, ss, w_ht = front(h, norm_mix[i], mla_in_weight(w_in_t), w_in_t, Q_LORA + KV_LORA + QK_ROPE)
        proj_h, (w_up_b,) = scaled_matmul(xb, ss, w_ht, F32, "in_proj_h", w_transposed=True, tn=1024,
                                          cast=(w_up[i],))
        q, k, vt = mla_prep(proj_a, pos, q_a_norm[i], kv_a_norm[i], wq_all, wk, wvt, freq, sign)
        o_mla, (w_o_b, w_pg_b) = causal_attention(q, k, vt, batch, seq, cast=(w_o[i], w_ple_gate[i]))
        o_hg = hgrn2(proj_h, 0, hg_lower_bound, hg_out_norm[i], batch, seq, i)
        h, hb, ss = out_proj_residual(o_mla, o_hg, w_o_b, h, norm_mlp[i])
        hidden, (w_down_b,) = scaled_matmul(hb, ss, w_up_b, BF16, "mlp_up", relu2=True, tn=1024, cast=(w_down[i],))
        h, hb, ss = mlp_down_residual(hidden, w_down_b, h, norm_ple[i])
        e = embed_norm(p[i].reshape(n, -1), w_ple[i].astype(BF16), ple_post_norm[i])
        h = gate_residual(hb, ss, w_pg_b, h, e)
    return rmsnorm(h, final_norm, x.dtype).reshape(batch, seq, d_model)
```

```python
import functools

import numpy as np
import jax
import jax.numpy as jnp
from jax import lax
from jax.experimental import pallas as pl
from jax.experimental.pallas import tpu as pltpu

EPS = 1e-6
MLA_HEADS = 16
QK_NOPE = 128
QK_ROPE = 64
QK_HEAD = QK_NOPE + QK_ROPE
V_HEAD = 128
Q_LORA = 768
KV_LORA = 512
ROPE_THETA = 10000.0
HG_HEADS = 16
HG_DK = 128
HG_DV = 128
HG_FDIM = HG_HEADS * HG_DK
HG_VDIM = HG_HEADS * HG_DV

LANES = 128
QK_PAD = 2 * LANES
A_WIDTH = Q_LORA + KV_LORA + 2 * LANES
HG_CHUNK = 128
VMEM_LIMIT_BYTES = 56 * 1024 * 1024
LOG2_E = 1.4426950408889634
VT_ROWS = V_HEAD + 16

F32 = jnp.float32
BF16 = jnp.bfloat16


def _params(*semantics):
    return pltpu.CompilerParams(dimension_semantics=semantics, vmem_limit_bytes=VMEM_LIMIT_BYTES)


def _rms(x, g):
    return x * lax.rsqrt(jnp.mean(x * x, axis=-1, keepdims=True) + EPS) * g


def _emit_norm_inputs(h, g_ref, hb_ref, ss_ref, first):
    hb_ref[...] = (h * g_ref[...]).astype(BF16)

    @pl.when(first)
    def _():
        ss_ref[...] = jnp.zeros_like(ss_ref)

    ss_ref[...] += jnp.sum(h * h, axis=-1, keepdims=True)


def _row_scale(ss_ref, d):
    return lax.rsqrt(ss_ref[:, :1] / d + EPS)


def _cast_specs(cast, n_steps, step_of):
    in_specs, out_specs, shapes = [], [], []
    for w, tn in cast:
        k, n = w.shape
        rows = k // n_steps
        assert rows * n_steps == k and rows % 16 == 0 and n % tn == 0
        in_specs.append(pl.BlockSpec((rows, n), lambda *ids: (step_of(*ids), 0)))
        out_specs.append(pl.BlockSpec((n // tn, rows, tn), lambda *ids: (0, step_of(*ids), 0)))
        shapes.append(jax.ShapeDtypeStruct((n // tn, k, tn), BF16))
    return in_specs, out_specs, shapes


def _cast_slabs(src_refs, dst_refs):
    for src, dst in zip(src_refs, dst_refs):
        tn = dst.shape[2]
        for j in range(dst.shape[0]):
            dst[j] = src[:, j * tn:(j + 1) * tn].astype(dst.dtype)


TRANS_B = (((1,), (1,)), ((), ()))


def _scaled_mm_body(a_ref, w_ref, ss_ref, *refs, relu2, w_transposed):
    n_cast = len(refs) // 2
    o_ref = refs[n_cast]
    if w_transposed:
        acc = lax.dot_general(a_ref[...], w_ref[...], TRANS_B, preferred_element_type=F32)
    else:
        acc = jnp.dot(a_ref[...], w_ref[0], preferred_element_type=F32)
    acc = acc * _row_scale(ss_ref, a_ref.shape[1])
    if relu2:
        acc = jnp.square(jnp.maximum(acc, 0.0))
    o_ref[...] = acc.astype(o_ref.dtype)
    _cast_slabs(refs[:n_cast], refs[n_cast + 1:])


def scaled_matmul(hb, ss, w, out_dtype, name, relu2=False, w_transposed=False, tiled_out=False, tm=1024, tn=None, cast=()):
    m, k = hb.shape
    if w_transposed:
        n = w.shape[0]
        w_spec = pl.BlockSpec((tn, k), lambda i, j: (j, 0))
    else:
        nj, _, tn = w.shape
        n = nj * tn
        w_spec = pl.BlockSpec((1, k, tn), lambda i, j: (j, 0, 0))
    nj = n // tn
    cast_in, cast_out, cast_shapes = _cast_specs(cast, (m // tm) * nj, lambda i, j: i * nj + j)
    if tiled_out:
        out_spec = pl.BlockSpec((None, tm, tn), lambda i, j: (j, i, 0))
        out_shape = jax.ShapeDtypeStruct((nj, m, tn), out_dtype)
    else:
        out_spec = pl.BlockSpec((tm, tn), lambda i, j: (i, j))
        out_shape = jax.ShapeDtypeStruct((m, n), out_dtype)
    outs = pl.pallas_call(
        functools.partial(_scaled_mm_body, relu2=relu2, w_transposed=w_transposed),
        grid=(m // tm, nj),
        in_specs=[pl.BlockSpec((tm, k), lambda i, j: (i, 0)), w_spec,
                  pl.BlockSpec((tm, LANES), lambda i, j: (i, 0))] + cast_in,
        out_specs=[out_spec] + cast_out,
        out_shape=[out_shape] + cast_shapes,
        compiler_params=_params("parallel", "parallel"),
        name=name,
    )(hb, w, ss, *[w_ for w_, _ in cast])
    return outs[0], outs[1:]


def _front_body(x_ref, g_ref, wat_ref, wt_lo_ref, wt_hi_ref, pa_ref, xb_ref, ss_ref, wht_ref):
    x = x_ref[...]
    xb = (x * g_ref[...]).astype(BF16)
    xb_ref[...] = xb
    ss = jnp.sum(x * x, axis=-1, keepdims=True)
    ss_ref[...] = jnp.broadcast_to(ss, ss_ref.shape)
    acc = lax.dot_general(xb, wat_ref[...], TRANS_B, preferred_element_type=F32)
    pa_ref[...] = acc * lax.rsqrt(ss / x.shape[1] + EPS)
    half = wt_lo_ref.shape[0]
    wht_ref[:half, :] = wt_lo_ref[...].astype(BF16)
    wht_ref[half:, :] = wt_hi_ref[...].astype(BF16)


def front(x, gain, w_at, w_t, h_row0, tm=256):
    m, d = x.shape
    n = w_at.shape[0]
    steps = m // tm
    h_rows = w_t.shape[0] - h_row0
    half = h_rows // (2 * steps)
    assert 2 * half * steps == h_rows and half % 16 == 0 and h_row0 % half == 0
    first = h_row0 // half
    row = lambda i: (i, 0)
    return pl.pallas_call(
        _front_body,
        grid=(steps,),
        in_specs=[pl.BlockSpec((tm, d), row), pl.BlockSpec((1, d), lambda i: (0, 0)),
                  pl.BlockSpec((n, d), lambda i: (0, 0), pipeline_mode=pl.Buffered(1)),
                  pl.BlockSpec((half, d), lambda i: (first + 2 * i, 0)),
                  pl.BlockSpec((half, d), lambda i: (first + 2 * i + 1, 0))],
        out_specs=[pl.BlockSpec((tm, n), row), pl.BlockSpec((tm, d), row), pl.BlockSpec((tm, LANES), row),
                   pl.BlockSpec((2 * half, d), row)],
        out_shape=[jax.ShapeDtypeStruct((m, n), F32), jax.ShapeDtypeStruct((m, d), BF16),
                   jax.ShapeDtypeStruct((m, LANES), F32), jax.ShapeDtypeStruct((h_rows, d), BF16)],
        compiler_params=_params("parallel"),
        name="front",
    )(x, gain.reshape(1, d), w_at, w_t, w_t)


def _norm_outputs(m, n, tm, tn, index):
    specs = [pl.BlockSpec((tm, tn), index), pl.BlockSpec((tm, tn), index),
             pl.BlockSpec((tm, LANES), lambda i, *_: (i, 0))]
    shapes = [jax.ShapeDtypeStruct((m, n), F32), jax.ShapeDtypeStruct((m, n), BF16),
              jax.ShapeDtypeStruct((m, LANES), F32)]
    return specs, shapes


def _out_proj_body(a1_ref, a2_ref, w1_ref, w2_ref, r_ref, g_ref, o_ref, hb_ref, ss_ref):
    @pl.when(pl.program_id(1) == 0)
    def _():
        ss_ref[...] = jnp.zeros_like(ss_ref)

    tm = o_ref.shape[0]
    half = tm // 2
    for rows in (slice(0, half), slice(half, tm)):
        acc = jnp.dot(a1_ref[rows, :], w1_ref[0], preferred_element_type=F32)
        acc = acc + jnp.dot(a2_ref[rows, :], w2_ref[0], preferred_element_type=F32)
        h = r_ref[rows, :] + acc
        o_ref[rows, :] = h
        hb_ref[rows, :] = (h * g_ref[...]).astype(BF16)
        ss_ref[rows, :] += jnp.sum(h * h, axis=-1, keepdims=True)


def out_proj_residual(a1, a2, w, res, gain, tm=1024):
    m, k1 = a1.shape
    k2 = a2.shape[1]
    nj, k, tn = w.shape
    assert k1 == k2 and k == k1 + k2
    n = nj * tn
    tile = lambda i, j: (i, j)
    out_specs, out_shape = _norm_outputs(m, n, tm, tn, tile)
    return pl.pallas_call(
        _out_proj_body,
        grid=(m // tm, n // tn),
        in_specs=[pl.BlockSpec((tm, k1), lambda i, j: (i, 0)), pl.BlockSpec((tm, k2), lambda i, j: (i, 0)),
                  pl.BlockSpec((1, k1, tn), lambda i, j: (j, 0, 0)), pl.BlockSpec((1, k2, tn), lambda i, j: (j, 1, 0)),
                  pl.BlockSpec((tm, tn), tile), pl.BlockSpec((1, tn), lambda i, j: (0, j))],
        out_specs=out_specs,
        out_shape=out_shape,
        compiler_params=_params("parallel", "arbitrary"),
        name="out_proj",
    )(a1, a2, w, w, res, gain.reshape(1, n))


def _mlp_down_body(a_ref, w_ref, r_ref, g_ref, o_ref, hb_ref, ss_ref, acc_ref):
    k = pl.program_id(2)
    last = pl.num_programs(2) - 1

    def partial_product():
        kc = a_ref.shape[2]
        parts = [jnp.dot(a_ref[c], w_ref[0, c * kc:(c + 1) * kc, :], preferred_element_type=F32)
                 for c in range(a_ref.shape[0])]
        return functools.reduce(lambda x, y: x + y, parts)

    @pl.when(k == 0)
    def _():
        acc_ref[...] = r_ref[...] + partial_product()

    @pl.when(jnp.logical_and(k > 0, k < last))
    def _():
        acc_ref[...] += partial_product()

    @pl.when(k == last)
    def _():
        h = acc_ref[...] + partial_product()
        o_ref[...] = h
        _emit_norm_inputs(h, g_ref, hb_ref, ss_ref, pl.program_id(1) == 0)


def mlp_down_residual(a, w, res, gain, tm=1024, tk=2048):
    kj, m, kc = a.shape
    nj, k, tn = w.shape
    n = nj * tn
    assert kj * kc == k and tk % kc == 0 and k // tk >= 2
    tile = lambda i, j, kk: (i, j)
    out_specs, out_shape = _norm_outputs(m, n, tm, tn, tile)
    return pl.pallas_call(
        _mlp_down_body,
        grid=(m // tm, n // tn, k // tk),
        in_specs=[pl.BlockSpec((tk // kc, tm, kc), lambda i, j, kk: (kk, i, 0)),
                  pl.BlockSpec((1, tk, tn), lambda i, j, kk: (j, kk, 0)),
                  pl.BlockSpec((tm, tn), tile), pl.BlockSpec((1, tn), lambda i, j, kk: (0, j))],
        out_specs=out_specs,
        out_shape=out_shape,
        scratch_shapes=[pltpu.VMEM((tm, tn), F32)],
        compiler_params=_params("parallel", "arbitrary", "arbitrary"),
        name="mlp_down",
    )(a, w, res, gain.reshape(1, n))


def _gate_body(a_ref, w_ref, ss_ref, h_ref, e_ref, o_ref):
    tm, d = a_ref.shape
    half = tm // 2
    for rows in (slice(0, half), slice(half, tm)):
        scale = lax.rsqrt(ss_ref[rows, :1] / d + EPS)
        acc = jnp.dot(a_ref[rows, :], w_ref[0], preferred_element_type=F32) * scale
        gate = 0.5 + 0.5 * jnp.tanh(0.5 * acc)
        o_ref[rows, :] = h_ref[rows, :] + gate * e_ref[rows, :]


def gate_residual(hb, ss, w, h, e, tm=1024):
    m, k = hb.shape
    nj, _, tn = w.shape
    n = nj * tn
    return pl.pallas_call(
        _gate_body,
        grid=(m // tm, n // tn),
        in_specs=[pl.BlockSpec((tm, k), lambda i, j: (i, 0)), pl.BlockSpec((1, k, tn), lambda i, j: (j, 0, 0)),
                  pl.BlockSpec((tm, LANES), lambda i, j: (i, 0)),
                  pl.BlockSpec((tm, tn), lambda i, j: (i, j)), pl.BlockSpec((tm, tn), lambda i, j: (i, j))],
        out_specs=pl.BlockSpec((tm, tn), lambda i, j: (i, j)),
        out_shape=jax.ShapeDtypeStruct((m, n), F32),
        compiler_params=_params("parallel", "parallel"),
        name="ple_gate",
    )(hb, w, ss, h, e)


def _rmsnorm_body(x_ref, g_ref, o_ref):
    o_ref[...] = _rms(x_ref[...], g_ref[...]).astype(o_ref.dtype)


def rmsnorm(x, g, out_dtype, tm=256):
    n, d = x.shape
    return pl.pallas_call(
        _rmsnorm_body,
        grid=(n // tm,),
        in_specs=[pl.BlockSpec((tm, d), lambda i: (i, 0)), pl.BlockSpec((1, d), lambda i: (0, 0))],
        out_specs=pl.BlockSpec((tm, d), lambda i: (i, 0)),
        out_shape=jax.ShapeDtypeStruct((n, d), out_dtype),
        compiler_params=_params("parallel"),
        name="rmsnorm",
    )(x, g.reshape(1, d))


def _embed_body(p_ref, w_ref, g_ref, o_ref):
    y = jnp.dot(p_ref[...].astype(BF16), w_ref[...], preferred_element_type=F32)
    o_ref[...] = _rms(y, g_ref[...])


def embed_norm(p, w, g, tm=256):
    m, k = p.shape
    n = w.shape[1]
    return pl.pallas_call(
        _embed_body,
        grid=(m // tm,),
        in_specs=[pl.BlockSpec((tm, k), lambda i: (i, 0)), pl.BlockSpec((k, n), lambda i: (0, 0)),
                  pl.BlockSpec((1, n), lambda i: (0, 0))],
        out_specs=pl.BlockSpec((tm, n), lambda i: (i, 0)),
        out_shape=jax.ShapeDtypeStruct((m, n), F32),
        compiler_params=_params("parallel"),
        name="ple_embed",
    )(p, w, g.reshape(1, n))


def _mla_prep_body(pa_ref, pos_ref, gq_ref, gkv_ref, wq_ref, wk_ref, wvt_ref, ones_ref, freq_ref, sign_ref, q_ref, k_ref,
                   vt_ref):
    pa = pa_ref[...]
    cq = _rms(pa[:, :Q_LORA], gq_ref[...]).astype(BF16)
    ckv = _rms(pa[:, Q_LORA:Q_LORA + KV_LORA], gkv_ref[...]).astype(BF16)
    kr = pa[:, Q_LORA + KV_LORA:Q_LORA + KV_LORA + LANES]
    krs = pa[:, Q_LORA + KV_LORA + LANES:]
    ang = pos_ref[...].astype(F32) * freq_ref[...]
    cos = jnp.cos(ang)
    sin = jnp.sin(ang) * sign_ref[...]
    q = jnp.dot(cq, wq_ref[...], preferred_element_type=F32)
    kn = jnp.dot(ckv, wk_ref[...], preferred_element_type=F32)
    vt_ref[...] = (lax.dot_general(wvt_ref[...], ckv, TRANS_B, preferred_element_type=F32) + ones_ref[...]).astype(BF16)
    krot = (kr * cos + krs * sin).astype(BF16)
    qscale = (QK_HEAD ** -0.5) * LOG2_E
    hn = MLA_HEADS * LANES
    hr = MLA_HEADS * QK_ROPE
    low = lax.broadcasted_iota(jnp.int32, cos.shape, 1) < QK_ROPE
    for h in range(MLA_HEADS):
        lo, hi = h * LANES, (h + 1) * LANES
        if h % 2 == 0:
            g0 = hn + (h // 2) * LANES
            pair = (q[:, g0:g0 + LANES] * cos + q[:, hr + g0:hr + g0 + LANES] * sin) * qscale
            qrot = pair
        else:
            qrot = pltpu.roll(pair, QK_ROPE, axis=1)
        q_ref[:, h * QK_PAD:h * QK_PAD + LANES] = (q[:, lo:hi] * qscale).astype(BF16)
        q_ref[:, h * QK_PAD + LANES:(h + 1) * QK_PAD] = jnp.where(low, qrot, 0.0).astype(BF16)
        k_ref[:, h * QK_PAD:h * QK_PAD + LANES] = kn[:, lo:hi].astype(BF16)
        k_ref[:, h * QK_PAD + LANES:(h + 1) * QK_PAD] = krot


def mla_prep(proj, pos, gq, gkv, wq, wk, wvt, freq, sign, tm=256):
    n = proj.shape[0]
    const = lambda i: (0, 0)
    ones_rows = np.zeros((MLA_HEADS, VT_ROWS, 1), np.float32)
    ones_rows[:, V_HEAD, 0] = 1.0
    ones_rows = jnp.asarray(ones_rows.reshape(MLA_HEADS * VT_ROWS, 1))
    return pl.pallas_call(
        _mla_prep_body,
        grid=(n // tm,),
        in_specs=[pl.BlockSpec((tm, A_WIDTH), lambda i: (i, 0)), pl.BlockSpec((tm, 1), lambda i: (i, 0)),
                  pl.BlockSpec((1, Q_LORA), const), pl.BlockSpec((1, KV_LORA), const),
                  pl.BlockSpec(wq.shape, const), pl.BlockSpec(wk.shape, const), pl.BlockSpec(wvt.shape, const),
                  pl.BlockSpec(ones_rows.shape, const), pl.BlockSpec((1, LANES), const), pl.BlockSpec((1, LANES), const)],
        out_specs=[pl.BlockSpec((tm, MLA_HEADS * QK_PAD), lambda i: (i, 0)),
                   pl.BlockSpec((tm, MLA_HEADS * QK_PAD), lambda i: (i, 0)),
                   pl.BlockSpec((MLA_HEADS * VT_ROWS, tm), lambda i: (0, i))],
        out_shape=[jax.ShapeDtypeStruct((n, MLA_HEADS * QK_PAD), BF16),
                   jax.ShapeDtypeStruct((n, MLA_HEADS * QK_PAD), BF16),
                   jax.ShapeDtypeStruct((MLA_HEADS * VT_ROWS, n), BF16)],
        compiler_params=_params("parallel"),
        name="mla_prep",
    )(proj, pos, gq.reshape(1, -1), gkv.reshape(1, -1), wq, wk, wvt, ones_rows, freq, sign)


def _attn_body(q_ref, k_ref, vt_ref, *refs, tq, tk, n_cast):
    o_ref = refs[n_cast]
    s_ref, mx_ref, m_ref, acc_ref = refs[2 * n_cast + 1:]
    _cast_slabs(refs[:n_cast], refs[n_cast + 1:2 * n_cast + 1])
    seq = q_ref.shape[0]
    n_diag = tq // tk
    gw = 2 * LANES
    tiles = [(qi, j) for qi in range(seq // tq) for j in range((qi + 1) * n_diag)]

    def keys_needed(qi, j, c):
        diag = j - qi * n_diag
        return tk if diag < 0 else max(0, min(tk, c + gw - diag * tk))

    def produce(slot, qi, j):
        for c in range(0, tq, gw):
            nk = keys_needed(qi, j, c)
            if nk == 0:
                continue
            q = q_ref[qi * tq + c:qi * tq + c + gw, :]
            s = lax.dot_general(k_ref[j * tk:j * tk + nk, :], q, (((1,), (1,)), ((), ())),
                                preferred_element_type=F32)
            s_ref[slot, :nk, c:c + gw] = s
            mx_ref[slot, :, c:c + gw] = jnp.max(s, axis=0, keepdims=True)

    def consume(slot, qi, j):
        for c in range(0, tq, gw):
            cols = slice(c, c + gw)
            nk = keys_needed(qi, j, c)
            if nk == 0:
                continue
            s = s_ref[slot, :nk, cols]
            first_key = (j - qi * n_diag) * tk
            if first_key + nk - 1 <= c:
                tile_max = mx_ref[slot, :, cols]
            else:
                key = first_key + lax.broadcasted_iota(jnp.int32, (nk, gw), 0)
                qry = c + lax.broadcasted_iota(jnp.int32, (nk, gw), 1)
                s = jnp.where(key <= qry, s, -jnp.inf)
                tile_max = jnp.max(s, axis=0, keepdims=True)
            vt = vt_ref[:, j * tk:j * tk + nk]
            if j == 0:
                m_new = tile_max
                p = jnp.exp2(s - m_new)
                acc_ref[:, cols] = jnp.dot(vt, p.astype(BF16), preferred_element_type=F32)
            else:
                m = m_ref[:, cols]
                m_new = jnp.maximum(m, tile_max)
                alpha = jnp.exp2(m - m_new)
                p = jnp.exp2(s - m_new)
                acc_ref[:, cols] = alpha * acc_ref[:, cols] + jnp.dot(vt, p.astype(BF16), preferred_element_type=F32)
            m_ref[:, cols] = m_new

    produce(0, *tiles[0])
    for t, (qi, j) in enumerate(tiles):
        if t + 1 < len(tiles):
            produce((t + 1) % 2, *tiles[t + 1])
        consume(t % 2, qi, j)
        if j == (qi + 1) * n_diag - 1:
            acc = acc_ref[...]
            o_ref[qi * tq:(qi + 1) * tq, :] = (acc[:V_HEAD] / acc[V_HEAD:V_HEAD + 1]).T.astype(o_ref.dtype)


def causal_attention(q, k, vt, batch, seq, tq=1024, tk=512, cast=()):
    cast_in, cast_out, cast_shapes = _cast_specs(cast, batch * MLA_HEADS, lambda b, h: b * MLA_HEADS + h)
    outs = pl.pallas_call(
        functools.partial(_attn_body, tq=tq, tk=tk, n_cast=len(cast)),
        grid=(batch, MLA_HEADS),
        in_specs=[pl.BlockSpec((seq, QK_PAD), lambda b, h: (b, h)), pl.BlockSpec((seq, QK_PAD), lambda b, h: (b, h)),
                  pl.BlockSpec((VT_ROWS, seq), lambda b, h: (h, b))] + cast_in,
        out_specs=[pl.BlockSpec((seq, V_HEAD), lambda b, h: (b, h))] + cast_out,
        out_shape=[jax.ShapeDtypeStruct((batch * seq, MLA_HEADS * V_HEAD), BF16)] + cast_shapes,
        scratch_shapes=[pltpu.VMEM((2, tk, tq), F32), pltpu.VMEM((2, 1, tq), F32), pltpu.VMEM((1, tq), F32),
                        pltpu.VMEM((VT_ROWS, tq), F32)],
        compiler_params=_params("parallel", "parallel"),
        name="mla_attention",
    )(q, k, vt, *[w_ for w_, _ in cast])
    return outs[0], outs[1:]


def _hgrn_constants(c):
    t = np.arange(c)
    ltri = (t[None, :] <= t[:, None]).astype(np.float32)
    nlev = int(np.log2(c))
    x = t[:, None] ^ t[None, :]
    level = np.full((c, c), -1, np.int32)
    lower = t[:, None] > t[None, :]
    level[lower] = (nlev - 1) - np.floor(np.log2(x[lower])).astype(np.int32)
    level[t, t] = nlev
    return np.concatenate([ltri, ltri, ltri], axis=1), level, nlev


def _level_exponent(b, half):
    c, dk = b.shape
    if half >= 8:
        x = b.reshape(c // (2 * half), 2 * half, dk)
        ref = x[:, half - 1:half, :]
        return jnp.concatenate([ref - x[:, :half, :], x[:, half:, :] - ref], axis=1).reshape(c, dk)
    x = b.reshape(c // 8, 8, dk)
    if half == 4:
        ref = jnp.broadcast_to(x[:, 3:4, :], x.shape)
    else:
        assert half == 2
        sub = lax.broadcasted_iota(jnp.int32, x.shape, 1)
        ref = jnp.where(sub < 4, x[:, 1:2, :], x[:, 5:6, :])
    return (-jnp.abs(x - ref)).reshape(c, dk)


def _hgrn_body(hq_ref, hf_ref, hi_ref, hg_ref, lbp_ref, gn_ref, w3_ref, lev_ref, o_ref, st_ref, *, layer, tile, chunk, nlev):
    @pl.when(pl.program_id(2) == 0)
    def _():
        st_ref[...] = jnp.zeros_like(st_ref)

    hb = lbp_ref[...]
    ex = jnp.exp(hb - jnp.max(hb, axis=0, keepdims=True))
    sm = ex / jnp.sum(ex, axis=0, keepdims=True)
    lb = jnp.sum(sm[:layer + 1], axis=0, keepdims=True)
    oml = 1.0 - lb
    lev = lev_ref[...]
    gn = gn_ref[...]
    trans_b = (((1,), (1,)), ((), ()))
    chunks = [slice(c * chunk, (c + 1) * chunk) for c in range(tile // chunk)]
    fs, qs, kks, vs, bs = [], [], [], [], []
    for rows in chunks:
        z = hf_ref[rows, :]
        f = lb + oml * jax.nn.sigmoid(z)
        g = jnp.log(f)
        hq = hq_ref[rows, :]
        fs.append(f)
        kks.append(oml * jax.nn.sigmoid(-z))
        qs.append(hq * jax.nn.sigmoid(hq))
        vs.append(hi_ref[rows, :].astype(BF16))
        g_hi = g.astype(BF16)
        r1 = g - g_hi.astype(F32)
        g_mid = r1.astype(BF16)
        g_lo = (r1 - g_mid.astype(F32)).astype(BF16)
        g3 = jnp.concatenate([g_hi, g_mid, g_lo], axis=0)
        bs.append(jnp.dot(w3_ref[...], g3, preferred_element_type=F32))
    scores = [jnp.zeros((chunk, chunk), F32) for _ in chunks]
    q16 = [q.astype(BF16) for q in qs]
    k16 = [kk.astype(BF16) for kk in kks]
    for lv in range(nlev + 1):
        half = chunk >> (lv + 1)
        for c, (f, q, kk, b) in enumerate(zip(fs, q16, k16, bs)):
            if half >= 2:
                e = jnp.exp(_level_exponent(b, half)).astype(BF16)
                ql, kl = q * e, kk * e
            elif half == 1:
                ql, kl = q * f.astype(BF16), kk
            else:
                ql, kl = q, kk
            p = lax.dot_general(ql, kl, trans_b, preferred_element_type=F32)
            scores[c] = jnp.where(lev == lv, p, scores[c])
    intra, updates, decays, qbs = [], [], [], []
    for a, q, kk, v, b in zip(scores, qs, kks, vs, bs):
        intra.append(jnp.dot(a.astype(BF16), v, preferred_element_type=F32))
        b_last = b[chunk - 1:chunk, :]
        kh = (kk * jnp.exp(b_last - b)).astype(BF16)
        updates.append(lax.dot_general(v, kh, (((0,), (0,)), ((), ())), preferred_element_type=F32))
        decays.append(jnp.exp(b_last))
        qbs.append((q * jnp.exp(b)).astype(BF16))
    state = st_ref[...]
    for rows, qb, o_intra, upd, dec in zip(chunks, qbs, intra, updates, decays):
        o = lax.dot_general(qb, state.astype(BF16), trans_b, preferred_element_type=F32) + o_intra
        state = dec * state + upd
        hg = hg_ref[rows, :]
        y = _rms(o, gn) * (hg * jax.nn.sigmoid(hg))
        o_ref[rows, :] = y.astype(o_ref.dtype)
    st_ref[...] = state


def hgrn2(proj, col0, lbp, gn, batch, seq, layer, tile=2048):
    w3, level, nlev = _hgrn_constants(HG_CHUNK)
    w3 = jnp.asarray(w3, BF16)
    level = jnp.asarray(level)
    nt = seq // tile
    cb = col0 // LANES

    def col(group):
        return lambda b, h, i: (b * nt + i, cb + group * HG_HEADS + h)

    return pl.pallas_call(
        functools.partial(_hgrn_body, layer=layer, tile=tile, chunk=HG_CHUNK, nlev=nlev),
        grid=(batch, HG_HEADS, nt),
        in_specs=[pl.BlockSpec((tile, LANES), col(0)), pl.BlockSpec((tile, LANES), col(1)),
                  pl.BlockSpec((tile, LANES), col(2)), pl.BlockSpec((tile, LANES), col(3)),
                  pl.BlockSpec((lbp.shape[0], LANES), lambda b, h, i: (0, h)),
                  pl.BlockSpec((1, LANES), lambda b, h, i: (0, h)),
                  pl.BlockSpec(w3.shape, lambda b, h, i: (0, 0)),
                  pl.BlockSpec(level.shape, lambda b, h, i: (0, 0))],
        out_specs=pl.BlockSpec((tile, LANES), lambda b, h, i: (b * nt + i, h)),
        out_shape=jax.ShapeDtypeStruct((batch * seq, HG_VDIM), BF16),
        scratch_shapes=[pltpu.VMEM((HG_DV, HG_DK), F32)],
        compiler_params=_params("parallel", "parallel", "arbitrary"),
        name="hgrn2",
    )(proj, proj, proj, proj, lbp, gn.reshape(1, -1), w3, level)


def _swap_halves(w):
    half = w.shape[-1] // 2
    return jnp.concatenate([w[..., half:], w[..., :half]], axis=-1)


def _mla_in_weight_body(wt_ref, o_ref):
    n_a = Q_LORA + KV_LORA
    half = QK_ROPE // 2
    top = wt_ref[...].astype(BF16)
    o_ref[...] = jnp.zeros_like(o_ref)
    o_ref[:n_a + QK_ROPE, :] = top
    o_ref[n_a + LANES:n_a + LANES + half, :] = top[n_a + half:]
    o_ref[n_a + LANES + half:n_a + LANES + QK_ROPE, :] = top[n_a:n_a + half]


def mla_in_weight(w_in_t, tn=512):
    d = w_in_t.shape[1]
    rows = Q_LORA + KV_LORA + QK_ROPE
    return pl.pallas_call(
        _mla_in_weight_body,
        grid=(d // tn,),
        in_specs=[pl.BlockSpec((rows, tn), lambda j: (0, j))],
        out_specs=pl.BlockSpec((A_WIDTH, tn), lambda j: (0, j)),
        out_shape=jax.ShapeDtypeStruct((A_WIDTH, d), BF16),
        compiler_params=_params("parallel"),
        name="mla_in_weight",
    )(w_in_t)


def _layer_weights(w_uq, w_ukv):
    wq = w_uq.reshape(Q_LORA, MLA_HEADS, QK_HEAD)
    rope = wq[:, :, QK_NOPE:]
    wq_all = jnp.concatenate([wq[:, :, :QK_NOPE].reshape(Q_LORA, -1), rope.reshape(Q_LORA, -1),
                              _swap_halves(rope).reshape(Q_LORA, -1)], axis=1)
    wkv = w_ukv.reshape(KV_LORA, MLA_HEADS, QK_NOPE + V_HEAD)
    wk = wkv[:, :, :QK_NOPE].reshape(KV_LORA, -1)
    wv = jnp.pad(wkv[:, :, QK_NOPE:], ((0, 0), (0, 0), (0, VT_ROWS - V_HEAD)))
    wvt = wv.reshape(KV_LORA, -1).T
    return wq_all.astype(BF16), wk.astype(BF16), wvt.astype(BF16)


def _rope_rows():
    inv_freq = ROPE_THETA ** (-jnp.arange(0, QK_ROPE, 2, dtype=F32) / QK_ROPE)
    half = jnp.ones((QK_ROPE // 2,), F32)
    reps = LANES // QK_ROPE
    freq = jnp.concatenate([inv_freq, inv_freq] * reps).reshape(1, LANES)
    sign = jnp.concatenate([-half, half] * reps).reshape(1, LANES)
    return freq, sign


def kernel(x, p, positions, norm_mix, w_in, q_a_norm, kv_a_norm, w_uq, w_ukv, hg_lower_bound, hg_out_norm, w_o,
           norm_mlp, w_up, w_down, norm_ple, w_ple_gate, w_ple, ple_post_norm, final_norm):
    batch, seq, d_model = x.shape
    n = batch * seq
    depth = w_in.shape[0]
    h = x.reshape(n, d_model)
    pos = positions.reshape(n, 1)
    freq, sign = _rope_rows()
    for i in range(depth):
        w_in_t = w_in[i].T
        wq_all, wk, wvt = _layer_weights(w_uq[i], w_ukv[i])
        proj_a, xb, ss, w_ht = front(h, norm_mix[i], mla_in_weight(w_in_t), w_in_t, Q_LORA + KV_LORA + QK_ROPE)
        proj_h, (w_up_b,) = scaled_matmul(xb, ss, w_ht, F32, "in_proj_h", w_transposed=True, tn=1024,
                                          cast=((w_up[i], 1024),))
        q, k, vt = mla_prep(proj_a, pos, q_a_norm[i], kv_a_norm[i], wq_all, wk, wvt, freq, sign)
        o_mla, (w_o_b, w_pg_b) = causal_attention(q, k, vt, batch, seq, cast=((w_o[i], 512), (w_ple_gate[i], 512)))
        o_hg = hgrn2(proj_h, 0, hg_lower_bound, hg_out_norm[i], batch, seq, i)
        h, hb, ss = out_proj_residual(o_mla, o_hg, w_o_b, h, norm_mlp[i])
        hidden, (w_down_b,) = scaled_matmul(hb, ss, w_up_b, BF16, "mlp_up", relu2=True, tiled_out=True,
                                            cast=((w_down[i], 1024),))
        h, hb, ss = mlp_down_residual(hidden, w_down_b, h, norm_ple[i])
        e = embed_norm(p[i].reshape(n, -1), w_ple[i].astype(BF16), ple_post_norm[i])
        h = gate_residual(hb, ss, w_pg_b, h, e)
    return rmsnorm(h, final_norm, x.dtype).reshape(batch, seq, d_model)
```

```python
import functools

import numpy as np
import jax
import jax.numpy as jnp
from jax import lax
from jax.experimental import pallas as pl
from jax.experimental.pallas import tpu as pltpu

EPS = 1e-6
MLA_HEADS = 16
QK_NOPE = 128
QK_ROPE = 64
QK_HEAD = QK_NOPE + QK_ROPE
V_HEAD = 128
Q_LORA = 768
KV_LORA = 512
ROPE_THETA = 10000.0
HG_HEADS = 16
HG_DK = 128
HG_DV = 128
HG_FDIM = HG_HEADS * HG_DK
HG_VDIM = HG_HEADS * HG_DV

LANES = 128
QK_PAD = 2 * LANES
A_WIDTH = Q_LORA + KV_LORA + 2 * LANES
HG_CHUNK = 128
VMEM_LIMIT_BYTES = 56 * 1024 * 1024
LOG2_E = 1.4426950408889634
VT_ROWS = V_HEAD + 16

F32 = jnp.float32
BF16 = jnp.bfloat16


def _params(*semantics):
    return pltpu.CompilerParams(dimension_semantics=semantics, vmem_limit_bytes=VMEM_LIMIT_BYTES)


def _rms(x, g):
    return x * lax.rsqrt(jnp.mean(x * x, axis=-1, keepdims=True) + EPS) * g


def _emit_norm_inputs(h, g_ref, hb_ref, ss_ref, first):
    hb_ref[...] = (h * g_ref[...]).astype(BF16)

    @pl.when(first)
    def _():
        ss_ref[...] = jnp.zeros_like(ss_ref)

    ss_ref[...] += jnp.sum(h * h, axis=-1, keepdims=True)


def _row_scale(ss_ref, d):
    return lax.rsqrt(ss_ref[:, :1] / d + EPS)


def _cast_specs(ws, n_steps, step_of):
    specs, shapes = [], []
    for w in ws:
        rows = w.shape[0] // n_steps
        assert rows * n_steps == w.shape[0] and rows % 16 == 0
        specs.append(pl.BlockSpec((rows, w.shape[1]), lambda *ids: (step_of(*ids), 0)))
        shapes.append(jax.ShapeDtypeStruct(w.shape, BF16))
    return specs, shapes


def _cast_slabs(src_refs, dst_refs):
    for src, dst in zip(src_refs, dst_refs):
        dst[...] = src[...].astype(dst.dtype)


TRANS_B = (((1,), (1,)), ((), ()))


def _scaled_mm_body(a_ref, w_ref, ss_ref, *refs, relu2, w_transposed):
    n_cast = len(refs) // 2
    o_ref = refs[n_cast]
    if w_transposed:
        acc = lax.dot_general(a_ref[...], w_ref[...], TRANS_B, preferred_element_type=F32)
    else:
        acc = jnp.dot(a_ref[...], w_ref[...], preferred_element_type=F32)
    acc = acc * _row_scale(ss_ref, a_ref.shape[1])
    if relu2:
        acc = jnp.square(jnp.maximum(acc, 0.0))
    o_ref[...] = acc.astype(o_ref.dtype)
    _cast_slabs(refs[:n_cast], refs[n_cast + 1:])


def scaled_matmul(hb, ss, w, out_dtype, name, relu2=False, w_transposed=False, tm=1024, tn=512, cast=()):
    m, k = hb.shape
    n = w.shape[0] if w_transposed else w.shape[1]
    nj = n // tn
    cast_specs, cast_shapes = _cast_specs(cast, (m // tm) * nj, lambda i, j: i * nj + j)
    w_spec = pl.BlockSpec((tn, k), lambda i, j: (j, 0)) if w_transposed else pl.BlockSpec((k, tn), lambda i, j: (0, j))
    outs = pl.pallas_call(
        functools.partial(_scaled_mm_body, relu2=relu2, w_transposed=w_transposed),
        grid=(m // tm, nj),
        in_specs=[pl.BlockSpec((tm, k), lambda i, j: (i, 0)), w_spec,
                  pl.BlockSpec((tm, LANES), lambda i, j: (i, 0))] + cast_specs,
        out_specs=[pl.BlockSpec((tm, tn), lambda i, j: (i, j))] + cast_specs,
        out_shape=[jax.ShapeDtypeStruct((m, n), out_dtype)] + cast_shapes,
        compiler_params=_params("parallel", "parallel"),
        name=name,
    )(hb, w, ss, *cast)
    return outs[0], outs[1:]


def _front_body(x_ref, g_ref, wat_ref, wt_lo_ref, wt_hi_ref, pa_ref, xb_ref, ss_ref, wht_ref):
    x = x_ref[...]
    xb = (x * g_ref[...]).astype(BF16)
    xb_ref[...] = xb
    ss = jnp.sum(x * x, axis=-1, keepdims=True)
    ss_ref[...] = jnp.broadcast_to(ss, ss_ref.shape)
    acc = lax.dot_general(xb, wat_ref[...], TRANS_B, preferred_element_type=F32)
    pa_ref[...] = acc * lax.rsqrt(ss / x.shape[1] + EPS)
    half = wt_lo_ref.shape[0]
    wht_ref[:half, :] = wt_lo_ref[...].astype(BF16)
    wht_ref[half:, :] = wt_hi_ref[...].astype(BF16)


def front(x, gain, w_at, w_t, h_row0, tm=256):
    m, d = x.shape
    n = w_at.shape[0]
    steps = m // tm
    h_rows = w_t.shape[0] - h_row0
    half = h_rows // (2 * steps)
    assert 2 * half * steps == h_rows and half % 16 == 0 and h_row0 % half == 0
    first = h_row0 // half
    row = lambda i: (i, 0)
    return pl.pallas_call(
        _front_body,
        grid=(steps,),
        in_specs=[pl.BlockSpec((tm, d), row), pl.BlockSpec((1, d), lambda i: (0, 0)),
                  pl.BlockSpec((n, d), lambda i: (0, 0), pipeline_mode=pl.Buffered(1)),
                  pl.BlockSpec((half, d), lambda i: (first + 2 * i, 0)),
                  pl.BlockSpec((half, d), lambda i: (first + 2 * i + 1, 0))],
        out_specs=[pl.BlockSpec((tm, n), row), pl.BlockSpec((tm, d), row), pl.BlockSpec((tm, LANES), row),
                   pl.BlockSpec((2 * half, d), row)],
        out_shape=[jax.ShapeDtypeStruct((m, n), F32), jax.ShapeDtypeStruct((m, d), BF16),
                   jax.ShapeDtypeStruct((m, LANES), F32), jax.ShapeDtypeStruct((h_rows, d), BF16)],
        compiler_params=_params("parallel"),
        name="front",
    )(x, gain.reshape(1, d), w_at, w_t, w_t)


def _norm_outputs(m, n, tm, tn, index):
    specs = [pl.BlockSpec((tm, tn), index), pl.BlockSpec((tm, tn), index),
             pl.BlockSpec((tm, LANES), lambda i, *_: (i, 0))]
    shapes = [jax.ShapeDtypeStruct((m, n), F32), jax.ShapeDtypeStruct((m, n), BF16),
              jax.ShapeDtypeStruct((m, LANES), F32)]
    return specs, shapes


def _out_proj_body(a1_ref, a2_ref, w1_ref, w2_ref, r_ref, g_ref, o_ref, hb_ref, ss_ref):
    @pl.when(pl.program_id(1) == 0)
    def _():
        ss_ref[...] = jnp.zeros_like(ss_ref)

    tm = o_ref.shape[0]
    half = tm // 2
    for rows in (slice(0, half), slice(half, tm)):
        acc = jnp.dot(a1_ref[rows, :], w1_ref[...], preferred_element_type=F32)
        acc = acc + jnp.dot(a2_ref[rows, :], w2_ref[...], preferred_element_type=F32)
        h = r_ref[rows, :] + acc
        o_ref[rows, :] = h
        hb_ref[rows, :] = (h * g_ref[...]).astype(BF16)
        ss_ref[rows, :] += jnp.sum(h * h, axis=-1, keepdims=True)


def out_proj_residual(a1, a2, w, res, gain, tm=1024, tn=512):
    m, k1 = a1.shape
    k2 = a2.shape[1]
    assert k1 == k2 and w.shape[0] == k1 + k2
    n = w.shape[1]
    tile = lambda i, j: (i, j)
    out_specs, out_shape = _norm_outputs(m, n, tm, tn, tile)
    return pl.pallas_call(
        _out_proj_body,
        grid=(m // tm, n // tn),
        in_specs=[pl.BlockSpec((tm, k1), lambda i, j: (i, 0)), pl.BlockSpec((tm, k2), lambda i, j: (i, 0)),
                  pl.BlockSpec((k1, tn), lambda i, j: (0, j)), pl.BlockSpec((k2, tn), lambda i, j: (1, j)),
                  pl.BlockSpec((tm, tn), tile), pl.BlockSpec((1, tn), lambda i, j: (0, j))],
        out_specs=out_specs,
        out_shape=out_shape,
        compiler_params=_params("parallel", "arbitrary"),
        name="out_proj",
    )(a1, a2, w, w, res, gain.reshape(1, n))


def _mlp_down_body(a_ref, w_ref, r_ref, g_ref, o_ref, hb_ref, ss_ref, acc_ref):
    k = pl.program_id(2)
    last = pl.num_programs(2) - 1

    def partial_product():
        return jnp.dot(a_ref[...], w_ref[...], preferred_element_type=F32)

    @pl.when(k == 0)
    def _():
        acc_ref[...] = r_ref[...] + partial_product()

    @pl.when(jnp.logical_and(k > 0, k < last))
    def _():
        acc_ref[...] += partial_product()

    @pl.when(k == last)
    def _():
        h = acc_ref[...] + partial_product()
        o_ref[...] = h
        _emit_norm_inputs(h, g_ref, hb_ref, ss_ref, pl.program_id(1) == 0)


def mlp_down_residual(a, w, res, gain, tm=1024, tn=1024, tk=2048):
    m, k = a.shape
    n = w.shape[1]
    assert k // tk >= 2
    tile = lambda i, j, kk: (i, j)
    out_specs, out_shape = _norm_outputs(m, n, tm, tn, tile)
    return pl.pallas_call(
        _mlp_down_body,
        grid=(m // tm, n // tn, k // tk),
        in_specs=[pl.BlockSpec((tm, tk), lambda i, j, kk: (i, kk)), pl.BlockSpec((tk, tn), lambda i, j, kk: (kk, j)),
                  pl.BlockSpec((tm, tn), tile), pl.BlockSpec((1, tn), lambda i, j, kk: (0, j))],
        out_specs=out_specs,
        out_shape=out_shape,
        scratch_shapes=[pltpu.VMEM((tm, tn), F32)],
        compiler_params=_params("parallel", "arbitrary", "arbitrary"),
        name="mlp_down",
    )(a, w, res, gain.reshape(1, n))


def _gate_body(a_ref, w_ref, ss_ref, h_ref, p_ref, we_ref, ge_ref, sse_ref, o_ref):
    tm, d = a_ref.shape
    half = tm // 2
    for rows in (slice(0, half), slice(half, tm)):
        scale = lax.rsqrt(ss_ref[rows, :1] / d + EPS)
        acc = jnp.dot(a_ref[rows, :], w_ref[...], preferred_element_type=F32) * scale
        gate = 0.5 + 0.5 * jnp.tanh(0.5 * acc)
        y = jnp.dot(p_ref[rows, :].astype(BF16), we_ref[...], preferred_element_type=F32)
        e = y * lax.rsqrt(sse_ref[rows, :1] / d + EPS) * ge_ref[...]
        o_ref[rows, :] = h_ref[rows, :] + gate * e


def gate_residual(hb, ss, w, h, p, w_e, g_e, ss_e, tm=1024, tn=512):
    m, k = hb.shape
    n = w.shape[1]
    assert w_e.shape[1] == n == k
    return pl.pallas_call(
        _gate_body,
        grid=(m // tm, n // tn),
        in_specs=[pl.BlockSpec((tm, k), lambda i, j: (i, 0)), pl.BlockSpec((k, tn), lambda i, j: (0, j)),
                  pl.BlockSpec((tm, LANES), lambda i, j: (i, 0)), pl.BlockSpec((tm, tn), lambda i, j: (i, j)),
                  pl.BlockSpec((tm, p.shape[1]), lambda i, j: (i, 0)), pl.BlockSpec((p.shape[1], tn), lambda i, j: (0, j)),
                  pl.BlockSpec((1, tn), lambda i, j: (0, j)), pl.BlockSpec((tm, LANES), lambda i, j: (i, 0))],
        out_specs=pl.BlockSpec((tm, tn), lambda i, j: (i, j)),
        out_shape=jax.ShapeDtypeStruct((m, n), F32),
        compiler_params=_params("parallel", "parallel"),
        name="ple_gate",
    )(hb, w, ss, h, p, w_e, g_e.reshape(1, n), ss_e)


def _rmsnorm_body(x_ref, g_ref, o_ref):
    o_ref[...] = _rms(x_ref[...], g_ref[...]).astype(o_ref.dtype)


def rmsnorm(x, g, out_dtype, tm=256):
    n, d = x.shape
    return pl.pallas_call(
        _rmsnorm_body,
        grid=(n // tm,),
        in_specs=[pl.BlockSpec((tm, d), lambda i: (i, 0)), pl.BlockSpec((1, d), lambda i: (0, 0))],
        out_specs=pl.BlockSpec((tm, d), lambda i: (i, 0)),
        out_shape=jax.ShapeDtypeStruct((n, d), out_dtype),
        compiler_params=_params("parallel"),
        name="rmsnorm",
    )(x, g.reshape(1, d))


def _embed_stats_body(p_ref, w_ref, ss_ref):
    y = jnp.dot(p_ref[...].astype(BF16), w_ref[...], preferred_element_type=F32)
    ss_ref[...] = jnp.broadcast_to(jnp.sum(y * y, axis=-1, keepdims=True), ss_ref.shape)


def embed_row_sumsq(p, w, tm=256):
    m, k = p.shape
    n = w.shape[1]
    return pl.pallas_call(
        _embed_stats_body,
        grid=(m // tm,),
        in_specs=[pl.BlockSpec((tm, k), lambda i: (i, 0)), pl.BlockSpec((k, n), lambda i: (0, 0))],
        out_specs=pl.BlockSpec((tm, LANES), lambda i: (i, 0)),
        out_shape=jax.ShapeDtypeStruct((m, LANES), F32),
        compiler_params=_params("parallel"),
        name="ple_embed_stats",
    )(p, w)


def _mla_prep_body(pa_ref, pos_ref, gq_ref, gkv_ref, wq_ref, wk_ref, wvt_ref, ones_ref, freq_ref, sign_ref, q_ref, k_ref,
                   vt_ref):
    pa = pa_ref[...]
    cq = _rms(pa[:, :Q_LORA], gq_ref[...]).astype(BF16)
    ckv = _rms(pa[:, Q_LORA:Q_LORA + KV_LORA], gkv_ref[...]).astype(BF16)
    kr = pa[:, Q_LORA + KV_LORA:Q_LORA + KV_LORA + LANES]
    krs = pa[:, Q_LORA + KV_LORA + LANES:]
    ang = pos_ref[...].astype(F32) * freq_ref[...]
    cos = jnp.cos(ang)
    sin = jnp.sin(ang) * sign_ref[...]
    q = jnp.dot(cq, wq_ref[...], preferred_element_type=F32)
    kn = jnp.dot(ckv, wk_ref[...], preferred_element_type=F32)
    vt_ref[...] = (lax.dot_general(wvt_ref[...], ckv, TRANS_B, preferred_element_type=F32) + ones_ref[...]).astype(BF16)
    krot = (kr * cos + krs * sin).astype(BF16)
    qscale = (QK_HEAD ** -0.5) * LOG2_E
    hn = MLA_HEADS * LANES
    hr = MLA_HEADS * QK_ROPE
    low = lax.broadcasted_iota(jnp.int32, cos.shape, 1) < QK_ROPE
    for h in range(MLA_HEADS):
        lo, hi = h * LANES, (h + 1) * LANES
        if h % 2 == 0:
            g0 = hn + (h // 2) * LANES
            pair = (q[:, g0:g0 + LANES] * cos + q[:, hr + g0:hr + g0 + LANES] * sin) * qscale
            qrot = pair
        else:
            qrot = pltpu.roll(pair, QK_ROPE, axis=1)
        q_ref[:, h * QK_PAD:h * QK_PAD + LANES] = (q[:, lo:hi] * qscale).astype(BF16)
        q_ref[:, h * QK_PAD + LANES:(h + 1) * QK_PAD] = jnp.where(low, qrot, 0.0).astype(BF16)
        k_ref[:, h * QK_PAD:h * QK_PAD + LANES] = kn[:, lo:hi].astype(BF16)
        k_ref[:, h * QK_PAD + LANES:(h + 1) * QK_PAD] = krot


def mla_prep(proj, pos, gq, gkv, wq, wk, wvt, freq, sign, tm=256):
    n = proj.shape[0]
    const = lambda i: (0, 0)
    ones_rows = np.zeros((MLA_HEADS, VT_ROWS, 1), np.float32)
    ones_rows[:, V_HEAD, 0] = 1.0
    ones_rows = jnp.asarray(ones_rows.reshape(MLA_HEADS * VT_ROWS, 1))
    return pl.pallas_call(
        _mla_prep_body,
        grid=(n // tm,),
        in_specs=[pl.BlockSpec((tm, A_WIDTH), lambda i: (i, 0)), pl.BlockSpec((tm, 1), lambda i: (i, 0)),
                  pl.BlockSpec((1, Q_LORA), const), pl.BlockSpec((1, KV_LORA), const),
                  pl.BlockSpec(wq.shape, const), pl.BlockSpec(wk.shape, const), pl.BlockSpec(wvt.shape, const),
                  pl.BlockSpec(ones_rows.shape, const), pl.BlockSpec((1, LANES), const), pl.BlockSpec((1, LANES), const)],
        out_specs=[pl.BlockSpec((tm, MLA_HEADS * QK_PAD), lambda i: (i, 0)),
                   pl.BlockSpec((tm, MLA_HEADS * QK_PAD), lambda i: (i, 0)),
                   pl.BlockSpec((MLA_HEADS * VT_ROWS, tm), lambda i: (0, i))],
        out_shape=[jax.ShapeDtypeStruct((n, MLA_HEADS * QK_PAD), BF16),
                   jax.ShapeDtypeStruct((n, MLA_HEADS * QK_PAD), BF16),
                   jax.ShapeDtypeStruct((MLA_HEADS * VT_ROWS, n), BF16)],
        compiler_params=_params("parallel"),
        name="mla_prep",
    )(proj, pos, gq.reshape(1, -1), gkv.reshape(1, -1), wq, wk, wvt, ones_rows, freq, sign)


def _attn_body(q_ref, k_ref, vt_ref, *refs, tq, tk, n_cast):
    o_ref = refs[n_cast]
    s_ref, mx_ref, m_ref, acc_ref = refs[2 * n_cast + 1:]
    _cast_slabs(refs[:n_cast], refs[n_cast + 1:2 * n_cast + 1])
    seq = q_ref.shape[0]
    n_diag = tq // tk
    gw = 2 * LANES
    tiles = [(qi, j) for qi in range(seq // tq) for j in range((qi + 1) * n_diag)]

    def keys_needed(qi, j, c):
        diag = j - qi * n_diag
        return tk if diag < 0 else max(0, min(tk, c + gw - diag * tk))

    def produce(slot, qi, j):
        for c in range(0, tq, gw):
            nk = keys_needed(qi, j, c)
            if nk == 0:
                continue
            q = q_ref[qi * tq + c:qi * tq + c + gw, :]
            s = lax.dot_general(k_ref[j * tk:j * tk + nk, :], q, (((1,), (1,)), ((), ())),
                                preferred_element_type=F32)
            s_ref[slot, :nk, c:c + gw] = s
            mx_ref[slot, :, c:c + gw] = jnp.max(s, axis=0, keepdims=True)

    def consume(slot, qi, j):
        for c in range(0, tq, gw):
            cols = slice(c, c + gw)
            nk = keys_needed(qi, j, c)
            if nk == 0:
                continue
            s = s_ref[slot, :nk, cols]
            first_key = (j - qi * n_diag) * tk
            if first_key + nk - 1 <= c:
                tile_max = mx_ref[slot, :, cols]
            else:
                key = first_key + lax.broadcasted_iota(jnp.int32, (nk, gw), 0)
                qry = c + lax.broadcasted_iota(jnp.int32, (nk, gw), 1)
                s = jnp.where(key <= qry, s, -jnp.inf)
                tile_max = jnp.max(s, axis=0, keepdims=True)
            vt = vt_ref[:, j * tk:j * tk + nk]
            if j == 0:
                m_new = tile_max
                p = jnp.exp2(s - m_new)
                acc_ref[:, cols] = jnp.dot(vt, p.astype(BF16), preferred_element_type=F32)
            else:
                m = m_ref[:, cols]
                m_new = jnp.maximum(m, tile_max)
                alpha = jnp.exp2(m - m_new)
                p = jnp.exp2(s - m_new)
                acc_ref[:, cols] = alpha * acc_ref[:, cols] + jnp.dot(vt, p.astype(BF16), preferred_element_type=F32)
            m_ref[:, cols] = m_new

    produce(0, *tiles[0])
    for t, (qi, j) in enumerate(tiles):
        if t + 1 < len(tiles):
            produce((t + 1) % 2, *tiles[t + 1])
        consume(t % 2, qi, j)
        if j == (qi + 1) * n_diag - 1:
            acc = acc_ref[...]
            o_ref[qi * tq:(qi + 1) * tq, :] = (acc[:V_HEAD] / acc[V_HEAD:V_HEAD + 1]).T.astype(o_ref.dtype)


def causal_attention(q, k, vt, batch, seq, tq=1024, tk=1024, cast=()):
    cast_specs, cast_shapes = _cast_specs(cast, batch * MLA_HEADS, lambda b, h: b * MLA_HEADS + h)
    outs = pl.pallas_call(
        functools.partial(_attn_body, tq=tq, tk=tk, n_cast=len(cast)),
        grid=(batch, MLA_HEADS),
        in_specs=[pl.BlockSpec((seq, QK_PAD), lambda b, h: (b, h)), pl.BlockSpec((seq, QK_PAD), lambda b, h: (b, h)),
                  pl.BlockSpec((VT_ROWS, seq), lambda b, h: (h, b))] + cast_specs,
        out_specs=[pl.BlockSpec((seq, V_HEAD), lambda b, h: (b, h))] + cast_specs,
        out_shape=[jax.ShapeDtypeStruct((batch * seq, MLA_HEADS * V_HEAD), BF16)] + cast_shapes,
        scratch_shapes=[pltpu.VMEM((2, tk, tq), F32), pltpu.VMEM((2, 1, tq), F32), pltpu.VMEM((1, tq), F32),
                        pltpu.VMEM((VT_ROWS, tq), F32)],
        compiler_params=_params("parallel", "parallel"),
        name="mla_attention",
    )(q, k, vt, *cast)
    return outs[0], outs[1:]


def _hgrn_constants(c):
    t = np.arange(c)
    ltri = (t[None, :] <= t[:, None]).astype(np.float32)
    nlev = int(np.log2(c))
    x = t[:, None] ^ t[None, :]
    level = np.full((c, c), -1, np.int32)
    lower = t[:, None] > t[None, :]
    level[lower] = (nlev - 1) - np.floor(np.log2(x[lower])).astype(np.int32)
    level[t, t] = nlev
    return np.concatenate([ltri, ltri, ltri], axis=1), level, nlev


def _level_exponent(b, half):
    c, dk = b.shape
    if half >= 8:
        x = b.reshape(c // (2 * half), 2 * half, dk)
        ref = x[:, half - 1:half, :]
        return jnp.concatenate([ref - x[:, :half, :], x[:, half:, :] - ref], axis=1).reshape(c, dk)
    x = b.reshape(c // 8, 8, dk)
    if half == 4:
        ref = jnp.broadcast_to(x[:, 3:4, :], x.shape)
    else:
        assert half == 2
        sub = lax.broadcasted_iota(jnp.int32, x.shape, 1)
        ref = jnp.where(sub < 4, x[:, 1:2, :], x[:, 5:6, :])
    return (-jnp.abs(x - ref)).reshape(c, dk)


def _hgrn_body(hq_ref, hf_ref, hi_ref, hg_ref, lbp_ref, gn_ref, w3_ref, lev_ref, o_ref, st_ref, *, layer, tile, chunk, nlev):
    @pl.when(pl.program_id(2) == 0)
    def _():
        st_ref[...] = jnp.zeros_like(st_ref)

    hb = lbp_ref[...]
    ex = jnp.exp(hb - jnp.max(hb, axis=0, keepdims=True))
    sm = ex / jnp.sum(ex, axis=0, keepdims=True)
    lb = jnp.sum(sm[:layer + 1], axis=0, keepdims=True)
    oml = 1.0 - lb
    lev = lev_ref[...]
    gn = gn_ref[...]
    trans_b = (((1,), (1,)), ((), ()))
    chunks = [slice(c * chunk, (c + 1) * chunk) for c in range(tile // chunk)]
    fs, qs, kks, vs, bs = [], [], [], [], []
    for rows in chunks:
        z = hf_ref[rows, :]
        f = lb + oml * jax.nn.sigmoid(z)
        g = jnp.log(f)
        hq = hq_ref[rows, :]
        fs.append(f)
        kks.append(oml * jax.nn.sigmoid(-z))
        qs.append(hq * jax.nn.sigmoid(hq))
        vs.append(hi_ref[rows, :].astype(BF16))
        g_hi = g.astype(BF16)
        r1 = g - g_hi.astype(F32)
        g_mid = r1.astype(BF16)
        g_lo = (r1 - g_mid.astype(F32)).astype(BF16)
        g3 = jnp.concatenate([g_hi, g_mid, g_lo], axis=0)
        bs.append(jnp.dot(w3_ref[...], g3, preferred_element_type=F32))
    scores = [jnp.zeros((chunk, chunk), F32) for _ in chunks]
    q16 = [q.astype(BF16) for q in qs]
    k16 = [kk.astype(BF16) for kk in kks]
    for lv in range(nlev + 1):
        half = chunk >> (lv + 1)
        for c, (f, q, kk, b) in enumerate(zip(fs, q16, k16, bs)):
            if half >= 2:
                e = jnp.exp(_level_exponent(b, half)).astype(BF16)
                ql, kl = q * e, kk * e
            elif half == 1:
                ql, kl = q * f.astype(BF16), kk
            else:
                ql, kl = q, kk
            p = lax.dot_general(ql, kl, trans_b, preferred_element_type=F32)
            scores[c] = jnp.where(lev == lv, p, scores[c])
    intra, updates, decays, qbs = [], [], [], []
    for a, q, kk, v, b in zip(scores, qs, kks, vs, bs):
        intra.append(jnp.dot(a.astype(BF16), v, preferred_element_type=F32))
        b_last = b[chunk - 1:chunk, :]
        kh = (kk * jnp.exp(b_last - b)).astype(BF16)
        updates.append(lax.dot_general(v, kh, (((0,), (0,)), ((), ())), preferred_element_type=F32))
        decays.append(jnp.exp(b_last))
        qbs.append((q * jnp.exp(b)).astype(BF16))
    state = st_ref[...]
    for rows, qb, o_intra, upd, dec in zip(chunks, qbs, intra, updates, decays):
        o = lax.dot_general(qb, state.astype(BF16), trans_b, preferred_element_type=F32) + o_intra
        state = dec * state + upd
        hg = hg_ref[rows, :]
        y = _rms(o, gn) * (hg * jax.nn.sigmoid(hg))
        o_ref[rows, :] = y.astype(o_ref.dtype)
    st_ref[...] = state


def hgrn2(proj, col0, lbp, gn, batch, seq, layer, tile=2048):
    w3, level, nlev = _hgrn_constants(HG_CHUNK)
    w3 = jnp.asarray(w3, BF16)
    level = jnp.asarray(level)
    nt = seq // tile
    cb = col0 // LANES

    def col(group):
        return lambda b, h, i: (b * nt + i, cb + group * HG_HEADS + h)

    return pl.pallas_call(
        functools.partial(_hgrn_body, layer=layer, tile=tile, chunk=HG_CHUNK, nlev=nlev),
        grid=(batch, HG_HEADS, nt),
        in_specs=[pl.BlockSpec((tile, LANES), col(0)), pl.BlockSpec((tile, LANES), col(1)),
                  pl.BlockSpec((tile, LANES), col(2)), pl.BlockSpec((tile, LANES), col(3)),
                  pl.BlockSpec((lbp.shape[0], LANES), lambda b, h, i: (0, h)),
                  pl.BlockSpec((1, LANES), lambda b, h, i: (0, h)),
                  pl.BlockSpec(w3.shape, lambda b, h, i: (0, 0)),
                  pl.BlockSpec(level.shape, lambda b, h, i: (0, 0))],
        out_specs=pl.BlockSpec((tile, LANES), lambda b, h, i: (b * nt + i, h)),
        out_shape=jax.ShapeDtypeStruct((batch * seq, HG_VDIM), BF16),
        scratch_shapes=[pltpu.VMEM((HG_DV, HG_DK), F32)],
        compiler_params=_params("parallel", "parallel", "arbitrary"),
        name="hgrn2",
    )(proj, proj, proj, proj, lbp, gn.reshape(1, -1), w3, level)


def _swap_halves(w):
    half = w.shape[-1] // 2
    return jnp.concatenate([w[..., half:], w[..., :half]], axis=-1)


def _mla_in_weight_body(wt_ref, o_ref):
    n_a = Q_LORA + KV_LORA
    half = QK_ROPE // 2
    top = wt_ref[...].astype(BF16)
    o_ref[...] = jnp.zeros_like(o_ref)
    o_ref[:n_a + QK_ROPE, :] = top
    o_ref[n_a + LANES:n_a + LANES + half, :] = top[n_a + half:]
    o_ref[n_a + LANES + half:n_a + LANES + QK_ROPE, :] = top[n_a:n_a + half]


def mla_in_weight(w_in_t, tn=512):
    d = w_in_t.shape[1]
    rows = Q_LORA + KV_LORA + QK_ROPE
    return pl.pallas_call(
        _mla_in_weight_body,
        grid=(d // tn,),
        in_specs=[pl.BlockSpec((rows, tn), lambda j: (0, j))],
        out_specs=pl.BlockSpec((A_WIDTH, tn), lambda j: (0, j)),
        out_shape=jax.ShapeDtypeStruct((A_WIDTH, d), BF16),
        compiler_params=_params("parallel"),
        name="mla_in_weight",
    )(w_in_t)


def _layer_weights(w_uq, w_ukv):
    wq = w_uq.reshape(Q_LORA, MLA_HEADS, QK_HEAD)
    rope = wq[:, :, QK_NOPE:]
    wq_all = jnp.concatenate([wq[:, :, :QK_NOPE].reshape(Q_LORA, -1), rope.reshape(Q_LORA, -1),
                              _swap_halves(rope).reshape(Q_LORA, -1)], axis=1)
    wkv = w_ukv.reshape(KV_LORA, MLA_HEADS, QK_NOPE + V_HEAD)
    wk = wkv[:, :, :QK_NOPE].reshape(KV_LORA, -1)
    wv = jnp.pad(wkv[:, :, QK_NOPE:], ((0, 0), (0, 0), (0, VT_ROWS - V_HEAD)))
    wvt = wv.reshape(KV_LORA, -1).T
    return wq_all.astype(BF16), wk.astype(BF16), wvt.astype(BF16)


def _rope_rows():
    inv_freq = ROPE_THETA ** (-jnp.arange(0, QK_ROPE, 2, dtype=F32) / QK_ROPE)
    half = jnp.ones((QK_ROPE // 2,), F32)
    reps = LANES // QK_ROPE
    freq = jnp.concatenate([inv_freq, inv_freq] * reps).reshape(1, LANES)
    sign = jnp.concatenate([-half, half] * reps).reshape(1, LANES)
    return freq, sign


def kernel(x, p, positions, norm_mix, w_in, q_a_norm, kv_a_norm, w_uq, w_ukv, hg_lower_bound, hg_out_norm, w_o,
           norm_mlp, w_up, w_down, norm_ple, w_ple_gate, w_ple, ple_post_norm, final_norm):
    batch, seq, d_model = x.shape
    n = batch * seq
    depth = w_in.shape[0]
    h = x.reshape(n, d_model)
    pos = positions.reshape(n, 1)
    freq, sign = _rope_rows()
    for i in range(depth):
        w_in_t = w_in[i].T
        wq_all, wk, wvt = _layer_weights(w_uq[i], w_ukv[i])
        proj_a, xb, ss, w_ht = front(h, norm_mix[i], mla_in_weight(w_in_t), w_in_t, Q_LORA + KV_LORA + QK_ROPE)
        proj_h, (w_up_b,) = scaled_matmul(xb, ss, w_ht, F32, "in_proj_h", w_transposed=True, tn=1024,
                                          cast=(w_up[i],))
        q, k, vt = mla_prep(proj_a, pos, q_a_norm[i], kv_a_norm[i], wq_all, wk, wvt, freq, sign)
        o_mla, (w_o_b, w_pg_b) = causal_attention(q, k, vt, batch, seq, cast=(w_o[i], w_ple_gate[i]))
        o_hg = hgrn2(proj_h, 0, hg_lower_bound, hg_out_norm[i], batch, seq, i)
        h, hb, ss = out_proj_residual(o_mla, o_hg, w_o_b, h, norm_mlp[i])
        hidden, (w_down_b,) = scaled_matmul(hb, ss, w_up_b, BF16, "mlp_up", relu2=True, tn=1024, cast=(w_down[i],))
        h, hb, ss = mlp_down_residual(hidden, w_down_b, h, norm_ple[i])
        p_i, w_e = p[i].reshape(n, -1), w_ple[i].astype(BF16)
        h = gate_residual(hb, ss, w_pg_b, h, p_i, w_e, ple_post_norm[i], embed_row_sumsq(p_i, w_e))
    return rmsnorm(h, final_norm, x.dtype).reshape(batch, seq, d_model)
```

```python
import functools

import numpy as np
import jax
import jax.numpy as jnp
from jax import lax
from jax.experimental import pallas as pl
from jax.experimental.pallas import tpu as pltpu

EPS = 1e-6
MLA_HEADS = 16
QK_NOPE = 128
QK_ROPE = 64
QK_HEAD = QK_NOPE + QK_ROPE
V_HEAD = 128
Q_LORA = 768
KV_LORA = 512
ROPE_THETA = 10000.0
HG_HEADS = 16
HG_DK = 128
HG_DV = 128
HG_FDIM = HG_HEADS * HG_DK
HG_VDIM = HG_HEADS * HG_DV

LANES = 128
QK_PAD = 2 * LANES
A_WIDTH = Q_LORA + KV_LORA + 2 * LANES
HG_CHUNK = 128
VMEM_LIMIT_BYTES = 56 * 1024 * 1024
LOG2_E = 1.4426950408889634
VT_ROWS = V_HEAD + 16

F32 = jnp.float32
BF16 = jnp.bfloat16


def _params(*semantics):
    return pltpu.CompilerParams(dimension_semantics=semantics, vmem_limit_bytes=VMEM_LIMIT_BYTES)


def _rms(x, g):
    return x * lax.rsqrt(jnp.mean(x * x, axis=-1, keepdims=True) + EPS) * g


def _emit_norm_inputs(h, g_ref, hb_ref, ss_ref, first):
    hb_ref[...] = (h * g_ref[...]).astype(BF16)

    @pl.when(first)
    def _():
        ss_ref[...] = jnp.zeros_like(ss_ref)

    ss_ref[...] += jnp.sum(h * h, axis=-1, keepdims=True)


def _row_scale(ss_ref, d):
    return lax.rsqrt(ss_ref[:, :1] / d + EPS)


def _cast_specs(ws, n_steps, step_of):
    specs, shapes = [], []
    for w in ws:
        rows = w.shape[0] // n_steps
        assert rows * n_steps == w.shape[0] and rows % 16 == 0
        specs.append(pl.BlockSpec((rows, w.shape[1]), lambda *ids: (step_of(*ids), 0)))
        shapes.append(jax.ShapeDtypeStruct(w.shape, BF16))
    return specs, shapes


def _cast_slabs(src_refs, dst_refs):
    for src, dst in zip(src_refs, dst_refs):
        dst[...] = src[...].astype(dst.dtype)


TRANS_B = (((1,), (1,)), ((), ()))


def _scaled_mm_body(a_ref, w_ref, ss_ref, *refs, relu2, w_transposed):
    n_cast = len(refs) // 2
    o_ref = refs[n_cast]
    if w_transposed:
        acc = lax.dot_general(a_ref[...], w_ref[...], TRANS_B, preferred_element_type=F32)
    else:
        acc = jnp.dot(a_ref[...], w_ref[...], preferred_element_type=F32)
    acc = acc * _row_scale(ss_ref, a_ref.shape[1])
    if relu2:
        acc = jnp.square(jnp.maximum(acc, 0.0))
    o_ref[...] = acc.astype(o_ref.dtype)
    _cast_slabs(refs[:n_cast], refs[n_cast + 1:])


def scaled_matmul(hb, ss, w, out_dtype, name, relu2=False, w_transposed=False, tm=1024, tn=512, cast=()):
    m, k = hb.shape
    n = w.shape[0] if w_transposed else w.shape[1]
    nj = n // tn
    cast_specs, cast_shapes = _cast_specs(cast, (m // tm) * nj, lambda i, j: i * nj + j)
    w_spec = pl.BlockSpec((tn, k), lambda i, j: (j, 0)) if w_transposed else pl.BlockSpec((k, tn), lambda i, j: (0, j))
    outs = pl.pallas_call(
        functools.partial(_scaled_mm_body, relu2=relu2, w_transposed=w_transposed),
        grid=(m // tm, nj),
        in_specs=[pl.BlockSpec((tm, k), lambda i, j: (i, 0)), w_spec,
                  pl.BlockSpec((tm, LANES), lambda i, j: (i, 0))] + cast_specs,
        out_specs=[pl.BlockSpec((tm, tn), lambda i, j: (i, j))] + cast_specs,
        out_shape=[jax.ShapeDtypeStruct((m, n), out_dtype)] + cast_shapes,
        compiler_params=_params("parallel", "parallel"),
        name=name,
    )(hb, w, ss, *cast)
    return outs[0], outs[1:]


def _front_body(x_ref, g_ref, wat_ref, wt_lo_ref, wt_hi_ref, pa_ref, xb_ref, ss_ref, wht_ref):
    x = x_ref[...]
    xb = (x * g_ref[...]).astype(BF16)
    xb_ref[...] = xb
    ss = jnp.sum(x * x, axis=-1, keepdims=True)
    ss_ref[...] = jnp.broadcast_to(ss, ss_ref.shape)
    acc = lax.dot_general(xb, wat_ref[...], TRANS_B, preferred_element_type=F32)
    pa_ref[...] = acc * lax.rsqrt(ss / x.shape[1] + EPS)
    half = wt_lo_ref.shape[0]
    wht_ref[:half, :] = wt_lo_ref[...].astype(BF16)
    wht_ref[half:, :] = wt_hi_ref[...].astype(BF16)


def front(x, gain, w_at, w_t, h_row0, tm=256):
    m, d = x.shape
    n = w_at.shape[0]
    steps = m // tm
    h_rows = w_t.shape[0] - h_row0
    half = h_rows // (2 * steps)
    assert 2 * half * steps == h_rows and half % 16 == 0 and h_row0 % half == 0
    first = h_row0 // half
    row = lambda i: (i, 0)
    return pl.pallas_call(
        _front_body,
        grid=(steps,),
        in_specs=[pl.BlockSpec((tm, d), row), pl.BlockSpec((1, d), lambda i: (0, 0)),
                  pl.BlockSpec((n, d), lambda i: (0, 0), pipeline_mode=pl.Buffered(1)),
                  pl.BlockSpec((half, d), lambda i: (first + 2 * i, 0)),
                  pl.BlockSpec((half, d), lambda i: (first + 2 * i + 1, 0))],
        out_specs=[pl.BlockSpec((tm, n), row), pl.BlockSpec((tm, d), row), pl.BlockSpec((tm, LANES), row),
                   pl.BlockSpec((2 * half, d), row)],
        out_shape=[jax.ShapeDtypeStruct((m, n), F32), jax.ShapeDtypeStruct((m, d), BF16),
                   jax.ShapeDtypeStruct((m, LANES), F32), jax.ShapeDtypeStruct((h_rows, d), BF16)],
        compiler_params=_params("parallel"),
        name="front",
    )(x, gain.reshape(1, d), w_at, w_t, w_t)


def _norm_outputs(m, n, tm, tn, index):
    specs = [pl.BlockSpec((tm, tn), index), pl.BlockSpec((tm, tn), index),
             pl.BlockSpec((tm, LANES), lambda i, *_: (i, 0))]
    shapes = [jax.ShapeDtypeStruct((m, n), F32), jax.ShapeDtypeStruct((m, n), BF16),
              jax.ShapeDtypeStruct((m, LANES), F32)]
    return specs, shapes


def _out_proj_body(a1_ref, a2_ref, w1_ref, w2_ref, r_ref, g_ref, o_ref, hb_ref, ss_ref):
    @pl.when(pl.program_id(1) == 0)
    def _():
        ss_ref[...] = jnp.zeros_like(ss_ref)

    tm = o_ref.shape[0]
    half = tm // 2
    for rows in (slice(0, half), slice(half, tm)):
        acc = jnp.dot(a1_ref[rows, :], w1_ref[...], preferred_element_type=F32)
        acc = acc + jnp.dot(a2_ref[rows, :], w2_ref[...], preferred_element_type=F32)
        h = r_ref[rows, :] + acc
        o_ref[rows, :] = h
        hb_ref[rows, :] = (h * g_ref[...]).astype(BF16)
        ss_ref[rows, :] += jnp.sum(h * h, axis=-1, keepdims=True)


def out_proj_residual(a1, a2, w, res, gain, tm=1024, tn=512):
    m, k1 = a1.shape
    k2 = a2.shape[1]
    assert k1 == k2 and w.shape[0] == k1 + k2
    n = w.shape[1]
    tile = lambda i, j: (i, j)
    out_specs, out_shape = _norm_outputs(m, n, tm, tn, tile)
    return pl.pallas_call(
        _out_proj_body,
        grid=(m // tm, n // tn),
        in_specs=[pl.BlockSpec((tm, k1), lambda i, j: (i, 0)), pl.BlockSpec((tm, k2), lambda i, j: (i, 0)),
                  pl.BlockSpec((k1, tn), lambda i, j: (0, j)), pl.BlockSpec((k2, tn), lambda i, j: (1, j)),
                  pl.BlockSpec((tm, tn), tile), pl.BlockSpec((1, tn), lambda i, j: (0, j))],
        out_specs=out_specs,
        out_shape=out_shape,
        compiler_params=_params("parallel", "arbitrary"),
        name="out_proj",
    )(a1, a2, w, w, res, gain.reshape(1, n))


def _mlp_down_body(a_ref, w_ref, r_ref, g_ref, o_ref, hb_ref, ss_ref, acc_ref):
    k = pl.program_id(2)
    last = pl.num_programs(2) - 1

    def partial_product():
        return jnp.dot(a_ref[...], w_ref[...], preferred_element_type=F32)

    @pl.when(k == 0)
    def _():
        acc_ref[...] = r_ref[...] + partial_product()

    @pl.when(jnp.logical_and(k > 0, k < last))
    def _():
        acc_ref[...] += partial_product()

    @pl.when(k == last)
    def _():
        h = acc_ref[...] + partial_product()
        o_ref[...] = h
        _emit_norm_inputs(h, g_ref, hb_ref, ss_ref, pl.program_id(1) == 0)


def mlp_down_residual(a, w, res, gain, tm=1024, tn=1024, tk=2048):
    m, k = a.shape
    n = w.shape[1]
    assert k // tk >= 2
    tile = lambda i, j, kk: (i, j)
    out_specs, out_shape = _norm_outputs(m, n, tm, tn, tile)
    return pl.pallas_call(
        _mlp_down_body,
        grid=(m // tm, n // tn, k // tk),
        in_specs=[pl.BlockSpec((tm, tk), lambda i, j, kk: (i, kk)), pl.BlockSpec((tk, tn), lambda i, j, kk: (kk, j)),
                  pl.BlockSpec((tm, tn), tile), pl.BlockSpec((1, tn), lambda i, j, kk: (0, j))],
        out_specs=out_specs,
        out_shape=out_shape,
        scratch_shapes=[pltpu.VMEM((tm, tn), F32)],
        compiler_params=_params("parallel", "arbitrary", "arbitrary"),
        name="mlp_down",
    )(a, w, res, gain.reshape(1, n))


def _gate_body(a_ref, w_ref, ss_ref, h_ref, p_ref, we_ref, ge_ref, sse_ref, o_ref):
    tm, d = a_ref.shape
    half = tm // 2
    for rows in (slice(0, half), slice(half, tm)):
        scale = lax.rsqrt(ss_ref[rows, :1] / d + EPS)
        acc = jnp.dot(a_ref[rows, :], w_ref[...], preferred_element_type=F32) * scale
        gate = 0.5 + 0.5 * jnp.tanh(0.5 * acc)
        y = jnp.dot(p_ref[rows, :].astype(BF16), we_ref[...], preferred_element_type=F32)
        e = y * lax.rsqrt(sse_ref[rows, :1] / d + EPS) * ge_ref[...]
        o_ref[rows, :] = h_ref[rows, :] + gate * e


def gate_residual(hb, ss, w, h, p, w_e, g_e, ss_e, tm=1024, tn=512):
    m, k = hb.shape
    n = w.shape[1]
    assert w_e.shape[1] == n == k
    return pl.pallas_call(
        _gate_body,
        grid=(m // tm, n // tn),
        in_specs=[pl.BlockSpec((tm, k), lambda i, j: (i, 0)), pl.BlockSpec((k, tn), lambda i, j: (0, j)),
                  pl.BlockSpec((tm, LANES), lambda i, j: (i, 0)), pl.BlockSpec((tm, tn), lambda i, j: (i, j)),
                  pl.BlockSpec((tm, p.shape[1]), lambda i, j: (i, 0)), pl.BlockSpec((p.shape[1], tn), lambda i, j: (0, j)),
                  pl.BlockSpec((1, tn), lambda i, j: (0, j)), pl.BlockSpec((tm, LANES), lambda i, j: (i, 0))],
        out_specs=pl.BlockSpec((tm, tn), lambda i, j: (i, j)),
        out_shape=jax.ShapeDtypeStruct((m, n), F32),
        compiler_params=_params("parallel", "parallel"),
        name="ple_gate",
    )(hb, w, ss, h, p, w_e, g_e.reshape(1, n), ss_e)


def _rmsnorm_body(x_ref, g_ref, o_ref):
    o_ref[...] = _rms(x_ref[...], g_ref[...]).astype(o_ref.dtype)


def rmsnorm(x, g, out_dtype, tm=256):
    n, d = x.shape
    return pl.pallas_call(
        _rmsnorm_body,
        grid=(n // tm,),
        in_specs=[pl.BlockSpec((tm, d), lambda i: (i, 0)), pl.BlockSpec((1, d), lambda i: (0, 0))],
        out_specs=pl.BlockSpec((tm, d), lambda i: (i, 0)),
        out_shape=jax.ShapeDtypeStruct((n, d), out_dtype),
        compiler_params=_params("parallel"),
        name="rmsnorm",
    )(x, g.reshape(1, d))


def _embed_stats_body(p_ref, w_ref, ss_ref):
    y = jnp.dot(p_ref[...].astype(BF16), w_ref[...], preferred_element_type=F32)
    ss_ref[...] = jnp.broadcast_to(jnp.sum(y * y, axis=-1, keepdims=True), ss_ref.shape)


def embed_row_sumsq(p, w, tm=256):
    m, k = p.shape
    n = w.shape[1]
    return pl.pallas_call(
        _embed_stats_body,
        grid=(m // tm,),
        in_specs=[pl.BlockSpec((tm, k), lambda i: (i, 0)), pl.BlockSpec((k, n), lambda i: (0, 0))],
        out_specs=pl.BlockSpec((tm, LANES), lambda i: (i, 0)),
        out_shape=jax.ShapeDtypeStruct((m, LANES), F32),
        compiler_params=_params("parallel"),
        name="ple_embed_stats",
    )(p, w)


def _mla_prep_body(pa_ref, pos_ref, gq_ref, gkv_ref, wq_ref, wk_ref, wvt_ref, ones_ref, freq_ref, sign_ref, q_ref, k_ref,
                   vt_ref):
    pa = pa_ref[...]
    cq = _rms(pa[:, :Q_LORA], gq_ref[...]).astype(BF16)
    ckv = _rms(pa[:, Q_LORA:Q_LORA + KV_LORA], gkv_ref[...]).astype(BF16)
    kr = pa[:, Q_LORA + KV_LORA:Q_LORA + KV_LORA + LANES]
    krs = pa[:, Q_LORA + KV_LORA + LANES:]
    ang = pos_ref[...].astype(F32) * freq_ref[...]
    cos = jnp.cos(ang)
    sin = jnp.sin(ang) * sign_ref[...]
    q = jnp.dot(cq, wq_ref[...], preferred_element_type=F32)
    kn = jnp.dot(ckv, wk_ref[...], preferred_element_type=F32)
    vt_ref[...] = (lax.dot_general(wvt_ref[...], ckv, TRANS_B, preferred_element_type=F32) + ones_ref[...]).astype(BF16)
    krot = (kr * cos + krs * sin).astype(BF16)
    qscale = (QK_HEAD ** -0.5) * LOG2_E
    hn = MLA_HEADS * LANES
    hr = MLA_HEADS * QK_ROPE
    low = lax.broadcasted_iota(jnp.int32, cos.shape, 1) < QK_ROPE
    for h in range(MLA_HEADS):
        lo, hi = h * LANES, (h + 1) * LANES
        if h % 2 == 0:
            g0 = hn + (h // 2) * LANES
            pair = (q[:, g0:g0 + LANES] * cos + q[:, hr + g0:hr + g0 + LANES] * sin) * qscale
            qrot = pair
        else:
            qrot = pltpu.roll(pair, QK_ROPE, axis=1)
        q_ref[:, h * QK_PAD:h * QK_PAD + LANES] = (q[:, lo:hi] * qscale).astype(BF16)
        q_ref[:, h * QK_PAD + LANES:(h + 1) * QK_PAD] = jnp.where(low, qrot, 0.0).astype(BF16)
        k_ref[:, h * QK_PAD:h * QK_PAD + LANES] = kn[:, lo:hi].astype(BF16)
        k_ref[:, h * QK_PAD + LANES:(h + 1) * QK_PAD] = krot


def mla_prep(proj, pos, gq, gkv, wq, wk, wvt, freq, sign, tm=256):
    n = proj.shape[0]
    const = lambda i: (0, 0)
    ones_rows = np.zeros((MLA_HEADS, VT_ROWS, 1), np.float32)
    ones_rows[:, V_HEAD, 0] = 1.0
    ones_rows = jnp.asarray(ones_rows.reshape(MLA_HEADS * VT_ROWS, 1))
    return pl.pallas_call(
        _mla_prep_body,
        grid=(n // tm,),
        in_specs=[pl.BlockSpec((tm, A_WIDTH), lambda i: (i, 0)), pl.BlockSpec((tm, 1), lambda i: (i, 0)),
                  pl.BlockSpec((1, Q_LORA), const), pl.BlockSpec((1, KV_LORA), const),
                  pl.BlockSpec(wq.shape, const), pl.BlockSpec(wk.shape, const), pl.BlockSpec(wvt.shape, const),
                  pl.BlockSpec(ones_rows.shape, const), pl.BlockSpec((1, LANES), const), pl.BlockSpec((1, LANES), const)],
        out_specs=[pl.BlockSpec((tm, MLA_HEADS * QK_PAD), lambda i: (i, 0)),
                   pl.BlockSpec((tm, MLA_HEADS * QK_PAD), lambda i: (i, 0)),
                   pl.BlockSpec((MLA_HEADS * VT_ROWS, tm), lambda i: (0, i))],
        out_shape=[jax.ShapeDtypeStruct((n, MLA_HEADS * QK_PAD), BF16),
                   jax.ShapeDtypeStruct((n, MLA_HEADS * QK_PAD), BF16),
                   jax.ShapeDtypeStruct((MLA_HEADS * VT_ROWS, n), BF16)],
        compiler_params=_params("parallel"),
        name="mla_prep",
    )(proj, pos, gq.reshape(1, -1), gkv.reshape(1, -1), wq, wk, wvt, ones_rows, freq, sign)


def _attn_body(q_ref, k_ref, vt_ref, *refs, tq, tk, n_cast):
    o_ref = refs[n_cast]
    s_ref, mx_ref, m_ref, acc_ref = refs[2 * n_cast + 1:]
    _cast_slabs(refs[:n_cast], refs[n_cast + 1:2 * n_cast + 1])
    seq = q_ref.shape[0]
    n_diag = tq // tk
    gw = 2 * LANES
    tiles = [(qi, j) for qi in range(seq // tq) for j in range((qi + 1) * n_diag)]

    def keys_needed(qi, j, c):
        diag = j - qi * n_diag
        return tk if diag < 0 else max(0, min(tk, c + gw - diag * tk))

    def produce(slot, qi, j):
        for c in range(0, tq, gw):
            nk = keys_needed(qi, j, c)
            if nk == 0:
                continue
            q = q_ref[qi * tq + c:qi * tq + c + gw, :]
            s = lax.dot_general(k_ref[j * tk:j * tk + nk, :], q, (((1,), (1,)), ((), ())),
                                preferred_element_type=F32)
            s_ref[slot, :nk, c:c + gw] = s
            mx_ref[slot, :, c:c + gw] = jnp.max(s, axis=0, keepdims=True)

    def consume(slot, qi, j):
        for c in range(0, tq, gw):
            cols = slice(c, c + gw)
            nk = keys_needed(qi, j, c)
            if nk == 0:
                continue
            s = s_ref[slot, :nk, cols]
            first_key = (j - qi * n_diag) * tk
            if first_key + nk - 1 <= c:
                tile_max = mx_ref[slot, :, cols]
            else:
                key = first_key + lax.broadcasted_iota(jnp.int32, (nk, gw), 0)
                qry = c + lax.broadcasted_iota(jnp.int32, (nk, gw), 1)
                s = jnp.where(key <= qry, s, -jnp.inf)
                tile_max = jnp.max(s, axis=0, keepdims=True)
            vt = vt_ref[:, j * tk:j * tk + nk]
            if j == 0:
                m_new = tile_max
                p = jnp.exp2(s - m_new)
                acc_ref[:, cols] = jnp.dot(vt, p.astype(BF16), preferred_element_type=F32)
            else:
                m = m_ref[:, cols]
                m_new = jnp.maximum(m, tile_max)
                alpha = jnp.exp2(m - m_new)
                p = jnp.exp2(s - m_new)
                acc_ref[:, cols] = alpha * acc_ref[:, cols] + jnp.dot(vt, p.astype(BF16), preferred_element_type=F32)
            m_ref[:, cols] = m_new

    produce(0, *tiles[0])
    for t, (qi, j) in enumerate(tiles):
        if t + 1 < len(tiles):
            produce((t + 1) % 2, *tiles[t + 1])
        consume(t % 2, qi, j)
        if j == (qi + 1) * n_diag - 1:
            acc = acc_ref[...]
            o_ref[qi * tq:(qi + 1) * tq, :] = (acc[:V_HEAD] / acc[V_HEAD:V_HEAD + 1]).T.astype(o_ref.dtype)


def causal_attention(q, k, vt, batch, seq, tq=1024, tk=1024, cast=()):
    cast_specs, cast_shapes = _cast_specs(cast, batch * MLA_HEADS, lambda b, h: b * MLA_HEADS + h)
    outs = pl.pallas_call(
        functools.partial(_attn_body, tq=tq, tk=tk, n_cast=len(cast)),
        grid=(batch, MLA_HEADS),
        in_specs=[pl.BlockSpec((seq, QK_PAD), lambda b, h: (b, h)), pl.BlockSpec((seq, QK_PAD), lambda b, h: (b, h)),
                  pl.BlockSpec((VT_ROWS, seq), lambda b, h: (h, b))] + cast_specs,
        out_specs=[pl.BlockSpec((seq, V_HEAD), lambda b, h: (b, h))] + cast_specs,
        out_shape=[jax.ShapeDtypeStruct((batch * seq, MLA_HEADS * V_HEAD), BF16)] + cast_shapes,
        scratch_shapes=[pltpu.VMEM((2, tk, tq), F32), pltpu.VMEM((2, 1, tq), F32), pltpu.VMEM((1, tq), F32),
                        pltpu.VMEM((VT_ROWS, tq), F32)],
        compiler_params=_params("parallel", "parallel"),
        name="mla_attention",
    )(q, k, vt, *cast)
    return outs[0], outs[1:]


def _hgrn_constants(c):
    t = np.arange(c)
    ltri = (t[None, :] <= t[:, None]).astype(np.float32)
    nlev = int(np.log2(c))
    x = t[:, None] ^ t[None, :]
    level = np.full((c, c), -1, np.int32)
    lower = t[:, None] > t[None, :]
    level[lower] = (nlev - 1) - np.floor(np.log2(x[lower])).astype(np.int32)
    level[t, t] = nlev
    return np.concatenate([ltri, ltri, ltri], axis=1), level, nlev


def _level_exponent(b, half):
    c, dk = b.shape
    if half >= 8:
        x = b.reshape(c // (2 * half), 2 * half, dk)
        ref = x[:, half - 1:half, :]
        return jnp.concatenate([ref - x[:, :half, :], x[:, half:, :] - ref], axis=1).reshape(c, dk)
    x = b.reshape(c // 8, 8, dk)
    if half == 4:
        ref = jnp.broadcast_to(x[:, 3:4, :], x.shape)
    else:
        assert half == 2
        sub = lax.broadcasted_iota(jnp.int32, x.shape, 1)
        ref = jnp.where(sub < 4, x[:, 1:2, :], x[:, 5:6, :])
    return (-jnp.abs(x - ref)).reshape(c, dk)


def _hgrn_body(hq_ref, hf_ref, hi_ref, hg_ref, lbp_ref, gn_ref, w3_ref, lev_ref, o_ref, st_ref, *, layer, tile, chunk, nlev):
    @pl.when(pl.program_id(2) == 0)
    def _():
        st_ref[...] = jnp.zeros_like(st_ref)

    hb = lbp_ref[...]
    ex = jnp.exp(hb - jnp.max(hb, axis=0, keepdims=True))
    sm = ex / jnp.sum(ex, axis=0, keepdims=True)
    lb = jnp.sum(sm[:layer + 1], axis=0, keepdims=True)
    oml = 1.0 - lb
    lev = lev_ref[...]
    gn = gn_ref[...]
    trans_b = (((1,), (1,)), ((), ()))
    chunks = [slice(c * chunk, (c + 1) * chunk) for c in range(tile // chunk)]
    fs, qs, kks, vs, bs = [], [], [], [], []
    for rows in chunks:
        z = hf_ref[rows, :]
        t = jnp.exp(-jnp.abs(z))
        big = 1.0 / (1.0 + t)
        small = t * big
        pos = z >= 0.0
        f = lb + oml * jnp.where(pos, big, small)
        g = jnp.log(f)
        hq = hq_ref[rows, :]
        fs.append(f)
        kks.append(oml * jnp.where(pos, small, big))
        qs.append(hq * jax.nn.sigmoid(hq))
        vs.append(hi_ref[rows, :].astype(BF16))
        g_hi = g.astype(BF16)
        r1 = g - g_hi.astype(F32)
        g_mid = r1.astype(BF16)
        g_lo = (r1 - g_mid.astype(F32)).astype(BF16)
        g3 = jnp.concatenate([g_hi, g_mid, g_lo], axis=0)
        bs.append(jnp.dot(w3_ref[...], g3, preferred_element_type=F32))
    scores = [jnp.zeros((chunk, chunk), F32) for _ in chunks]
    q16 = [q.astype(BF16) for q in qs]
    k16 = [kk.astype(BF16) for kk in kks]
    for lv in range(nlev + 1):
        half = chunk >> (lv + 1)
        for c, (f, q, kk, b) in enumerate(zip(fs, q16, k16, bs)):
            if half >= 2:
                e = jnp.exp(_level_exponent(b, half)).astype(BF16)
                ql, kl = q * e, kk * e
            elif half == 1:
                ql, kl = q * f.astype(BF16), kk
            else:
                ql, kl = q, kk
            p = lax.dot_general(ql, kl, trans_b, preferred_element_type=F32)
            scores[c] = jnp.where(lev == lv, p, scores[c])
    intra, updates, decays, qbs = [], [], [], []
    for a, q, kk, v, b in zip(scores, qs, kks, vs, bs):
        intra.append(jnp.dot(a.astype(BF16), v, preferred_element_type=F32))
        b_last = b[chunk - 1:chunk, :]
        kh = (kk * jnp.exp(b_last - b)).astype(BF16)
        updates.append(lax.dot_general(v, kh, (((0,), (0,)), ((), ())), preferred_element_type=F32))
        decays.append(jnp.exp(b_last))
        qbs.append((q * jnp.exp(b)).astype(BF16))
    state = st_ref[...]
    for rows, qb, o_intra, upd, dec in zip(chunks, qbs, intra, updates, decays):
        o = lax.dot_general(qb, state.astype(BF16), trans_b, preferred_element_type=F32) + o_intra
        state = dec * state + upd
        hg = hg_ref[rows, :]
        y = _rms(o, gn) * (hg * jax.nn.sigmoid(hg))
        o_ref[rows, :] = y.astype(o_ref.dtype)
    st_ref[...] = state


def hgrn2(proj, col0, lbp, gn, batch, seq, layer, tile=2048):
    w3, level, nlev = _hgrn_constants(HG_CHUNK)
    w3 = jnp.asarray(w3, BF16)
    level = jnp.asarray(level)
    nt = seq // tile
    cb = col0 // LANES

    def col(group):
        return lambda b, h, i: (b * nt + i, cb + group * HG_HEADS + h)

    return pl.pallas_call(
        functools.partial(_hgrn_body, layer=layer, tile=tile, chunk=HG_CHUNK, nlev=nlev),
        grid=(batch, HG_HEADS, nt),
        in_specs=[pl.BlockSpec((tile, LANES), col(0)), pl.BlockSpec((tile, LANES), col(1)),
                  pl.BlockSpec((tile, LANES), col(2)), pl.BlockSpec((tile, LANES), col(3)),
                  pl.BlockSpec((lbp.shape[0], LANES), lambda b, h, i: (0, h)),
                  pl.BlockSpec((1, LANES), lambda b, h, i: (0, h)),
                  pl.BlockSpec(w3.shape, lambda b, h, i: (0, 0)),
                  pl.BlockSpec(level.shape, lambda b, h, i: (0, 0))],
        out_specs=pl.BlockSpec((tile, LANES), lambda b, h, i: (b * nt + i, h)),
        out_shape=jax.ShapeDtypeStruct((batch * seq, HG_VDIM), BF16),
        scratch_shapes=[pltpu.VMEM((HG_DV, HG_DK), F32)],
        compiler_params=_params("parallel", "parallel", "arbitrary"),
        name="hgrn2",
    )(proj, proj, proj, proj, lbp, gn.reshape(1, -1), w3, level)


def _swap_halves(w):
    half = w.shape[-1] // 2
    return jnp.concatenate([w[..., half:], w[..., :half]], axis=-1)


def _mla_in_weight_body(wt_ref, o_ref):
    n_a = Q_LORA + KV_LORA
    half = QK_ROPE // 2
    top = wt_ref[...].astype(BF16)
    o_ref[...] = jnp.zeros_like(o_ref)
    o_ref[:n_a + QK_ROPE, :] = top
    o_ref[n_a + LANES:n_a + LANES + half, :] = top[n_a + half:]
    o_ref[n_a + LANES + half:n_a + LANES + QK_ROPE, :] = top[n_a:n_a + half]


def mla_in_weight(w_in_t, tn=512):
    d = w_in_t.shape[1]
    rows = Q_LORA + KV_LORA + QK_ROPE
    return pl.pallas_call(
        _mla_in_weight_body,
        grid=(d // tn,),
        in_specs=[pl.BlockSpec((rows, tn), lambda j: (0, j))],
        out_specs=pl.BlockSpec((A_WIDTH, tn), lambda j: (0, j)),
        out_shape=jax.ShapeDtypeStruct((A_WIDTH, d), BF16),
        compiler_params=_params("parallel"),
        name="mla_in_weight",
    )(w_in_t)


def _layer_weights(w_uq, w_ukv):
    wq = w_uq.reshape(Q_LORA, MLA_HEADS, QK_HEAD)
    rope = wq[:, :, QK_NOPE:]
    wq_all = jnp.concatenate([wq[:, :, :QK_NOPE].reshape(Q_LORA, -1), rope.reshape(Q_LORA, -1),
                              _swap_halves(rope).reshape(Q_LORA, -1)], axis=1)
    wkv = w_ukv.reshape(KV_LORA, MLA_HEADS, QK_NOPE + V_HEAD)
    wk = wkv[:, :, :QK_NOPE].reshape(KV_LORA, -1)
    wv = jnp.pad(wkv[:, :, QK_NOPE:], ((0, 0), (0, 0), (0, VT_ROWS - V_HEAD)))
    wvt = wv.reshape(KV_LORA, -1).T
    return wq_all.astype(BF16), wk.astype(BF16), wvt.astype(BF16)


def _rope_rows():
    inv_freq = ROPE_THETA ** (-jnp.arange(0, QK_ROPE, 2, dtype=F32) / QK_ROPE)
    half = jnp.ones((QK_ROPE // 2,), F32)
    reps = LANES // QK_ROPE
    freq = jnp.concatenate([inv_freq, inv_freq] * reps).reshape(1, LANES)
    sign = jnp.concatenate([-half, half] * reps).reshape(1, LANES)
    return freq, sign


def kernel(x, p, positions, norm_mix, w_in, q_a_norm, kv_a_norm, w_uq, w_ukv, hg_lower_bound, hg_out_norm, w_o,
           norm_mlp, w_up, w_down, norm_ple, w_ple_gate, w_ple, ple_post_norm, final_norm):
    batch, seq, d_model = x.shape
    n = batch * seq
    depth = w_in.shape[0]
    h = x.reshape(n, d_model)
    pos = positions.reshape(n, 1)
    freq, sign = _rope_rows()
    for i in range(depth):
        w_in_t = w_in[i].T
        wq_all, wk, wvt = _layer_weights(w_uq[i], w_ukv[i])
        proj_a, xb, ss, w_ht = front(h, norm_mix[i], mla_in_weight(w_in_t), w_in_t, Q_LORA + KV_LORA + QK_ROPE)
        proj_h, (w_up_b,) = scaled_matmul(xb, ss, w_ht, F32, "in_proj_h", w_transposed=True, tn=1024,
                                          cast=(w_up[i],))
        q, k, vt = mla_prep(proj_a, pos, q_a_norm[i], kv_a_norm[i], wq_all, wk, wvt, freq, sign)
        o_mla, (w_o_b, w_pg_b) = causal_attention(q, k, vt, batch, seq, cast=(w_o[i], w_ple_gate[i]))
        o_hg = hgrn2(proj_h, 0, hg_lower_bound, hg_out_norm[i], batch, seq, i)
        h, hb, ss = out_proj_residual(o_mla, o_hg, w_o_b, h, norm_mlp[i])
        hidden, (w_down_b,) = scaled_matmul(hb, ss, w_up_b, BF16, "mlp_up", relu2=True, tn=1024, cast=(w_down[i],))
        h, hb, ss = mlp_down_residual(hidden, w_down_b, h, norm_ple[i])
        p_i, w_e = p[i].reshape(n, -1), w_ple[i].astype(BF16)
        h = gate_residual(hb, ss, w_pg_b, h, p_i, w_e, ple_post_norm[i], embed_row_sumsq(p_i, w_e))
    return rmsnorm(h, final_norm, x.dtype).reshape(batch, seq, d_model)
```
